```python
import jax, jax.numpy as jnp
from jax import lax
import numpy as np

D_MODEL = 1024
BATCH = 16
SEQ = 2048
DEPTH = 4

N_MIXERS = 3
GRID_W = 64
PLE_DIM = 256
EPS = 1e-6
BLOCK_Q = 128

MLA_HEADS = 16
MLA_NOPE = 64
MLA_ROPE = 32
MLA_V = 64
MLA_Q_RANK = 256
MLA_KV_RANK = 128
MLA_THETA = 10000.0

SWA_HEADS = 16
SWA_KV_HEADS = 4
SWA_HEAD_DIM = 64
SWA_WINDOW = 128

AX_HEADS = 8
AX_KV_HEADS = 4
AX_HEAD_DIM = 128
AX_THETA = 10000.0

FFN_DIM = 2816
N_EXPERTS = 8
TOP_K = 2
EXPERT_DIM = 3584

N_MLA = (DEPTH + 2) // 3
N_SWA = (DEPTH + 1) // 3
N_AX = DEPTH // 3
N_DENSE = (DEPTH + 1) // 2
N_MOE = DEPTH // 2

kernel_name = 'hybrid_mla_swa_axial_moe_encoder'


def rms_norm(x, gain):
    xf = x.astype(jnp.float32)
    y = xf * lax.rsqrt(jnp.mean(xf * xf, axis=-1, keepdims=True) + EPS)
    return (y * gain.astype(jnp.float32)).astype(x.dtype)


def rope_tables(pos, dim, theta):
    inv = theta ** (-jnp.arange(0, dim, 2, dtype=jnp.float32) / dim)
    ang = pos.astype(jnp.float32)[:, None] * inv[None, :]
    return jnp.cos(ang), jnp.sin(ang)


def apply_rope(x, cos, sin):
    x1, x2 = jnp.split(x, 2, axis=-1)
    c = cos[:, None, :].astype(x.dtype)
    s = sin[:, None, :].astype(x.dtype)
    return jnp.concatenate([x1 * c - x2 * s, x1 * s + x2 * c], axis=-1)


def alibi_slopes(n):
    return jnp.asarray(2.0 ** (-8.0 * np.arange(1, n + 1) / n), dtype=jnp.float32)


def dense_block_attention(q, k, v, scale):
    B, S, Hq, dk = q.shape
    G = k.shape[2]
    R = Hq // G
    dv = v.shape[-1]
    nb = S // BLOCK_Q
    qb = q.reshape(B, nb, BLOCK_Q, G, R, dk).transpose(1, 0, 2, 3, 4, 5)

    def one_block(qi):
        s = jnp.einsum('bqgrd,bkgd->bgrqk', qi, k, preferred_element_type=jnp.float32) * scale
        pr = jax.nn.softmax(s, axis=-1)
        return jnp.einsum('bgrqk,bkgd->bqgrd', pr.astype(v.dtype), v)

    o = lax.map(one_block, qb)
    return o.transpose(1, 0, 2, 3, 4, 5).reshape(B, S, Hq * dv)


def windowed_attention(q, k, v, sink, scale):
    B, S, Hq, dk = q.shape
    G = k.shape[2]
    R = Hq // G
    dv = v.shape[-1]
    nb = S // BLOCK_Q
    W = SWA_WINDOW
    span = BLOCK_Q + 2 * W
    pad = ((0, 0), (W, W), (0, 0), (0, 0))
    kp = jnp.pad(k, pad)
    vp = jnp.pad(v, pad)
    qb = q.reshape(B, nb, BLOCK_Q, G, R, dk).transpose(1, 0, 2, 3, 4, 5)
    slopes = alibi_slopes(Hq).reshape(G, R)
    sink_f = sink.astype(jnp.float32).reshape(G, R)

    def one_block(args):
        j, qi = args
        start = j * BLOCK_Q
        kj = lax.dynamic_slice_in_dim(kp, start, span, axis=1)
        vj = lax.dynamic_slice_in_dim(vp, start, span, axis=1)
        t_pos = start + jnp.arange(BLOCK_Q)
        s_pos = start - W + jnp.arange(span)
        dist = jnp.abs(t_pos[:, None] - s_pos[None, :])
        valid = (dist <= W) & (s_pos[None, :] >= 0) & (s_pos[None, :] < S)
        logits = jnp.einsum('bqgrd,bkgd->bgrqk', qi, kj, preferred_element_type=jnp.float32) * scale
        logits = logits - slopes[:, :, None, None] * dist.astype(jnp.float32)
        logits = jnp.where(valid, logits, -jnp.inf)
        sink_col = jnp.broadcast_to(sink_f[None, :, :, None, None], logits.shape[:-1] + (1,))
        pr = jax.nn.softmax(jnp.concatenate([logits, sink_col], axis=-1), axis=-1)[..., :-1]
        return jnp.einsum('bgrqk,bkgd->bqgrd', pr.astype(vj.dtype), vj)

    o = lax.map(one_block, (jnp.arange(nb), qb))
    return o.transpose(1, 0, 2, 3, 4, 5).reshape(B, S, Hq * dv)


def mla_mixer(h, cos, sin, w_down, q_norm, w_uq, kv_norm, w_ukv, q_gain, k_gain, w_o):
    B, S, _ = h.shape
    down = h @ w_down
    c_q, c_kv, k_rope = jnp.split(down, [MLA_Q_RANK, MLA_Q_RANK + MLA_KV_RANK], axis=-1)
    q = (rms_norm(c_q, q_norm) @ w_uq).reshape(B, S, MLA_HEADS, MLA_NOPE + MLA_ROPE)
    kv = (rms_norm(c_kv, kv_norm) @ w_ukv).reshape(B, S, MLA_HEADS, MLA_NOPE + MLA_V)
    k_nope, v = jnp.split(kv, [MLA_NOPE], axis=-1)
    k_rope = jnp.broadcast_to(k_rope[:, :, None, :], (B, S, MLA_HEADS, MLA_ROPE))
    k = jnp.concatenate([k_nope, k_rope], axis=-1)
    q = rms_norm(q, q_gain)
    k = rms_norm(k, k_gain)
    q = jnp.concatenate([q[..., :MLA_NOPE], apply_rope(q[..., MLA_NOPE:], cos, sin)], axis=-1)
    k = jnp.concatenate([k[..., :MLA_NOPE], apply_rope(k[..., MLA_NOPE:], cos, sin)], axis=-1)
    o = dense_block_attention(q, k, v, (MLA_NOPE + MLA_ROPE) ** -0.5)
    return o @ w_o


def swa_mixer(h, w_qkv, q_gain, k_gain, sink, w_o):
    B, S, _ = h.shape
    qkv = h @ w_qkv
    q, k, v = jnp.split(qkv, [SWA_HEADS * SWA_HEAD_DIM, (SWA_HEADS + SWA_KV_HEADS) * SWA_HEAD_DIM], axis=-1)
    q = rms_norm(q.reshape(B, S, SWA_HEADS, SWA_HEAD_DIM), q_gain)
    k = rms_norm(k.reshape(B, S, SWA_KV_HEADS, SWA_HEAD_DIM), k_gain)
    v = v.reshape(B, S, SWA_KV_HEADS, SWA_HEAD_DIM)
    o = windowed_attention(q, k, v, sink, SWA_HEAD_DIM ** -0.5)
    return o @ w_o


def axial_mixer(h, cos_r, sin_r, cos_c, sin_c, w_qkv, q_gain, k_gain, w_o):
    B, S, _ = h.shape
    qkv = h @ w_qkv
    q, k, v = jnp.split(qkv, [AX_HEADS * AX_HEAD_DIM, (AX_HEADS + AX_KV_HEADS) * AX_HEAD_DIM], axis=-1)
    q = rms_norm(q.reshape(B, S, AX_HEADS, AX_HEAD_DIM), q_gain)
    k = rms_norm(k.reshape(B, S, AX_KV_HEADS, AX_HEAD_DIM), k_gain)
    v = v.reshape(B, S, AX_KV_HEADS, AX_HEAD_DIM)
    half = AX_HEAD_DIM // 2
    q = jnp.concatenate([apply_rope(q[..., :half], cos_r, sin_r), apply_rope(q[..., half:], cos_c, sin_c)], axis=-1)
    k = jnp.concatenate([apply_rope(k[..., :half], cos_r, sin_r), apply_rope(k[..., half:], cos_c, sin_c)], axis=-1)
    o = dense_block_attention(q, k, v, AX_HEAD_DIM ** -0.5)
    return o @ w_o


def swiglu(h, w_gate, w_up, w_down):
    return (jax.nn.silu(h @ w_gate) * (h @ w_up)) @ w_down


def moe_swiglu(h, w_router, b_router, w_gate, w_up, w_down):
    B, S, D = h.shape
    t = h.reshape(B * S, D)
    logits = (t @ w_router).astype(jnp.float32) + b_router.astype(jnp.float32)
    top_val, top_idx = lax.top_k(logits, TOP_K)
    top_w = jax.nn.softmax(top_val, axis=-1)
    gates = jnp.sum(jax.nn.one_hot(top_idx, N_EXPERTS, dtype=jnp.float32) * top_w[..., None], axis=1)
    out = jnp.zeros_like(t)
    for e in range(N_EXPERTS):
        y_e = swiglu(t, w_gate[e], w_up[e], w_down[e])
        out = out + gates[:, e:e + 1].astype(t.dtype) * y_e
    return out.reshape(B, S, D)


def setup_inputs(seed: int = 0) -> dict:
    key = jax.random.key(seed)
    ks = iter(jax.random.split(key, 40))

    def w(shape, fan_in):
        return jax.random.normal(next(ks), shape, jnp.float32) * (fan_in ** -0.5)

    def gain(shape):
        return 1.0 + 0.02 * jax.random.normal(next(ks), shape, jnp.float32)

    D = D_MODEL
    mla_down = MLA_Q_RANK + MLA_KV_RANK + MLA_ROPE
    mla_qk = MLA_NOPE + MLA_ROPE
    swa_qkv = (SWA_HEADS + 2 * SWA_KV_HEADS) * SWA_HEAD_DIM
    ax_qkv = (AX_HEADS + 2 * AX_KV_HEADS) * AX_HEAD_DIM
    return {
        'x': jax.random.normal(next(ks), (BATCH, SEQ, D), jnp.float32),
        'p': jax.random.normal(next(ks), (DEPTH, BATCH, SEQ, PLE_DIM), jnp.float32),
        'attn_norm': gain((DEPTH, D)),
        'ffn_norm': gain((DEPTH, D)),
        'ple_norm': gain((DEPTH, D)),
        'ple_w_in': w((DEPTH, PLE_DIM, D), PLE_DIM),
        'ple_w_gate': w((DEPTH, D, D), D),
        'mla_w_down': w((N_MLA, D, mla_down), D),
        'mla_q_norm': gain((N_MLA, MLA_Q_RANK)),
        'mla_w_uq': w((N_MLA, MLA_Q_RANK, MLA_HEADS * mla_qk), MLA_Q_RANK),
        'mla_kv_norm': gain((N_MLA, MLA_KV_RANK)),
        'mla_w_ukv': w((N_MLA, MLA_KV_RANK, MLA_HEADS * (MLA_NOPE + MLA_V)), MLA_KV_RANK),
        'mla_q_gain': gain((N_MLA, mla_qk)),
        'mla_k_gain': gain((N_MLA, mla_qk)),
        'mla_w_o': w((N_MLA, MLA_HEADS * MLA_V, D), MLA_HEADS * MLA_V),
        'swa_w_qkv': w((N_SWA, D, swa_qkv), D),
        'swa_q_gain': gain((N_SWA, SWA_HEAD_DIM)),
        'swa_k_gain': gain((N_SWA, SWA_HEAD_DIM)),
        'swa_sink': 0.5 * jax.random.normal(next(ks), (N_SWA, SWA_HEADS), jnp.float32),
        'swa_w_o': w((N_SWA, SWA_HEADS * SWA_HEAD_DIM, D), SWA_HEADS * SWA_HEAD_DIM),
        'ax_w_qkv': w((N_AX, D, ax_qkv), D),
        'ax_q_gain': gain((N_AX, AX_HEAD_DIM)),
        'ax_k_gain': gain((N_AX, AX_HEAD_DIM)),
        'ax_w_o': w((N_AX, AX_HEADS * AX_HEAD_DIM, D), AX_HEADS * AX_HEAD_DIM),
        'ffn_w_gate': w((N_DENSE, D, FFN_DIM), D),
        'ffn_w_up': w((N_DENSE, D, FFN_DIM), D),
        'ffn_w_down': w((N_DENSE, FFN_DIM, D), FFN_DIM),
        'moe_w_router': w((N_MOE, D, N_EXPERTS), D),
        'moe_b_router': 0.01 * jax.random.normal(next(ks), (N_MOE, N_EXPERTS), jnp.float32),
        'moe_w_gate': w((N_MOE, N_EXPERTS, D, EXPERT_DIM), D),
        'moe_w_up': w((N_MOE, N_EXPERTS, D, EXPERT_DIM), D),
        'moe_w_down': w((N_MOE, N_EXPERTS, EXPERT_DIM, D), EXPERT_DIM),
    }


def reference(x, p, attn_norm, ffn_norm, ple_norm, ple_w_in, ple_w_gate,
              mla_w_down, mla_q_norm, mla_w_uq, mla_kv_norm, mla_w_ukv, mla_q_gain, mla_k_gain, mla_w_o,
              swa_w_qkv, swa_q_gain, swa_k_gain, swa_sink, swa_w_o,
              ax_w_qkv, ax_q_gain, ax_k_gain, ax_w_o,
              ffn_w_gate, ffn_w_up, ffn_w_down,
              moe_w_router, moe_b_router, moe_w_gate, moe_w_up, moe_w_down):
    S = x.shape[1]
    pos = jnp.arange(S)
    ROWS = S // GRID_W
    row = jnp.repeat(jnp.arange(ROWS), GRID_W)
    col = pos % GRID_W
    cos_1d, sin_1d = rope_tables(pos, MLA_ROPE, MLA_THETA)
    cos_r, sin_r = rope_tables(row, AX_HEAD_DIM // 2, AX_THETA)
    cos_c, sin_c = rope_tables(col, AX_HEAD_DIM // 2, AX_THETA)

    for i in range(DEPTH):
        h = rms_norm(x, attn_norm[i])
        kind = i % N_MIXERS
        j = i // N_MIXERS
        if kind == 0:
            mix = mla_mixer(h, cos_1d, sin_1d, mla_w_down[j], mla_q_norm[j], mla_w_uq[j],
                            mla_kv_norm[j], mla_w_ukv[j], mla_q_gain[j], mla_k_gain[j], mla_w_o[j])
        elif kind == 1:
            mix = swa_mixer(h, swa_w_qkv[j], swa_q_gain[j], swa_k_gain[j], swa_sink[j], swa_w_o[j])
        else:
            mix = axial_mixer(h, cos_r, sin_r, cos_c, sin_c, ax_w_qkv[j], ax_q_gain[j], ax_k_gain[j], ax_w_o[j])
        x = x + mix

        h = rms_norm(x, ffn_norm[i])
        f = i // 2
        if i % 2 == 0:
            x = x + swiglu(h, ffn_w_gate[f], ffn_w_up[f], ffn_w_down[f])
        else:
            x = x + moe_swiglu(h, moe_w_router[f], moe_b_router[f], moe_w_gate[f], moe_w_up[f], moe_w_down[f])

        ple = (p[i] @ ple_w_in[i]) * jax.nn.sigmoid(rms_norm(x, ple_norm[i]) @ ple_w_gate[i])
        x = x + ple
    return x
```

```python
import functools

import numpy as np
import jax
import jax.numpy as jnp
from jax import lax
from jax.experimental import pallas as pl
from jax.experimental.pallas import tpu as pltpu

F32 = jnp.float32
BF16 = jnp.bfloat16

EPS = 1e-6
GRID_W = 64
BLOCK_Q = 128

MLA_HEADS = 16
MLA_NOPE = 64
MLA_ROPE = 32
MLA_V = 64
MLA_Q_RANK = 256
MLA_KV_RANK = 128
MLA_THETA = 10000.0
MLA_QK = MLA_NOPE + MLA_ROPE

SWA_HEADS = 16
SWA_KV_HEADS = 4
SWA_HEAD_DIM = 64
SWA_WINDOW = 128

AX_HEADS = 8
AX_KV_HEADS = 4
AX_HEAD_DIM = 128
AX_THETA = 10000.0

N_EXPERTS = 8
TOP_K = 2
N_MIXERS = 3

LANES = 128
ROW_TILE = 512
MOE_ROW_TILE = 512
ATTN_Q_TILE = 256
FFN_SUB = 256
VMEM_LIMIT = 56 * 1024 * 1024


def _cparams(*sem):
    return pltpu.CompilerParams(dimension_semantics=sem, vmem_limit_bytes=VMEM_LIMIT)


def _rms(xf, gain):
    ms = jnp.mean(xf * xf, axis=-1, keepdims=True)
    return xf * lax.rsqrt(ms + EPS) * gain


def _dot(a, b):
    return jnp.dot(a, b, preferred_element_type=F32)


def _dot_nt(a, b):
    return lax.dot_general(a, b, (((1,), (1,)), ((), ())), preferred_element_type=F32)


def _resident(shape):
    return pl.BlockSpec(shape, lambda *_: (0,) * len(shape), pipeline_mode=pl.Buffered(1))


def _norm_proj_kernel(x_ref, g_ref, w_ref, o_ref):
    h = _rms(x_ref[...], g_ref[...]).astype(BF16)
    o_ref[...] = _dot(h, w_ref[...]).astype(o_ref.dtype)


def norm_proj(x, gain, w):
    T, D = x.shape
    N = w.shape[1]
    return pl.pallas_call(
        _norm_proj_kernel,
        name="norm_proj",
        grid=(T // ROW_TILE,),
        in_specs=[
            pl.BlockSpec((ROW_TILE, D), lambda i: (i, 0)),
            _resident((1, D)),
            _resident((D, N)),
        ],
        out_specs=pl.BlockSpec((ROW_TILE, N), lambda i: (i, 0)),
        out_shape=jax.ShapeDtypeStruct((T, N), BF16),
        compiler_params=_cparams("parallel"),
    )(x, gain.reshape(1, D), w)


def _mla_proj_kernel(x_ref, g_ref, wd_ref, qn_ref, kvn_ref, wuq_ref, wukv_ref, q_ref, kv_ref, kr_ref):
    h = _rms(x_ref[...], g_ref[...]).astype(BF16)
    down = _dot(h, wd_ref[...])
    cq = _rms(down[:, :MLA_Q_RANK], qn_ref[...]).astype(BF16)
    ckv = _rms(down[:, MLA_Q_RANK:MLA_Q_RANK + MLA_KV_RANK], kvn_ref[...]).astype(BF16)
    q_ref[...] = _dot(cq, wuq_ref[...]).astype(BF16)
    kv_ref[...] = _dot(ckv, wukv_ref[...]).astype(BF16)
    kr_ref[...] = down[:, MLA_Q_RANK + MLA_KV_RANK:]


def mla_proj(x, gain, wd, qn, kvn, wuq, wukv):
    T, D = x.shape
    nd = wd.shape[1]
    nq = wuq.shape[1]
    nkv = wukv.shape[1]
    row = lambda n: pl.BlockSpec((ROW_TILE, n), lambda i: (i, 0))
    return pl.pallas_call(
        _mla_proj_kernel,
        name="mla_proj",
        grid=(T // ROW_TILE,),
        in_specs=[
            row(D),
            _resident((1, D)),
            _resident((D, nd)),
            _resident((1, MLA_Q_RANK)),
            _resident((1, MLA_KV_RANK)),
            _resident((MLA_Q_RANK, nq)),
            _resident((MLA_KV_RANK, nkv)),
        ],
        out_specs=[row(nq), row(nkv), row(LANES)],
        out_shape=[
            jax.ShapeDtypeStruct((T, nq), BF16),
            jax.ShapeDtypeStruct((T, nkv), BF16),
            jax.ShapeDtypeStruct((T, LANES), F32),
        ],
        compiler_params=_cparams("parallel"),
    )(x, gain.reshape(1, D), wd, qn.reshape(1, -1), kvn.reshape(1, -1), wuq, wukv)


def _resid_proj_kernel(x_ref, a_ref, w_ref, o_ref):
    o_ref[...] = x_ref[...] + _dot(a_ref[...], w_ref[...])


def resid_proj(x, a, w):
    T, D = x.shape
    K = a.shape[1]
    return pl.pallas_call(
        _resid_proj_kernel,
        name="resid_proj",
        grid=(T // ROW_TILE,),
        in_specs=[
            pl.BlockSpec((ROW_TILE, D), lambda i: (i, 0)),
            pl.BlockSpec((ROW_TILE, K), lambda i: (i, 0)),
            _resident((K, D)),
        ],
        out_specs=pl.BlockSpec((ROW_TILE, D), lambda i: (i, 0)),
        out_shape=jax.ShapeDtypeStruct((T, D), F32),
        compiler_params=_cparams("parallel"),
    )(x, a, w)


def _softmax_pv(s, v, extra_logit=None):
    m = jnp.max(s, axis=-1, keepdims=True)
    if extra_logit is not None:
        m = jnp.maximum(m, extra_logit)
    p = jnp.exp(s - m)
    l = jnp.sum(p, axis=-1, keepdims=True)
    if extra_logit is not None:
        l = l + jnp.exp(extra_logit - m)
    return _dot(p.astype(BF16), v) / l


def _axial_attn_kernel(q_ref, k_ref, v_ref, aq_ref, bq_ref, ak_ref, bk_ref, o_ref, k_scr, *, tq, scale):
    S = k_ref.shape[0]
    lane = lax.broadcasted_iota(jnp.int32, (1, LANES), 1)
    first = (lane % 64) < 32

    def norm_rope(xf, a, b, extra):
        c = lax.rsqrt(jnp.mean(xf * xf, axis=-1, keepdims=True) + EPS) * extra
        partner = jnp.where(first, pltpu.roll(xf, LANES - 32, 1), pltpu.roll(xf, 32, 1))
        return (xf * a + partner * b) * c

    k_scr[...] = norm_rope(k_ref[...].astype(F32), ak_ref[...], bk_ref[...], 1.0).astype(BF16)

    def body(i, carry):
        r0 = pl.multiple_of(i * tq, tq)
        rows = pl.ds(r0, tq)
        for r in range(q_ref.shape[1] // LANES):
            cols = slice(r * LANES, (r + 1) * LANES)
            q = norm_rope(q_ref[rows, cols].astype(F32), aq_ref[rows, :], bq_ref[rows, :], scale)
            s = _dot_nt(q.astype(BF16), k_scr[...])
            o_ref[rows, cols] = _softmax_pv(s, v_ref[...]).astype(o_ref.dtype)
        return carry

    lax.fori_loop(0, S // tq, body, 0)


def axial_attention(qkv, tabs, B, S):
    R = AX_HEADS // AX_KV_HEADS
    tq = min(ATTN_Q_TILE, S)
    kern = functools.partial(_axial_attn_kernel, tq=tq, scale=AX_HEAD_DIM ** -0.5)
    tab = pl.BlockSpec((S, LANES), lambda b, g: (0, 0), pipeline_mode=pl.Buffered(1))
    return pl.pallas_call(
        kern,
        name="axial_attn",
        grid=(B, AX_KV_HEADS),
        in_specs=[
            pl.BlockSpec((S, R * LANES), lambda b, g: (b, g)),
            pl.BlockSpec((S, LANES), lambda b, g: (b, AX_HEADS + g)),
            pl.BlockSpec((S, LANES), lambda b, g: (b, AX_HEADS + AX_KV_HEADS + g)),
            tab, tab, tab, tab,
        ],
        out_specs=pl.BlockSpec((S, R * LANES), lambda b, g: (b, g)),
        out_shape=jax.ShapeDtypeStruct((B * S, AX_HEADS * AX_HEAD_DIM), BF16),
        scratch_shapes=[pltpu.VMEM((S, LANES), BF16)],
        compiler_params=_cparams("parallel", "parallel"),
    )(qkv, qkv, qkv, *tabs)


def _mla_attn_kernel(q_ref, kv_ref, kr_ref, aq_ref, bq_ref, ak_ref, bk_ref, o_ref, k_scr, *, tq, scale):
    S = kv_ref.shape[0]
    lane = lax.broadcasted_iota(jnp.int32, (1, LANES), 1)
    lo = lane < MLA_NOPE
    first = lane < MLA_NOPE + MLA_ROPE // 2
    half = MLA_ROPE // 2

    def norm_rope(xf, a, b, extra):
        ms = jnp.sum(xf * xf, axis=-1, keepdims=True) * (1.0 / MLA_QK)
        c = lax.rsqrt(ms + EPS) * extra
        partner = jnp.where(first, pltpu.roll(xf, LANES - half, 1), pltpu.roll(xf, half, 1))
        return (xf * a + partner * b) * c

    kr = kr_ref[...]
    for hh in range(2):
        cols = slice(hh * LANES, (hh + 1) * LANES)
        k = jnp.where(lo, kv_ref[:, cols].astype(F32), kr)
        k_scr[hh] = norm_rope(k, ak_ref[...], bk_ref[...], 1.0).astype(BF16)

    def body(i, carry):
        r0 = pl.multiple_of(i * tq, tq)
        rows = pl.ds(r0, tq)
        outs = []
        for hh in range(2):
            cols = slice(hh * LANES, (hh + 1) * LANES)
            q = norm_rope(q_ref[rows, cols].astype(F32), aq_ref[rows, :], bq_ref[rows, :], scale)
            s = _dot_nt(q.astype(BF16), k_scr[hh])
            outs.append(_softmax_pv(s, kv_ref[:, cols]))
        o_ref[rows, :] = jnp.where(lo, pltpu.roll(outs[0], MLA_V, 1), outs[1]).astype(o_ref.dtype)
        return carry

    lax.fori_loop(0, S // tq, body, 0)


def mla_attention(q, kv, kr, tabs, B, S):
    tq = min(ATTN_Q_TILE, S)
    kern = functools.partial(_mla_attn_kernel, tq=tq, scale=MLA_QK ** -0.5)
    tab = pl.BlockSpec((S, LANES), lambda b, g: (0, 0), pipeline_mode=pl.Buffered(1))
    return pl.pallas_call(
        kern,
        name="mla_attn",
        grid=(B, MLA_HEADS // 2),
        in_specs=[
            pl.BlockSpec((S, 2 * LANES), lambda b, g: (b, g)),
            pl.BlockSpec((S, 2 * LANES), lambda b, g: (b, g)),
            pl.BlockSpec((S, LANES), lambda b, g: (b, 0)),
            tab, tab, tab, tab,
        ],
        out_specs=pl.BlockSpec((S, LANES), lambda b, g: (b, g)),
        out_shape=jax.ShapeDtypeStruct((B * S, MLA_HEADS * MLA_V), BF16),
        scratch_shapes=[pltpu.VMEM((2, S, LANES), BF16)],
        compiler_params=_cparams("parallel", "parallel"),
    )(q, kv, kr, *tabs)


def _swa_attn_kernel(slope_ref, sink_ref, q_ref, k_ref, v_ref, gq_ref, gk_ref, o_ref, k_scr, *, scale):
    S = k_ref.shape[0]
    span = BLOCK_Q + 2 * SWA_WINDOW
    n_pairs = q_ref.shape[1] // LANES
    per_group = n_pairs // 2
    pid = pl.program_id(1)
    lane = lax.broadcasted_iota(jnp.int32, (1, LANES), 1)
    lo = lane < SWA_HEAD_DIM

    def seg_norm(xf, gain):
        sq = xf * xf
        s_lo = jnp.sum(jnp.where(lo, sq, 0.0), axis=-1, keepdims=True)
        s_hi = jnp.sum(jnp.where(lo, 0.0, sq), axis=-1, keepdims=True)
        inv = 1.0 / SWA_HEAD_DIM
        c = jnp.where(lo, lax.rsqrt(s_lo * inv + EPS), lax.rsqrt(s_hi * inv + EPS))
        return xf * c * gain

    k_scr[...] = seg_norm(k_ref[...].astype(F32), gk_ref[...]).astype(BF16)

    def body(j, carry):
        r0 = pl.multiple_of(j * BLOCK_Q, BLOCK_Q)
        rows = pl.ds(r0, BLOCK_Q)
        start = pl.multiple_of(jnp.clip(r0 - SWA_WINDOW, 0, S - span), BLOCK_Q)
        kw = k_scr[pl.ds(start, span), :]
        vw = v_ref[pl.ds(start, span), :]
        t_pos = r0 + lax.broadcasted_iota(jnp.int32, (BLOCK_Q, span), 0)
        s_pos = start + lax.broadcasted_iota(jnp.int32, (BLOCK_Q, span), 1)
        dist = jnp.abs(t_pos - s_pos)
        valid = dist <= SWA_WINDOW
        distf = dist.astype(F32)
        for pb in range(n_pairs):
            e = pb // per_group
            cols = slice(pb * LANES, (pb + 1) * LANES)
            qp = seg_norm(q_ref[rows, cols].astype(F32), gq_ref[...]) * scale
            qr = pltpu.roll(qp, SWA_HEAD_DIM, 1)
            keep = lo if e == 0 else jnp.logical_not(lo)
            outs = []
            for i in range(2):
                h = pid * (2 * n_pairs) + pb * 2 + i
                qz = jnp.where(keep, qp if i == e else qr, 0.0).astype(BF16)
                s = _dot_nt(qz, kw) - slope_ref[h] * distf
                s = jnp.where(valid, s, -jnp.inf)
                o = _softmax_pv(s, vw, extra_logit=sink_ref[h])
                outs.append(o if i == e else pltpu.roll(o, SWA_HEAD_DIM, 1))
            o_ref[rows, cols] = jnp.where(lo, outs[0], outs[1]).astype(o_ref.dtype)
        return carry

    lax.fori_loop(0, S // BLOCK_Q, body, 0)


def swa_attention(qkv, slopes, sink, gq, gk, B, S):
    n_steps = SWA_KV_HEADS // 2
    qw = SWA_HEADS * SWA_HEAD_DIM // n_steps
    kbase = SWA_HEADS * SWA_HEAD_DIM // LANES
    smem = pl.BlockSpec(memory_space=pltpu.SMEM)
    gain = pl.BlockSpec((1, LANES), lambda b, g: (0, 0))
    kern = functools.partial(_swa_attn_kernel, scale=SWA_HEAD_DIM ** -0.5)
    return pl.pallas_call(
        kern,
        name="swa_attn",
        grid=(B, n_steps),
        in_specs=[
            smem, smem,
            pl.BlockSpec((S, qw), lambda b, g: (b, g)),
            pl.BlockSpec((S, LANES), lambda b, g: (b, kbase + g)),
            pl.BlockSpec((S, LANES), lambda b, g: (b, kbase + n_steps + g)),
            gain, gain,
        ],
        out_specs=pl.BlockSpec((S, qw), lambda b, g: (b, g)),
        out_shape=jax.ShapeDtypeStruct((B * S, SWA_HEADS * SWA_HEAD_DIM), BF16),
        scratch_shapes=[pltpu.VMEM((S, LANES), BF16)],
        compiler_params=_cparams("parallel", "parallel"),
    )(slopes, sink, qkv, qkv, qkv, gq, gk)


def _swiglu_accumulate(h, wg_ref, wu_ref, wd_ref, acc_ref):
    for c in range(wg_ref.shape[1] // FFN_SUB):
        sl = slice(c * FFN_SUB, (c + 1) * FFN_SUB)
        g = _dot(h, wg_ref[:, sl])
        u = _dot(h, wu_ref[:, sl])
        a = (g * jax.nn.sigmoid(g) * u).astype(BF16)
        acc_ref[...] += _dot(a, wd_ref[sl, :])


def _ffn_kernel(x_ref, g_ref, wg_ref, wu_ref, wd_ref, o_ref, acc_scr):
    x = x_ref[...]
    acc_scr[...] = x
    _swiglu_accumulate(_rms(x, g_ref[...]).astype(BF16), wg_ref, wu_ref, wd_ref, acc_scr)
    o_ref[...] = acc_scr[...]


def dense_ffn(x, gain, wg, wu, wd):
    T, D = x.shape
    Fd = wg.shape[1]
    return pl.pallas_call(
        _ffn_kernel,
        name="dense_ffn",
        grid=(T // ROW_TILE,),
        in_specs=[
            pl.BlockSpec((ROW_TILE, D), lambda i: (i, 0)),
            _resident((1, D)),
            _resident((D, Fd)),
            _resident((D, Fd)),
            _resident((Fd, D)),
        ],
        out_specs=pl.BlockSpec((ROW_TILE, D), lambda i: (i, 0)),
        out_shape=jax.ShapeDtypeStruct((T, D), F32),
        scratch_shapes=[pltpu.VMEM((ROW_TILE, D), F32)],
        compiler_params=_cparams("parallel"),
    )(x, gain.reshape(1, D), wg, wu, wd)


def _router_kernel(x_ref, g_ref, whi_ref, wlo_ref, b_ref, h_ref, idx_ref, wt_ref):
    hf = _rms(x_ref[...], g_ref[...])
    h_hi = hf.astype(BF16)
    h_lo = (hf - h_hi.astype(F32)).astype(BF16)
    h_ref[...] = h_hi
    logits = _dot(h_hi, whi_ref[...]) + _dot(h_hi, wlo_ref[...]) + _dot(h_lo, whi_ref[...]) + b_ref[...]
    lane = lax.broadcasted_iota(jnp.int32, logits.shape, 1)
    logits = jnp.where(lane < N_EXPERTS, logits, -jnp.inf)
    m1 = jnp.max(logits, axis=-1, keepdims=True)
    i1 = jnp.min(jnp.where(logits == m1, lane, LANES), axis=-1, keepdims=True)
    rest = jnp.where(lane == i1, -jnp.inf, logits)
    m2 = jnp.max(rest, axis=-1, keepdims=True)
    i2 = jnp.min(jnp.where(rest == m2, lane, LANES), axis=-1, keepdims=True)
    e2 = jnp.exp(m2 - m1)
    w1 = 1.0 / (1.0 + e2)
    w2 = e2 / (1.0 + e2)
    idx_ref[...] = jnp.where(lane == 0, i1, jnp.where(lane == 1, i2, 0))
    wt_ref[...] = jnp.where(lane == 0, w1, jnp.where(lane == 1, w2, 0.0))


def moe_router(x, gain, w_hi, w_lo, bias):
    T, D = x.shape
    row = lambda n: pl.BlockSpec((ROW_TILE, n), lambda i: (i, 0))
    return pl.pallas_call(
        _router_kernel,
        name="moe_router",
        grid=(T // ROW_TILE,),
        in_specs=[row(D), _resident((1, D)), _resident((D, LANES)), _resident((D, LANES)), _resident((1, LANES))],
        out_specs=[row(D), row(LANES), row(LANES)],
        out_shape=[
            jax.ShapeDtypeStruct((T, D), BF16),
            jax.ShapeDtypeStruct((T, LANES), jnp.int32),
            jax.ShapeDtypeStruct((T, LANES), F32),
        ],
        compiler_params=_cparams("parallel"),
    )(x, gain.reshape(1, D), w_hi, w_lo, bias)


def _moe_ffn_kernel(te_ref, tv_ref, h_ref, wg_ref, wu_ref, wd_ref, o_ref, acc_scr):
    i = pl.program_id(0)
    f = pl.program_id(1)
    last = pl.num_programs(1) - 1

    @pl.when(f == 0)
    def _():
        acc_scr[...] = jnp.zeros_like(acc_scr)

    @pl.when(tv_ref[i] > 0)
    def _():
        _swiglu_accumulate(h_ref[...], wg_ref, wu_ref, wd_ref, acc_scr)

    @pl.when(f == last)
    def _():
        o_ref[...] = acc_scr[...].astype(o_ref.dtype)


def moe_ffn(h_sorted, tile_expert, tile_valid, wg, wu, wd, n_chunks):
    R, D = h_sorted.shape
    E, _, Fe = wg.shape
    tf = Fe // n_chunks
    grid_spec = pltpu.PrefetchScalarGridSpec(
        num_scalar_prefetch=2,
        grid=(R // MOE_ROW_TILE, n_chunks),
        in_specs=[
            pl.BlockSpec((MOE_ROW_TILE, D), lambda i, f, te, tv: (i, 0)),
            pl.BlockSpec((None, D, tf), lambda i, f, te, tv: (te[i], 0, f)),
            pl.BlockSpec((None, D, tf), lambda i, f, te, tv: (te[i], 0, f)),
            pl.BlockSpec((None, tf, D), lambda i, f, te, tv: (te[i], f, 0)),
        ],
        out_specs=pl.BlockSpec((MOE_ROW_TILE, D), lambda i, f, te, tv: (i, 0)),
        scratch_shapes=[pltpu.VMEM((MOE_ROW_TILE, D), F32)],
    )
    return pl.pallas_call(
        _moe_ffn_kernel,
        name="moe_ffn",
        grid_spec=grid_spec,
        out_shape=jax.ShapeDtypeStruct((R, D), BF16),
        compiler_params=_cparams("parallel", "arbitrary"),
    )(tile_expert, tile_valid, h_sorted, wg, wu, wd)


def _ple_kernel(*refs, with_moe):
    if with_moe:
        x_ref, y0_ref, y1_ref, wt_ref, p_ref, g_ref, win_ref, wgate_ref, o_ref = refs
        wt = wt_ref[...]
        x = x_ref[...] + wt[:, 0:1] * y0_ref[...].astype(F32) + wt[:, 1:2] * y1_ref[...].astype(F32)
    else:
        x_ref, p_ref, g_ref, win_ref, wgate_ref, o_ref = refs
        x = x_ref[...]
    gate = jax.nn.sigmoid(_dot(_rms(x, g_ref[...]).astype(BF16), wgate_ref[...]))
    o_ref[...] = x + _dot(p_ref[...].astype(BF16), win_ref[...]) * gate


def ple_update(x, p, gain, w_in, w_gate, moe=None):
    T, D = x.shape
    P = p.shape[1]
    row = lambda n: pl.BlockSpec((ROW_TILE, n), lambda i: (i, 0))
    ins = [x]
    specs = [row(D)]
    if moe is not None:
        y0, y1, wt = moe
        ins += [y0, y1, wt]
        specs += [row(D), row(D), row(LANES)]
    ins += [p, gain.reshape(1, D), w_in, w_gate]
    specs += [row(P), _resident((1, D)), _resident((P, D)), _resident((D, D))]
    return pl.pallas_call(
        functools.partial(_ple_kernel, with_moe=moe is not None),
        name="ple_moe" if moe is not None else "ple",
        grid=(T // ROW_TILE,),
        in_specs=specs,
        out_specs=row(D),
        out_shape=jax.ShapeDtypeStruct((T, D), F32),
        compiler_params=_cparams("parallel"),
    )(*ins)


def _rope_cos_sin(pos, dim, theta):
    inv = theta ** (-jnp.arange(0, dim, 2, dtype=F32) / dim)
    ang = pos.astype(F32)[:, None] * inv[None, :]
    return jnp.cos(ang), jnp.sin(ang)


def _fold_tables(gain_lanes, cos_lanes, sin_lanes, partner):
    return gain_lanes[None, :] * cos_lanes, gain_lanes[partner][None, :] * sin_lanes


def _axial_tables(S, q_gain, k_gain):
    pos = jnp.arange(S)
    cr, sr = _rope_cos_sin(pos // GRID_W, AX_HEAD_DIM // 2, AX_THETA)
    cc, sc = _rope_cos_sin(pos % GRID_W, AX_HEAD_DIM // 2, AX_THETA)
    cos = jnp.concatenate([cr, cr, cc, cc], axis=1)
    sin = jnp.concatenate([-sr, sr, -sc, sc], axis=1)
    lane = np.arange(LANES)
    partner = np.where(lane % 64 < 32, lane + 32, lane - 32)
    return _fold_tables(q_gain, cos, sin, partner) + _fold_tables(k_gain, cos, sin, partner)


def _mla_tables(S, q_gain, k_gain):
    c, s = _rope_cos_sin(jnp.arange(S), MLA_ROPE, MLA_THETA)
    pad = LANES - MLA_QK
    cos = jnp.concatenate([jnp.ones((S, MLA_NOPE), F32), c, c, jnp.ones((S, pad), F32)], axis=1)
    sin = jnp.concatenate([jnp.zeros((S, MLA_NOPE), F32), -s, s, jnp.zeros((S, pad), F32)], axis=1)
    lane = np.arange(LANES)
    half = MLA_ROPE // 2
    partner = np.where((lane >= MLA_NOPE) & (lane < MLA_NOPE + half), lane + half,
                       np.where((lane >= MLA_NOPE + half) & (lane < MLA_QK), lane - half, lane))
    zpad = jnp.zeros((pad,), F32)
    gq = jnp.concatenate([q_gain, zpad])
    gk = jnp.concatenate([k_gain, zpad])
    return _fold_tables(gq, cos, sin, partner) + _fold_tables(gk, cos, sin, partner)


def _moe_dispatch(idx2, n_tiles):
    T = idx2.shape[0]
    tm = MOE_ROW_TILE
    flat = idx2.reshape(-1)
    onehot = (flat[:, None] == jnp.arange(N_EXPERTS, dtype=jnp.int32)[None, :]).astype(jnp.int32)
    csum = jnp.cumsum(onehot, axis=0)
    counts = csum[-1]
    rank = jnp.take_along_axis(csum, flat[:, None], axis=1)[:, 0] - 1
    padded = ((counts + tm - 1) // tm) * tm
    ends = jnp.cumsum(padded)
    starts = ends - padded
    dest = starts[flat] + rank
    src_tok = jnp.zeros((n_tiles * tm,), jnp.int32).at[dest].set(jnp.arange(TOP_K * T, dtype=jnp.int32) // TOP_K)
    tile_start = jnp.arange(n_tiles, dtype=jnp.int32) * tm
    tile_expert = jnp.minimum(jnp.searchsorted(ends, tile_start, side="right"), N_EXPERTS - 1).astype(jnp.int32)
    tile_valid = (tile_start < ends[-1]).astype(jnp.int32)
    return dest.reshape(T, TOP_K), src_tok, tile_expert, tile_valid


def kernel(x, p, attn_norm, ffn_norm, ple_norm, ple_w_in, ple_w_gate, mla_w_down, mla_q_norm, mla_w_uq, mla_kv_norm, mla_w_ukv, mla_q_gain, mla_k_gain, mla_w_o, swa_w_qkv, swa_q_gain, swa_k_gain, swa_sink, swa_w_o, ax_w_qkv, ax_q_gain, ax_k_gain, ax_w_o, ffn_w_gate, ffn_w_up, ffn_w_down, moe_w_router, moe_b_router, moe_w_gate, moe_w_up, moe_w_down):
    B, S, D = x.shape
    depth = p.shape[0]
    T = B * S
    xt = x.reshape(T, D)
    bf = lambda a: a.astype(BF16)
    n_moe_tiles = (TOP_K * T) // MOE_ROW_TILE + N_EXPERTS
    slopes = jnp.asarray(2.0 ** (-8.0 * np.arange(1, SWA_HEADS + 1) / SWA_HEADS), dtype=F32)

    for i in range(depth):
        kind = i % N_MIXERS
        j = i // N_MIXERS
        if kind == 0:
            wd = mla_w_down[j]
            zc = lambda n: jnp.zeros((D, n), F32)
            wd = jnp.concatenate([wd[:, :MLA_Q_RANK + MLA_KV_RANK], zc(MLA_NOPE), wd[:, MLA_Q_RANK + MLA_KV_RANK:],
                                  zc(LANES - MLA_QK)], axis=1)
            wuq = mla_w_uq[j].reshape(MLA_Q_RANK, MLA_HEADS, MLA_QK)
            wuq = jnp.pad(wuq, ((0, 0), (0, 0), (0, LANES - MLA_QK))).reshape(MLA_Q_RANK, MLA_HEADS * LANES)
            q, kv, kr = mla_proj(xt, attn_norm[i], bf(wd), mla_q_norm[j], mla_kv_norm[j], bf(wuq), bf(mla_w_ukv[j]))
            o = mla_attention(q, kv, kr, _mla_tables(S, mla_q_gain[j], mla_k_gain[j]), B, S)
            w_o = mla_w_o[j]
        elif kind == 1:
            qkv = norm_proj(xt, attn_norm[i], bf(swa_w_qkv[j]))
            gq = jnp.tile(swa_q_gain[j], 2).reshape(1, LANES)
            gk = jnp.tile(swa_k_gain[j], 2).reshape(1, LANES)
            o = swa_attention(qkv, slopes, swa_sink[j].astype(F32), gq, gk, B, S)
            w_o = swa_w_o[j]
        else:
            qkv = norm_proj(xt, attn_norm[i], bf(ax_w_qkv[j]))
            o = axial_attention(qkv, _axial_tables(S, ax_q_gain[j], ax_k_gain[j]), B, S)
            w_o = ax_w_o[j]
        xt = resid_proj(xt, o, bf(w_o))

        f = i // 2
        if i % 2 == 0:
            xt = dense_ffn(xt, ffn_norm[i], bf(ffn_w_gate[f]), bf(ffn_w_up[f]), bf(ffn_w_down[f]))
            moe = None
        else:
            wr = jnp.pad(moe_w_router[f], ((0, 0), (0, LANES - N_EXPERTS)))
            wr_hi = bf(wr)
            wr_lo = bf(wr - wr_hi.astype(F32))
            br = jnp.pad(moe_b_router[f].astype(F32), (0, LANES - N_EXPERTS)).reshape(1, LANES)
            h, idx, wt = moe_router(xt, ffn_norm[i], wr_hi, wr_lo, br)
            dest, src_tok, tile_expert, tile_valid = _moe_dispatch(idx[:, :TOP_K], n_moe_tiles)
            h_sorted = jnp.take(h, src_tok, axis=0)
            y = moe_ffn(h_sorted, tile_expert, tile_valid, bf(moe_w_gate[f]), bf(moe_w_up[f]), bf(moe_w_down[f]), 2)
            moe = (jnp.take(y, dest[:, 0], axis=0), jnp.take(y, dest[:, 1], axis=0), wt)
        xt = ple_update(xt, p[i].reshape(T, -1), ple_norm[i], bf(ple_w_in[i]), bf(ple_w_gate[i]), moe)
    return xt.reshape(B, S, D)
```

```python
import functools

import numpy as np
import jax
import jax.numpy as jnp
from jax import lax
from jax.experimental import pallas as pl
from jax.experimental.pallas import tpu as pltpu

F32 = jnp.float32
BF16 = jnp.bfloat16

EPS = 1e-6
GRID_W = 64
BLOCK_Q = 128

MLA_HEADS = 16
MLA_NOPE = 64
MLA_ROPE = 32
MLA_V = 64
MLA_Q_RANK = 256
MLA_KV_RANK = 128
MLA_THETA = 10000.0
MLA_QK = MLA_NOPE + MLA_ROPE

SWA_HEADS = 16
SWA_KV_HEADS = 4
SWA_HEAD_DIM = 64
SWA_WINDOW = 128

AX_HEADS = 8
AX_KV_HEADS = 4
AX_HEAD_DIM = 128
AX_THETA = 10000.0

N_EXPERTS = 8
TOP_K = 2
N_MIXERS = 3

LANES = 128
SUBLANES = 8
ROW_TILE = 512
MOE_ROW_TILE = 512
ATTN_Q_TILE = 256
FFN_SUB = 256
VMEM_LIMIT = 56 * 1024 * 1024
LOG2E = 1.4426950408889634


def _cparams(*sem, flags=None):
    return pltpu.CompilerParams(dimension_semantics=sem, vmem_limit_bytes=VMEM_LIMIT, flags=flags)


ATTN_FLAGS = None


def _rms(xf, gain):
    ms = jnp.mean(xf * xf, axis=-1, keepdims=True)
    return xf * lax.rsqrt(ms + EPS) * gain


def _dot(a, b):
    return jnp.dot(a, b, preferred_element_type=F32)


def _dot_nt(a, b):
    return lax.dot_general(a, b, (((1,), (1,)), ((), ())), preferred_element_type=F32)


def _resident(shape):
    return pl.BlockSpec(shape, lambda *_: (0,) * len(shape), pipeline_mode=pl.Buffered(1))


def _norm_proj_kernel(x_ref, g_ref, w_ref, o_ref):
    h = _rms(x_ref[...], g_ref[...]).astype(BF16)
    o_ref[...] = _dot(h, w_ref[...]).astype(o_ref.dtype)


def norm_proj(x, gain, w):
    T, D = x.shape
    N = w.shape[1]
    return pl.pallas_call(
        _norm_proj_kernel,
        name="norm_proj",
        grid=(T // ROW_TILE,),
        in_specs=[
            pl.BlockSpec((ROW_TILE, D), lambda i: (i, 0)),
            _resident((1, D)),
            _resident((D, N)),
        ],
        out_specs=pl.BlockSpec((ROW_TILE, N), lambda i: (i, 0)),
        out_shape=jax.ShapeDtypeStruct((T, N), BF16),
        compiler_params=_cparams("parallel"),
    )(x, gain.reshape(1, D), w)


def _mla_proj_kernel(x_ref, g_ref, wd_ref, qn_ref, kvn_ref, wuq_ref, wukv_ref, q_ref, kv_ref, kr_ref):
    h = _rms(x_ref[...], g_ref[...]).astype(BF16)
    down = _dot(h, wd_ref[...])
    cq = _rms(down[:, :MLA_Q_RANK], qn_ref[...]).astype(BF16)
    ckv = _rms(down[:, MLA_Q_RANK:MLA_Q_RANK + MLA_KV_RANK], kvn_ref[...]).astype(BF16)
    q_ref[...] = _dot(cq, wuq_ref[...]).astype(BF16)
    kv_ref[...] = _dot(ckv, wukv_ref[...]).astype(BF16)
    kr_ref[...] = down[:, MLA_Q_RANK + MLA_KV_RANK:]


def mla_proj(x, gain, wd, qn, kvn, wuq, wukv):
    T, D = x.shape
    nd = wd.shape[1]
    nq = wuq.shape[1]
    nkv = wukv.shape[1]
    row = lambda n: pl.BlockSpec((ROW_TILE, n), lambda i: (i, 0))
    return pl.pallas_call(
        _mla_proj_kernel,
        name="mla_proj",
        grid=(T // ROW_TILE,),
        in_specs=[
            row(D),
            _resident((1, D)),
            _resident((D, nd)),
            _resident((1, MLA_Q_RANK)),
            _resident((1, MLA_KV_RANK)),
            _resident((MLA_Q_RANK, nq)),
            _resident((MLA_KV_RANK, nkv)),
        ],
        out_specs=[row(nq), row(nkv), row(LANES)],
        out_shape=[
            jax.ShapeDtypeStruct((T, nq), BF16),
            jax.ShapeDtypeStruct((T, nkv), BF16),
            jax.ShapeDtypeStruct((T, LANES), F32),
        ],
        compiler_params=_cparams("parallel"),
    )(x, gain.reshape(1, D), wd, qn.reshape(1, -1), kvn.reshape(1, -1), wuq, wukv)


def _resid_proj_kernel(x_ref, a_ref, w_ref, o_ref):
    o_ref[...] = x_ref[...] + _dot(a_ref[...], w_ref[...])


def resid_proj(x, a, w):
    T, D = x.shape
    K = a.shape[1]
    return pl.pallas_call(
        _resid_proj_kernel,
        name="resid_proj",
        grid=(T // ROW_TILE,),
        in_specs=[
            pl.BlockSpec((ROW_TILE, D), lambda i: (i, 0)),
            pl.BlockSpec((ROW_TILE, K), lambda i: (i, 0)),
            _resident((K, D)),
        ],
        out_specs=pl.BlockSpec((ROW_TILE, D), lambda i: (i, 0)),
        out_shape=jax.ShapeDtypeStruct((T, D), F32),
        compiler_params=_cparams("parallel"),
    )(x, a, w)


def _two_unit_pipeline(n, scores, finish, emit):
    def step(i, prefetch):
        scores(i, 1)
        o0 = finish(i, 0)
        if prefetch:
            scores(i + 1, 0)
        o1 = finish(i, 1)
        emit(i, o0, o1)

    scores(0, 0)
    if n > 1:
        def body(i, carry):
            step(i, True)
            return carry

        lax.fori_loop(0, n - 1, body, 0)
    step(n - 1, False)


def _softmax_numerators(s, extra_logit=None):
    m = jnp.max(s, axis=-1, keepdims=True)
    if extra_logit is not None:
        m = jnp.maximum(m, extra_logit)
    return jnp.exp2(s - m).astype(BF16), m


def _axial_attn_kernel(q_ref, k_ref, v_ref, aq_ref, bq_ref, ak_ref, bk_ref, o_ref, k_scr, v_scr, s0_scr, s1_scr,
                       *, tq, scale):
    S = k_ref.shape[0]
    s_bufs = (s0_scr, s1_scr)
    lane = lax.broadcasted_iota(jnp.int32, (1, LANES), 1)
    first = (lane % 64) < 32

    def norm_rope(xf, a, b, extra):
        c = lax.rsqrt(jnp.mean(xf * xf, axis=-1, keepdims=True) + EPS) * extra
        partner = jnp.where(first, pltpu.roll(xf, LANES - 32, 1), pltpu.roll(xf, 32, 1))
        return (xf * a + partner * b) * c

    k_scr[...] = norm_rope(k_ref[...].astype(F32), ak_ref[...], bk_ref[...], 1.0).astype(BF16)
    v_scr[:, :LANES] = v_ref[...]
    v_scr[:, LANES:] = jnp.ones((S, LANES), BF16)

    def rows(i):
        return pl.ds(pl.multiple_of(i * tq, tq), tq)

    def scores(i, u):
        q = norm_rope(q_ref[rows(i), u * LANES:(u + 1) * LANES].astype(F32), aq_ref[rows(i), :], bq_ref[rows(i), :],
                      scale * LOG2E)
        s_bufs[u][...] = _dot_nt(q.astype(BF16), k_scr[...])

    def finish(i, u):
        p, _ = _softmax_numerators(s_bufs[u][...])
        o = _dot(p, v_scr[...])
        return o[:, :LANES] / o[:, LANES:]

    def emit(i, o0, o1):
        o_ref[rows(i), :LANES] = o0.astype(o_ref.dtype)
        o_ref[rows(i), LANES:] = o1.astype(o_ref.dtype)

    _two_unit_pipeline(S // tq, scores, finish, emit)


def axial_attention(qkv, tabs, B, S):
    R = AX_HEADS // AX_KV_HEADS
    assert R == 2
    tq = min(ATTN_Q_TILE, S)
    kern = functools.partial(_axial_attn_kernel, tq=tq, scale=AX_HEAD_DIM ** -0.5)
    tab = pl.BlockSpec((S, LANES), lambda b, g: (0, 0), pipeline_mode=pl.Buffered(1))
    return pl.pallas_call(
        kern,
        name="axial_attn",
        grid=(B, AX_KV_HEADS),
        in_specs=[
            pl.BlockSpec((S, R * LANES), lambda b, g: (b, g)),
            pl.BlockSpec((S, LANES), lambda b, g: (b, AX_HEADS + g)),
            pl.BlockSpec((S, LANES), lambda b, g: (b, AX_HEADS + AX_KV_HEADS + g)),
            tab, tab, tab, tab,
        ],
        out_specs=pl.BlockSpec((S, R * LANES), lambda b, g: (b, g)),
        out_shape=jax.ShapeDtypeStruct((B * S, AX_HEADS * AX_HEAD_DIM), BF16),
        scratch_shapes=[
            pltpu.VMEM((S, LANES), BF16),
            pltpu.VMEM((S, 2 * LANES), BF16),
            pltpu.VMEM((tq, S), F32),
            pltpu.VMEM((tq, S), F32),
        ],
        compiler_params=_cparams("parallel", "parallel", flags=ATTN_FLAGS),
    )(qkv, qkv, qkv, *tabs)


def _mla_attn_kernel(q_ref, kv_ref, kr_ref, aq_ref, bq_ref, ak_ref, bk_ref, o_ref, k_scr, v_scr, s0_scr, s1_scr,
                     *, tq, scale):
    S = kv_ref.shape[0]
    s_bufs = (s0_scr, s1_scr)
    lane = lax.broadcasted_iota(jnp.int32, (1, LANES), 1)
    lo = lane < MLA_NOPE
    first = lane < MLA_NOPE + MLA_ROPE // 2
    half = MLA_ROPE // 2

    def norm_rope(xf, a, b, extra):
        ms = jnp.sum(xf * xf, axis=-1, keepdims=True) * (1.0 / MLA_QK)
        c = lax.rsqrt(ms + EPS) * extra
        partner = jnp.where(first, pltpu.roll(xf, LANES - half, 1), pltpu.roll(xf, half, 1))
        return (xf * a + partner * b) * c

    kr = kr_ref[...]
    for hh in range(2):
        kvh = kv_ref[:, hh * LANES:(hh + 1) * LANES].astype(F32)
        k = jnp.where(lo, kvh, kr)
        k_scr[hh] = norm_rope(k, ak_ref[...], bk_ref[...], 1.0).astype(BF16)
        vh = jnp.where(lo, pltpu.roll(kvh, MLA_V, 1), 1.0) if hh == 0 else jnp.where(lo, 1.0, kvh)
        v_scr[hh] = vh.astype(BF16)

    def rows(i):
        return pl.ds(pl.multiple_of(i * tq, tq), tq)

    def scores(i, u):
        q = norm_rope(q_ref[rows(i), u * LANES:(u + 1) * LANES].astype(F32), aq_ref[rows(i), :], bq_ref[rows(i), :],
                      scale * LOG2E)
        s_bufs[u][...] = _dot_nt(q.astype(BF16), k_scr[u])

    def finish(i, u):
        p, _ = _softmax_numerators(s_bufs[u][...])
        o = _dot(p, v_scr[u])
        return o / pltpu.roll(o, MLA_V, 1)

    def emit(i, o0, o1):
        o_ref[rows(i), :] = jnp.where(lo, o0, o1).astype(o_ref.dtype)

    _two_unit_pipeline(S // tq, scores, finish, emit)


def mla_attention(q, kv, kr, tabs, B, S):
    tq = min(ATTN_Q_TILE, S)
    kern = functools.partial(_mla_attn_kernel, tq=tq, scale=MLA_QK ** -0.5)
    tab = pl.BlockSpec((S, LANES), lambda b, g: (0, 0), pipeline_mode=pl.Buffered(1))
    return pl.pallas_call(
        kern,
        name="mla_attn",
        grid=(B, MLA_HEADS // 2),
        in_specs=[
            pl.BlockSpec((S, 2 * LANES), lambda b, g: (b, g)),
            pl.BlockSpec((S, 2 * LANES), lambda b, g: (b, g)),
            pl.BlockSpec((S, LANES), lambda b, g: (b, 0)),
            tab, tab, tab, tab,
        ],
        out_specs=pl.BlockSpec((S, LANES), lambda b, g: (b, g)),
        out_shape=jax.ShapeDtypeStruct((B * S, MLA_HEADS * MLA_V), BF16),
        scratch_shapes=[
            pltpu.VMEM((2, S, LANES), BF16),
            pltpu.VMEM((2, S, LANES), BF16),
            pltpu.VMEM((tq, S), F32),
            pltpu.VMEM((tq, S), F32),
        ],
        compiler_params=_cparams("parallel", "parallel", flags=ATTN_FLAGS),
    )(q, kv, kr, *tabs)


def _swa_attn_kernel(slope_ref, sink_ref, q_ref, k_ref, v_ref, gq_ref, gk_ref, o_ref, k_scr, v_scr, s0_scr, s1_scr,
                     *, scale):
    S = k_ref.shape[0]
    span = BLOCK_Q + 2 * SWA_WINDOW
    R = SWA_HEADS // SWA_KV_HEADS
    rows_u = R * BLOCK_Q
    s_bufs = (s0_scr, s1_scr)
    pid = pl.program_id(1)
    lane = lax.broadcasted_iota(jnp.int32, (1, LANES), 1)
    lo = lane < SWA_HEAD_DIM
    hi = jnp.logical_not(lo)

    def seg_norm(xf, gain):
        sq = xf * xf
        s_lo = jnp.sum(jnp.where(lo, sq, 0.0), axis=-1, keepdims=True)
        s_hi = jnp.sum(jnp.where(lo, 0.0, sq), axis=-1, keepdims=True)
        inv = 1.0 / SWA_HEAD_DIM
        c = jnp.where(lo, lax.rsqrt(s_lo * inv + EPS), lax.rsqrt(s_hi * inv + EPS))
        return xf * c * gain

    k_scr[...] = seg_norm(k_ref[...].astype(F32), gk_ref[...]).astype(BF16)
    v = v_ref[...].astype(F32)
    v_scr[0] = jnp.where(lo, v, 1.0).astype(BF16)
    v_scr[1] = jnp.where(lo, 1.0, v).astype(BF16)

    head_of_row = lax.broadcasted_iota(jnp.int32, (rows_u, 1), 0) // BLOCK_Q

    def head_column(ref, e):
        col = jnp.zeros((rows_u, 1), F32)
        for r in range(R):
            col = jnp.where(head_of_row == r, ref[pid * 2 * R + e * R + r], col)
        return col

    slope_cols = [head_column(slope_ref, e) for e in range(2)]
    sink_cols = [head_column(sink_ref, e) for e in range(2)]
    rel = (lax.broadcasted_iota(jnp.int32, (rows_u, span), 0) % BLOCK_Q
           - lax.broadcasted_iota(jnp.int32, (rows_u, span), 1))

    def rows(j):
        return pl.ds(pl.multiple_of(j * BLOCK_Q, BLOCK_Q), BLOCK_Q)

    def window(j):
        start = jnp.clip(j * BLOCK_Q - SWA_WINDOW, 0, S - span)
        return pl.multiple_of(start, BLOCK_Q)

    def scores(j, e):
        keep = lo if e == 0 else hi
        parts = []
        for pb in range(e * (R // 2), (e + 1) * (R // 2)):
            qp = seg_norm(q_ref[rows(j), pb * LANES:(pb + 1) * LANES].astype(F32), gq_ref[...]) * (scale * LOG2E)
            qr = pltpu.roll(qp, SWA_HEAD_DIM, 1)
            for i in range(2):
                parts.append(jnp.where(keep, qp if i == e else qr, 0.0))
        q4 = jnp.concatenate(parts, axis=0).astype(BF16)
        start = window(j)
        s = _dot_nt(q4, k_scr[pl.ds(start, span), :])
        dist = jnp.abs(rel + (j * BLOCK_Q - start))
        s_bufs[e][...] = jnp.where(dist <= SWA_WINDOW, s - slope_cols[e] * dist.astype(F32), -jnp.inf)

    def finish(j, e):
        p, m = _softmax_numerators(s_bufs[e][...], extra_logit=sink_cols[e])
        o = _dot(p, v_scr[e, pl.ds(window(j), span), :])
        den = pltpu.roll(o, SWA_HEAD_DIM, 1) + jnp.exp2(sink_cols[e] - m)
        return o / den

    def emit(j, o0, o1):
        for e, o in ((0, o0), (1, o1)):
            orot = pltpu.roll(o, SWA_HEAD_DIM, 1)
            for k in range(R // 2):
                pb = e * (R // 2) + k
                even = (o if e == 0 else orot)[2 * k * BLOCK_Q:(2 * k + 1) * BLOCK_Q]
                odd = (o if e == 1 else orot)[(2 * k + 1) * BLOCK_Q:(2 * k + 2) * BLOCK_Q]
                o_ref[rows(j), pb * LANES:(pb + 1) * LANES] = jnp.where(lo, even, odd).astype(o_ref.dtype)

    _two_unit_pipeline(S // BLOCK_Q, scores, finish, emit)


def swa_attention(qkv, slopes, sink, gq, gk, B, S):
    n_steps = SWA_KV_HEADS // 2
    R = SWA_HEADS // SWA_KV_HEADS
    span = BLOCK_Q + 2 * SWA_WINDOW
    qw = SWA_HEADS * SWA_HEAD_DIM // n_steps
    kbase = SWA_HEADS * SWA_HEAD_DIM // LANES
    smem = pl.BlockSpec(memory_space=pltpu.SMEM)
    gain = pl.BlockSpec((1, LANES), lambda b, g: (0, 0))
    kern = functools.partial(_swa_attn_kernel, scale=SWA_HEAD_DIM ** -0.5)
    return pl.pallas_call(
        kern,
        name="swa_attn",
        grid=(B, n_steps),
        in_specs=[
            smem, smem,
            pl.BlockSpec((S, qw), lambda b, g: (b, g)),
            pl.BlockSpec((S, LANES), lambda b, g: (b, kbase + g)),
            pl.BlockSpec((S, LANES), lambda b, g: (b, kbase + n_steps + g)),
            gain, gain,
        ],
        out_specs=pl.BlockSpec((S, qw), lambda b, g: (b, g)),
        out_shape=jax.ShapeDtypeStruct((B * S, SWA_HEADS * SWA_HEAD_DIM), BF16),
        scratch_shapes=[
            pltpu.VMEM((S, LANES), BF16),
            pltpu.VMEM((2, S, LANES), BF16),
            pltpu.VMEM((R * BLOCK_Q, span), F32),
            pltpu.VMEM((R * BLOCK_Q, span), F32),
        ],
        compiler_params=_cparams("parallel", "parallel", flags=ATTN_FLAGS),
    )(slopes, sink, qkv, qkv, qkv, gq, gk)


def _swiglu_accumulate(h, wg_ref, wu_ref, wd_ref, acc_ref):
    for c in range(wg_ref.shape[1] // FFN_SUB):
        sl = slice(c * FFN_SUB, (c + 1) * FFN_SUB)
        g = _dot(h, wg_ref[:, sl])
        u = _dot(h, wu_ref[:, sl])
        a = (g * jax.nn.sigmoid(g) * u).astype(BF16)
        acc_ref[...] += _dot(a, wd_ref[sl, :])


def _ffn_kernel(x_ref, g_ref, wg_ref, wu_ref, wd_ref, o_ref, acc_scr):
    x = x_ref[...]
    acc_scr[...] = x
    _swiglu_accumulate(_rms(x, g_ref[...]).astype(BF16), wg_ref, wu_ref, wd_ref, acc_scr)
    o_ref[...] = acc_scr[...]


def dense_ffn(x, gain, wg, wu, wd):
    T, D = x.shape
    Fd = wg.shape[1]
    return pl.pallas_call(
        _ffn_kernel,
        name="dense_ffn",
        grid=(T // ROW_TILE,),
        in_specs=[
            pl.BlockSpec((ROW_TILE, D), lambda i: (i, 0)),
            _resident((1, D)),
            _resident((D, Fd)),
            _resident((D, Fd)),
            _resident((Fd, D)),
        ],
        out_specs=pl.BlockSpec((ROW_TILE, D), lambda i: (i, 0)),
        out_shape=jax.ShapeDtypeStruct((T, D), F32),
        scratch_shapes=[pltpu.VMEM((ROW_TILE, D), F32)],
        compiler_params=_cparams("parallel"),
    )(x, gain.reshape(1, D), wg, wu, wd)


def _router_kernel(x_ref, g_ref, whi_ref, wlo_ref, b_ref, h_ref, idx_ref, wt_ref, cnt_ref):
    hf = _rms(x_ref[...], g_ref[...])
    h_hi = hf.astype(BF16)
    h_lo = (hf - h_hi.astype(F32)).astype(BF16)
    h_ref[...] = h_hi
    logits = _dot(h_hi, whi_ref[...]) + _dot(h_hi, wlo_ref[...]) + _dot(h_lo, whi_ref[...]) + b_ref[...]
    lane = lax.broadcasted_iota(jnp.int32, logits.shape, 1)
    logits = jnp.where(lane < N_EXPERTS, logits, -jnp.inf)
    m1 = jnp.max(logits, axis=-1, keepdims=True)
    i1 = jnp.min(jnp.where(logits == m1, lane, LANES), axis=-1, keepdims=True)
    rest = jnp.where(lane == i1, -jnp.inf, logits)
    m2 = jnp.max(rest, axis=-1, keepdims=True)
    i2 = jnp.min(jnp.where(rest == m2, lane, LANES), axis=-1, keepdims=True)
    e2 = jnp.exp(m2 - m1)
    w1 = 1.0 / (1.0 + e2)
    w2 = e2 / (1.0 + e2)
    wt_ref[...] = jnp.where(lane == 0, w1, jnp.where(lane == 1, w2, 0.0))
    onehot = jnp.where(jnp.logical_or(lane == i1, lane == i2), 1.0, 0.0)
    tm = onehot.shape[0]
    earlier = (lax.broadcasted_iota(jnp.int32, (tm, tm), 0) > lax.broadcasted_iota(jnp.int32, (tm, tm), 1))
    prefix = _dot(jnp.where(earlier, 1.0, 0.0).astype(BF16), onehot.astype(BF16))
    r1 = jnp.sum(jnp.where(lane == i1, prefix, 0.0), axis=-1, keepdims=True).astype(jnp.int32)
    r2 = jnp.sum(jnp.where(lane == i2, prefix, 0.0), axis=-1, keepdims=True).astype(jnp.int32)
    idx_ref[...] = jnp.where(lane == 0, i1, jnp.where(lane == 1, i2, jnp.where(lane == 2, r1, jnp.where(lane == 3, r2, 0))))
    cnt_ref[...] = jnp.broadcast_to(jnp.sum(onehot, axis=0, keepdims=True), cnt_ref.shape)


def moe_router(x, gain, w_hi, w_lo, bias):
    T, D = x.shape
    row = lambda n: pl.BlockSpec((ROW_TILE, n), lambda i: (i, 0))
    return pl.pallas_call(
        _router_kernel,
        name="moe_router",
        grid=(T // ROW_TILE,),
        in_specs=[row(D), _resident((1, D)), _resident((D, LANES)), _resident((D, LANES)), _resident((1, LANES))],
        out_specs=[row(D), row(LANES), row(LANES), pl.BlockSpec((SUBLANES, LANES), lambda i: (i, 0))],
        out_shape=[
            jax.ShapeDtypeStruct((T, D), BF16),
            jax.ShapeDtypeStruct((T, LANES), jnp.int32),
            jax.ShapeDtypeStruct((T, LANES), F32),
            jax.ShapeDtypeStruct((T // ROW_TILE * SUBLANES, LANES), F32),
        ],
        compiler_params=_cparams("parallel"),
    )(x, gain.reshape(1, D), w_hi, w_lo, bias)


def _moe_ffn_kernel(te_ref, tv_ref, h_ref, wg_ref, wu_ref, wd_ref, o_ref, acc_scr):
    i = pl.program_id(0)
    f = pl.program_id(1)
    last = pl.num_programs(1) - 1

    @pl.when(f == 0)
    def _():
        acc_scr[...] = jnp.zeros_like(acc_scr)

    @pl.when(tv_ref[i] > 0)
    def _():
        _swiglu_accumulate(h_ref[...], wg_ref, wu_ref, wd_ref, acc_scr)

    @pl.when(f == last)
    def _():
        o_ref[...] = acc_scr[...].astype(o_ref.dtype)


def moe_ffn(h_sorted, tile_expert, tile_valid, wg, wu, wd, n_chunks):
    R, D = h_sorted.shape
    E, _, Fe = wg.shape
    tf = Fe // n_chunks
    grid_spec = pltpu.PrefetchScalarGridSpec(
        num_scalar_prefetch=2,
        grid=(R // MOE_ROW_TILE, n_chunks),
        in_specs=[
            pl.BlockSpec((MOE_ROW_TILE, D), lambda i, f, te, tv: (i, 0)),
            pl.BlockSpec((None, D, tf), lambda i, f, te, tv: (te[i], 0, f)),
            pl.BlockSpec((None, D, tf), lambda i, f, te, tv: (te[i], 0, f)),
            pl.BlockSpec((None, tf, D), lambda i, f, te, tv: (te[i], f, 0)),
        ],
        out_specs=pl.BlockSpec((MOE_ROW_TILE, D), lambda i, f, te, tv: (i, 0)),
        scratch_shapes=[pltpu.VMEM((MOE_ROW_TILE, D), F32)],
    )
    return pl.pallas_call(
        _moe_ffn_kernel,
        name="moe_ffn",
        grid_spec=grid_spec,
        out_shape=jax.ShapeDtypeStruct((R, D), BF16),
        compiler_params=_cparams("parallel", "arbitrary"),
    )(tile_expert, tile_valid, h_sorted, wg, wu, wd)


def _ple_kernel(*refs, with_moe):
    if with_moe:
        x_ref, y0_ref, y1_ref, wt_ref, p_ref, g_ref, win_ref, wgate_ref, o_ref = refs
        wt = wt_ref[...]
        x = x_ref[...] + wt[:, 0:1] * y0_ref[...].astype(F32) + wt[:, 1:2] * y1_ref[...].astype(F32)
    else:
        x_ref, p_ref, g_ref, win_ref, wgate_ref, o_ref = refs
        x = x_ref[...]
    gate = jax.nn.sigmoid(_dot(_rms(x, g_ref[...]).astype(BF16), wgate_ref[...]))
    o_ref[...] = x + _dot(p_ref[...].astype(BF16), win_ref[...]) * gate


def ple_update(x, p, layer, gain, w_in, w_gate, moe=None):
    T, D = x.shape
    P = p.shape[2]
    row = lambda n: pl.BlockSpec((ROW_TILE, n), lambda i: (i, 0))
    ins = [x]
    specs = [row(D)]
    if moe is not None:
        y0, y1, wt = moe
        ins += [y0, y1, wt]
        specs += [row(D), row(D), row(LANES)]
    ins += [p, gain.reshape(1, D), w_in, w_gate]
    specs += [pl.BlockSpec((None, ROW_TILE, P), lambda i: (layer, i, 0)), _resident((1, D)), _resident((P, D)),
              _resident((D, D))]
    return pl.pallas_call(
        functools.partial(_ple_kernel, with_moe=moe is not None),
        name="ple_moe" if moe is not None else "ple",
        grid=(T // ROW_TILE,),
        in_specs=specs,
        out_specs=row(D),
        out_shape=jax.ShapeDtypeStruct((T, D), F32),
        compiler_params=_cparams("parallel"),
    )(*ins)


def _rope_cos_sin(pos, dim, theta):
    inv = theta ** (-jnp.arange(0, dim, 2, dtype=F32) / dim)
    ang = pos.astype(F32)[:, None] * inv[None, :]
    return jnp.cos(ang), jnp.sin(ang)


def _fold_tables(gain_lanes, cos_lanes, sin_lanes, partner):
    return gain_lanes[None, :] * cos_lanes, gain_lanes[partner][None, :] * sin_lanes


def _axial_tables(S, q_gain, k_gain):
    pos = jnp.arange(S)
    cr, sr = _rope_cos_sin(pos // GRID_W, AX_HEAD_DIM // 2, AX_THETA)
    cc, sc = _rope_cos_sin(pos % GRID_W, AX_HEAD_DIM // 2, AX_THETA)
    cos = jnp.concatenate([cr, cr, cc, cc], axis=1)
    sin = jnp.concatenate([-sr, sr, -sc, sc], axis=1)
    lane = np.arange(LANES)
    partner = np.where(lane % 64 < 32, lane + 32, lane - 32)
    return _fold_tables(q_gain, cos, sin, partner) + _fold_tables(k_gain, cos, sin, partner)


def _mla_tables(S, q_gain, k_gain):
    c, s = _rope_cos_sin(jnp.arange(S), MLA_ROPE, MLA_THETA)
    pad = LANES - MLA_QK
    cos = jnp.concatenate([jnp.ones((S, MLA_NOPE), F32), c, c, jnp.ones((S, pad), F32)], axis=1)
    sin = jnp.concatenate([jnp.zeros((S, MLA_NOPE), F32), -s, s, jnp.zeros((S, pad), F32)], axis=1)
    lane = np.arange(LANES)
    half = MLA_ROPE // 2
    partner = np.where((lane >= MLA_NOPE) & (lane < MLA_NOPE + half), lane + half,
                       np.where((lane >= MLA_NOPE + half) & (lane < MLA_QK), lane - half, lane))
    zpad = jnp.zeros((pad,), F32)
    gq = jnp.concatenate([q_gain, zpad])
    gk = jnp.concatenate([k_gain, zpad])
    return _fold_tables(gq, cos, sin, partner) + _fold_tables(gk, cos, sin, partner)


def _moe_dispatch(idx, cnt, n_tiles):
    T = idx.shape[0]
    tm = MOE_ROW_TILE
    n_rt = T // ROW_TILE
    e_pair = idx[:, :TOP_K]
    cnt = cnt.reshape(n_rt, SUBLANES, LANES)[:, 0, :N_EXPERTS].astype(jnp.int32)
    rt = jnp.arange(n_rt)
    before = jnp.sum(jnp.where((rt[None, :] < rt[:, None])[:, :, None], cnt[None, :, :], 0), axis=1)
    counts = jnp.sum(cnt, axis=0)
    padded = ((counts + tm - 1) // tm) * tm
    ex = jnp.arange(N_EXPERTS)
    ends = jnp.sum(jnp.where(ex[None, :] <= ex[:, None], padded[None, :], 0), axis=1)
    base = jnp.repeat((ends - padded)[None, :] + before, ROW_TILE, axis=0)
    dest = jnp.sum(jnp.where(e_pair[:, :, None] == ex[None, None, :], base[:, None, :], 0), axis=-1) + idx[:, TOP_K:2 * TOP_K]
    src_tok = jnp.zeros((n_tiles * tm,), jnp.int32).at[dest.reshape(-1)].set(jnp.arange(TOP_K * T, dtype=jnp.int32) // TOP_K)
    tile_start = jnp.arange(n_tiles, dtype=jnp.int32) * tm
    tile_expert = jnp.minimum(jnp.sum((tile_start[:, None] >= ends[None, :]).astype(jnp.int32), axis=1), N_EXPERTS - 1)
    tile_valid = (tile_start < ends[-1]).astype(jnp.int32)
    return dest, src_tok, tile_expert, tile_valid


def kernel(x, p, attn_norm, ffn_norm, ple_norm, ple_w_in, ple_w_gate, mla_w_down, mla_q_norm, mla_w_uq, mla_kv_norm, mla_w_ukv, mla_q_gain, mla_k_gain, mla_w_o, swa_w_qkv, swa_q_gain, swa_k_gain, swa_sink, swa_w_o, ax_w_qkv, ax_q_gain, ax_k_gain, ax_w_o, ffn_w_gate, ffn_w_up, ffn_w_down, moe_w_router, moe_b_router, moe_w_gate, moe_w_up, moe_w_down):
    B, S, D = x.shape
    depth = p.shape[0]
    T = B * S
    xt = x.reshape(T, D)
    bf = lambda a: a.astype(BF16)
    n_moe_tiles = (TOP_K * T) // MOE_ROW_TILE + N_EXPERTS
    slopes = jnp.asarray(2.0 ** (-8.0 * np.arange(1, SWA_HEADS + 1) / SWA_HEADS) * LOG2E, dtype=F32)
    p3 = p.reshape(depth, T, -1)

    for i in range(depth):
        kind = i % N_MIXERS
        j = i // N_MIXERS
        if kind == 0:
            wd = mla_w_down[j]
            zc = lambda n: jnp.zeros((D, n), F32)
            wd = jnp.concatenate([wd[:, :MLA_Q_RANK + MLA_KV_RANK], zc(MLA_NOPE), wd[:, MLA_Q_RANK + MLA_KV_RANK:],
                                  zc(LANES - MLA_QK)], axis=1)
            wuq = mla_w_uq[j].reshape(MLA_Q_RANK, MLA_HEADS, MLA_QK)
            wuq = jnp.pad(wuq, ((0, 0), (0, 0), (0, LANES - MLA_QK))).reshape(MLA_Q_RANK, MLA_HEADS * LANES)
            q, kv, kr = mla_proj(xt, attn_norm[i], bf(wd), mla_q_norm[j], mla_kv_norm[j], bf(wuq), bf(mla_w_ukv[j]))
            o = mla_attention(q, kv, kr, _mla_tables(S, mla_q_gain[j], mla_k_gain[j]), B, S)
            w_o = mla_w_o[j]
        elif kind == 1:
            qkv = norm_proj(xt, attn_norm[i], bf(swa_w_qkv[j]))
            gq = jnp.tile(swa_q_gain[j], 2).reshape(1, LANES)
            gk = jnp.tile(swa_k_gain[j], 2).reshape(1, LANES)
            o = swa_attention(qkv, slopes, swa_sink[j].astype(F32) * LOG2E, gq, gk, B, S)
            w_o = swa_w_o[j]
        else:
            qkv = norm_proj(xt, attn_norm[i], bf(ax_w_qkv[j]))
            o = axial_attention(qkv, _axial_tables(S, ax_q_gain[j], ax_k_gain[j]), B, S)
            w_o = ax_w_o[j]
        xt = resid_proj(xt, o, bf(w_o))

        f = i // 2
        if i % 2 == 0:
            xt = dense_ffn(xt, ffn_norm[i], bf(ffn_w_gate[f]), bf(ffn_w_up[f]), bf(ffn_w_down[f]))
            moe = None
        else:
            wr = jnp.pad(moe_w_router[f], ((0, 0), (0, LANES - N_EXPERTS)))
            wr_hi = bf(wr)
            wr_lo = bf(wr - wr_hi.astype(F32))
            br = jnp.pad(moe_b_router[f].astype(F32), (0, LANES - N_EXPERTS)).reshape(1, LANES)
            h, idx, wt, cnt = moe_router(xt, ffn_norm[i], wr_hi, wr_lo, br)
            dest, src_tok, tile_expert, tile_valid = _moe_dispatch(idx, cnt, n_moe_tiles)
            h_sorted = jnp.take(h, src_tok, axis=0)
            y = moe_ffn(h_sorted, tile_expert, tile_valid, bf(moe_w_gate[f]), bf(moe_w_up[f]), bf(moe_w_down[f]), 2)
            moe = (jnp.take(y, dest[:, 0], axis=0), jnp.take(y, dest[:, 1], axis=0), wt)
        xt = ple_update(xt, p3, i, ple_norm[i], bf(ple_w_in[i]), bf(ple_w_gate[i]), moe)
    return xt.reshape(B, S, D)
```

```python
import functools

import numpy as np
import jax
import jax.numpy as jnp
from jax import lax
from jax.experimental import pallas as pl
from jax.experimental.pallas import tpu as pltpu

F32 = jnp.float32
BF16 = jnp.bfloat16

EPS = 1e-6
GRID_W = 64
BLOCK_Q = 128

MLA_HEADS = 16
MLA_NOPE = 64
MLA_ROPE = 32
MLA_V = 64
MLA_Q_RANK = 256
MLA_KV_RANK = 128
MLA_THETA = 10000.0
MLA_QK = MLA_NOPE + MLA_ROPE

SWA_HEADS = 16
SWA_KV_HEADS = 4
SWA_HEAD_DIM = 64
SWA_WINDOW = 128

AX_HEADS = 8
AX_KV_HEADS = 4
AX_HEAD_DIM = 128
AX_THETA = 10000.0

N_EXPERTS = 8
TOP_K = 2
N_MIXERS = 3

LANES = 128
SUBLANES = 8
ROW_TILE = 512
MOE_ROW_TILE = 512
ATTN_Q_TILE = 256
FFN_SUB = 256
VMEM_LIMIT = 56 * 1024 * 1024
LOG2E = 1.4426950408889634


def _cparams(*sem, flags=None):
    return pltpu.CompilerParams(dimension_semantics=sem, vmem_limit_bytes=VMEM_LIMIT, flags=flags)


ATTN_FLAGS = None


def _rms(xf, gain):
    ms = jnp.mean(xf * xf, axis=-1, keepdims=True)
    return xf * lax.rsqrt(ms + EPS) * gain


def _dot(a, b):
    return jnp.dot(a, b, preferred_element_type=F32)


def _dot_nt(a, b):
    return lax.dot_general(a, b, (((1,), (1,)), ((), ())), preferred_element_type=F32)


def _resident(shape):
    return pl.BlockSpec(shape, lambda *_: (0,) * len(shape), pipeline_mode=pl.Buffered(1))


def _norm_proj_kernel(x_ref, g_ref, w_ref, o_ref):
    h = _rms(x_ref[...], g_ref[...]).astype(BF16)
    o_ref[...] = _dot(h, w_ref[...]).astype(o_ref.dtype)


def norm_proj(x, gain, w):
    T, D = x.shape
    N = w.shape[1]
    return pl.pallas_call(
        _norm_proj_kernel,
        name="norm_proj",
        grid=(T // ROW_TILE,),
        in_specs=[
            pl.BlockSpec((ROW_TILE, D), lambda i: (i, 0)),
            _resident((1, D)),
            _resident((D, N)),
        ],
        out_specs=pl.BlockSpec((ROW_TILE, N), lambda i: (i, 0)),
        out_shape=jax.ShapeDtypeStruct((T, N), BF16),
        compiler_params=_cparams("parallel"),
    )(x, gain.reshape(1, D), w)


def _mla_proj_kernel(x_ref, g_ref, wd_ref, qn_ref, kvn_ref, wuq_ref, wukv_ref, q_ref, kv_ref, kr_ref):
    h = _rms(x_ref[...], g_ref[...]).astype(BF16)
    down = _dot(h, wd_ref[...])
    cq = _rms(down[:, :MLA_Q_RANK], qn_ref[...]).astype(BF16)
    ckv = _rms(down[:, MLA_Q_RANK:MLA_Q_RANK + MLA_KV_RANK], kvn_ref[...]).astype(BF16)
    q_ref[...] = _dot(cq, wuq_ref[...]).astype(BF16)
    kv_ref[...] = _dot(ckv, wukv_ref[...]).astype(BF16)
    kr_ref[...] = down[:, MLA_Q_RANK + MLA_KV_RANK:]


def mla_proj(x, gain, wd, qn, kvn, wuq, wukv):
    T, D = x.shape
    nd = wd.shape[1]
    nq = wuq.shape[1]
    nkv = wukv.shape[1]
    row = lambda n: pl.BlockSpec((ROW_TILE, n), lambda i: (i, 0))
    return pl.pallas_call(
        _mla_proj_kernel,
        name="mla_proj",
        grid=(T // ROW_TILE,),
        in_specs=[
            row(D),
            _resident((1, D)),
            _resident((D, nd)),
            _resident((1, MLA_Q_RANK)),
            _resident((1, MLA_KV_RANK)),
            _resident((MLA_Q_RANK, nq)),
            _resident((MLA_KV_RANK, nkv)),
        ],
        out_specs=[row(nq), row(nkv), row(LANES)],
        out_shape=[
            jax.ShapeDtypeStruct((T, nq), BF16),
            jax.ShapeDtypeStruct((T, nkv), BF16),
            jax.ShapeDtypeStruct((T, LANES), F32),
        ],
        compiler_params=_cparams("parallel"),
    )(x, gain.reshape(1, D), wd, qn.reshape(1, -1), kvn.reshape(1, -1), wuq, wukv)


def _two_unit_pipeline(n, scores, finish, emit):
    def step(i, prefetch):
        scores(i, 1)
        o0 = finish(i, 0)
        if prefetch:
            scores(i + 1, 0)
        o1 = finish(i, 1)
        emit(i, o0, o1)

    scores(0, 0)
    if n > 1:
        def body(i, carry):
            step(i, True)
            return carry

        lax.fori_loop(0, n - 1, body, 0)
    step(n - 1, False)


def _store_scores(s, s_buf, m_buf=None):
    s_buf[...] = s
    if m_buf is not None:
        m_buf[...] = jnp.max(s, axis=-1, keepdims=True)


def _softmax_numerators(s_buf, m_buf=None, extra_logit=None):
    m = m_buf[...] if m_buf is not None else jnp.max(s_buf[...], axis=-1, keepdims=True)
    if extra_logit is not None:
        m = jnp.maximum(m, extra_logit)
    return jnp.exp2(s_buf[...] - m).astype(BF16), m


def _axial_attn_kernel(q_ref, k_ref, v_ref, aq_ref, bq_ref, ak_ref, bk_ref, o_ref, k_scr, v_scr, s0_scr, s1_scr,
                       *, tq, scale):
    S = k_ref.shape[0]
    s_bufs = (s0_scr, s1_scr)
    lane = lax.broadcasted_iota(jnp.int32, (1, LANES), 1)
    first = (lane % 64) < 32

    def norm_rope(xf, a, b, extra):
        c = lax.rsqrt(jnp.mean(xf * xf, axis=-1, keepdims=True) + EPS) * extra
        partner = jnp.where(first, pltpu.roll(xf, LANES - 32, 1), pltpu.roll(xf, 32, 1))
        return (xf * a + partner * b) * c

    k_scr[...] = norm_rope(k_ref[...].astype(F32), ak_ref[...], bk_ref[...], 1.0).astype(BF16)
    v_scr[:, :LANES] = v_ref[...]
    v_scr[:, LANES:] = jnp.ones((S, LANES), BF16)

    def rows(i):
        return pl.ds(pl.multiple_of(i * tq, tq), tq)

    def scores(i, u):
        q = norm_rope(q_ref[rows(i), u * LANES:(u + 1) * LANES].astype(F32), aq_ref[rows(i), :], bq_ref[rows(i), :],
                      scale * LOG2E)
        _store_scores(_dot_nt(q.astype(BF16), k_scr[...]), s_bufs[u])

    def finish(i, u):
        p, _ = _softmax_numerators(s_bufs[u])
        o = _dot(p, v_scr[...])
        return o[:, :LANES] / o[:, LANES:]

    def emit(i, o0, o1):
        o_ref[rows(i), :LANES] = o0.astype(o_ref.dtype)
        o_ref[rows(i), LANES:] = o1.astype(o_ref.dtype)

    _two_unit_pipeline(S // tq, scores, finish, emit)


def axial_attention(qkv, tabs, B, S):
    R = AX_HEADS // AX_KV_HEADS
    assert R == 2
    tq = min(ATTN_Q_TILE, S)
    kern = functools.partial(_axial_attn_kernel, tq=tq, scale=AX_HEAD_DIM ** -0.5)
    tab = pl.BlockSpec((S, LANES), lambda b, g: (0, 0), pipeline_mode=pl.Buffered(1))
    return pl.pallas_call(
        kern,
        name="axial_attn",
        grid=(B, AX_KV_HEADS),
        in_specs=[
            pl.BlockSpec((S, R * LANES), lambda b, g: (b, g)),
            pl.BlockSpec((S, LANES), lambda b, g: (b, AX_HEADS + g)),
            pl.BlockSpec((S, LANES), lambda b, g: (b, AX_HEADS + AX_KV_HEADS + g)),
            tab, tab, tab, tab,
        ],
        out_specs=pl.BlockSpec((S, R * LANES), lambda b, g: (b, g)),
        out_shape=jax.ShapeDtypeStruct((B * S, AX_HEADS * AX_HEAD_DIM), BF16),
        scratch_shapes=[
            pltpu.VMEM((S, LANES), BF16),
            pltpu.VMEM((S, 2 * LANES), BF16),
            pltpu.VMEM((tq, S), F32),
            pltpu.VMEM((tq, S), F32),
        ],
        compiler_params=_cparams("parallel", "parallel", flags=ATTN_FLAGS),
    )(qkv, qkv, qkv, *tabs)


def _mla_attn_kernel(q_ref, kv_ref, kr_ref, aq_ref, bq_ref, ak_ref, bk_ref, o_ref, k_scr, v_scr, s0_scr, s1_scr,
                     m0_scr, m1_scr, *, tq, scale):
    S = kv_ref.shape[0]
    s_bufs = (s0_scr, s1_scr)
    m_bufs = (m0_scr, m1_scr)
    lane = lax.broadcasted_iota(jnp.int32, (1, LANES), 1)
    lo = lane < MLA_NOPE
    first = lane < MLA_NOPE + MLA_ROPE // 2
    half = MLA_ROPE // 2

    def norm_rope(xf, a, b, extra):
        ms = jnp.sum(xf * xf, axis=-1, keepdims=True) * (1.0 / MLA_QK)
        c = lax.rsqrt(ms + EPS) * extra
        partner = jnp.where(first, pltpu.roll(xf, LANES - half, 1), pltpu.roll(xf, half, 1))
        return (xf * a + partner * b) * c

    kr = kr_ref[...]
    for hh in range(2):
        kvh = kv_ref[:, hh * LANES:(hh + 1) * LANES].astype(F32)
        k = jnp.where(lo, kvh, kr)
        k_scr[hh] = norm_rope(k, ak_ref[...], bk_ref[...], 1.0).astype(BF16)
        vh = jnp.where(lo, pltpu.roll(kvh, MLA_V, 1), 1.0) if hh == 0 else jnp.where(lo, 1.0, kvh)
        v_scr[hh] = vh.astype(BF16)

    def rows(i):
        return pl.ds(pl.multiple_of(i * tq, tq), tq)

    def scores(i, u):
        q = norm_rope(q_ref[rows(i), u * LANES:(u + 1) * LANES].astype(F32), aq_ref[rows(i), :], bq_ref[rows(i), :],
                      scale * LOG2E)
        _store_scores(_dot_nt(q.astype(BF16), k_scr[u]), s_bufs[u], m_bufs[u])

    def finish(i, u):
        p, _ = _softmax_numerators(s_bufs[u], m_bufs[u])
        o = _dot(p, v_scr[u])
        return o / pltpu.roll(o, MLA_V, 1)

    def emit(i, o0, o1):
        o_ref[rows(i), :] = jnp.where(lo, o0, o1).astype(o_ref.dtype)

    _two_unit_pipeline(S // tq, scores, finish, emit)


def mla_attention(q, kv, kr, tabs, B, S):
    tq = min(ATTN_Q_TILE, S)
    kern = functools.partial(_mla_attn_kernel, tq=tq, scale=MLA_QK ** -0.5)
    tab = pl.BlockSpec((S, LANES), lambda b, g: (0, 0), pipeline_mode=pl.Buffered(1))
    return pl.pallas_call(
        kern,
        name="mla_attn",
        grid=(B, MLA_HEADS // 2),
        in_specs=[
            pl.BlockSpec((S, 2 * LANES), lambda b, g: (b, g)),
            pl.BlockSpec((S, 2 * LANES), lambda b, g: (b, g)),
            pl.BlockSpec((S, LANES), lambda b, g: (b, 0)),
            tab, tab, tab, tab,
        ],
        out_specs=pl.BlockSpec((S, LANES), lambda b, g: (b, g)),
        out_shape=jax.ShapeDtypeStruct((B * S, MLA_HEADS * MLA_V), BF16),
        scratch_shapes=[
            pltpu.VMEM((2, S, LANES), BF16),
            pltpu.VMEM((2, S, LANES), BF16),
            pltpu.VMEM((tq, S), F32),
            pltpu.VMEM((tq, S), F32),
            pltpu.VMEM((tq, 1), F32),
            pltpu.VMEM((tq, 1), F32),
        ],
        compiler_params=_cparams("parallel", "parallel", flags=ATTN_FLAGS),
    )(q, kv, kr, *tabs)


def _swa_attn_kernel(slope_ref, sink_ref, q_ref, k_ref, v_ref, gq_ref, gk_ref, o_ref, k_scr, v_scr, s0_scr, s1_scr,
                     *, scale):
    S = k_ref.shape[0]
    span = BLOCK_Q + 2 * SWA_WINDOW
    R = SWA_HEADS // SWA_KV_HEADS
    rows_u = R * BLOCK_Q
    s_bufs = (s0_scr, s1_scr)
    pid = pl.program_id(1)
    lane = lax.broadcasted_iota(jnp.int32, (1, LANES), 1)
    lo = lane < SWA_HEAD_DIM
    hi = jnp.logical_not(lo)

    def seg_norm(xf, gain):
        sq = xf * xf
        s_lo = jnp.sum(jnp.where(lo, sq, 0.0), axis=-1, keepdims=True)
        s_hi = jnp.sum(jnp.where(lo, 0.0, sq), axis=-1, keepdims=True)
        inv = 1.0 / SWA_HEAD_DIM
        c = jnp.where(lo, lax.rsqrt(s_lo * inv + EPS), lax.rsqrt(s_hi * inv + EPS))
        return xf * c * gain

    k_scr[...] = seg_norm(k_ref[...].astype(F32), gk_ref[...]).astype(BF16)
    v = v_ref[...].astype(F32)
    v_scr[0] = jnp.where(lo, v, 1.0).astype(BF16)
    v_scr[1] = jnp.where(lo, 1.0, v).astype(BF16)

    head_of_row = lax.broadcasted_iota(jnp.int32, (rows_u, 1), 0) // BLOCK_Q

    def head_column(ref, e):
        col = jnp.zeros((rows_u, 1), F32)
        for r in range(R):
            col = jnp.where(head_of_row == r, ref[pid * 2 * R + e * R + r], col)
        return col

    slope_cols = [head_column(slope_ref, e) for e in range(2)]
    sink_cols = [head_column(sink_ref, e) for e in range(2)]
    rel = (lax.broadcasted_iota(jnp.int32, (rows_u, span), 0) % BLOCK_Q
           - lax.broadcasted_iota(jnp.int32, (rows_u, span), 1))

    def rows(j):
        return pl.ds(pl.multiple_of(j * BLOCK_Q, BLOCK_Q), BLOCK_Q)

    def window(j):
        start = jnp.clip(j * BLOCK_Q - SWA_WINDOW, 0, S - span)
        return pl.multiple_of(start, BLOCK_Q)

    def scores(j, e):
        keep = lo if e == 0 else hi
        parts = []
        for pb in range(e * (R // 2), (e + 1) * (R // 2)):
            qp = seg_norm(q_ref[rows(j), pb * LANES:(pb + 1) * LANES].astype(F32), gq_ref[...]) * (scale * LOG2E)
            qr = pltpu.roll(qp, SWA_HEAD_DIM, 1)
            for i in range(2):
                parts.append(jnp.where(keep, qp if i == e else qr, 0.0))
        q4 = jnp.concatenate(parts, axis=0).astype(BF16)
        start = window(j)
        s = _dot_nt(q4, k_scr[pl.ds(start, span), :])
        dist = jnp.abs(rel + (j * BLOCK_Q - start))
        s = jnp.where(dist <= SWA_WINDOW, s - slope_cols[e] * dist.astype(F32), -jnp.inf)
        _store_scores(s, s_bufs[e])

    def finish(j, e):
        p, m = _softmax_numerators(s_bufs[e], extra_logit=sink_cols[e])
        o = _dot(p, v_scr[e, pl.ds(window(j), span), :])
        den = pltpu.roll(o, SWA_HEAD_DIM, 1) + jnp.exp2(sink_cols[e] - m)
        return o / den

    def emit(j, o0, o1):
        for e, o in ((0, o0), (1, o1)):
            orot = pltpu.roll(o, SWA_HEAD_DIM, 1)
            for k in range(R // 2):
                pb = e * (R // 2) + k
                even = (o if e == 0 else orot)[2 * k * BLOCK_Q:(2 * k + 1) * BLOCK_Q]
                odd = (o if e == 1 else orot)[(2 * k + 1) * BLOCK_Q:(2 * k + 2) * BLOCK_Q]
                o_ref[rows(j), pb * LANES:(pb + 1) * LANES] = jnp.where(lo, even, odd).astype(o_ref.dtype)

    _two_unit_pipeline(S // BLOCK_Q, scores, finish, emit)


def swa_attention(qkv, slopes, sink, gq, gk, B, S):
    n_steps = SWA_KV_HEADS // 2
    R = SWA_HEADS // SWA_KV_HEADS
    span = BLOCK_Q + 2 * SWA_WINDOW
    qw = SWA_HEADS * SWA_HEAD_DIM // n_steps
    kbase = SWA_HEADS * SWA_HEAD_DIM // LANES
    smem = pl.BlockSpec(memory_space=pltpu.SMEM)
    gain = pl.BlockSpec((1, LANES), lambda b, g: (0, 0))
    kern = functools.partial(_swa_attn_kernel, scale=SWA_HEAD_DIM ** -0.5)
    return pl.pallas_call(
        kern,
        name="swa_attn",
        grid=(B, n_steps),
        in_specs=[
            smem, smem,
            pl.BlockSpec((S, qw), lambda b, g: (b, g)),
            pl.BlockSpec((S, LANES), lambda b, g: (b, kbase + g)),
            pl.BlockSpec((S, LANES), lambda b, g: (b, kbase + n_steps + g)),
            gain, gain,
        ],
        out_specs=pl.BlockSpec((S, qw), lambda b, g: (b, g)),
        out_shape=jax.ShapeDtypeStruct((B * S, SWA_HEADS * SWA_HEAD_DIM), BF16),
        scratch_shapes=[
            pltpu.VMEM((S, LANES), BF16),
            pltpu.VMEM((2, S, LANES), BF16),
            pltpu.VMEM((R * BLOCK_Q, span), F32),
            pltpu.VMEM((R * BLOCK_Q, span), F32),
        ],
        compiler_params=_cparams("parallel", "parallel", flags=ATTN_FLAGS),
    )(slopes, sink, qkv, qkv, qkv, gq, gk)


def _swiglu_accumulate(h, wg_ref, wu_ref, wd_ref, acc_ref):
    for c in range(wg_ref.shape[1] // FFN_SUB):
        sl = slice(c * FFN_SUB, (c + 1) * FFN_SUB)
        g = _dot(h, wg_ref[:, sl])
        u = _dot(h, wu_ref[:, sl])
        a = (g * jax.nn.sigmoid(g) * u).astype(BF16)
        acc_ref[...] += _dot(a, wd_ref[sl, :])


def _ple(x, p, gain, win_ref, wgate_ref):
    gate = jax.nn.sigmoid(_dot(_rms(x, gain).astype(BF16), wgate_ref[...]))
    return x + _dot(p.astype(BF16), win_ref[...]) * gate


def _dense_tail_kernel(x_ref, a_ref, wo_ref, g_ref, wg_ref, wu_ref, wd_ref, p_ref, gp_ref, win_ref, wgate_ref, o_ref,
                       acc_scr):
    x = x_ref[...] + _dot(a_ref[...], wo_ref[...])
    acc_scr[...] = x
    _swiglu_accumulate(_rms(x, g_ref[...]).astype(BF16), wg_ref, wu_ref, wd_ref, acc_scr)
    o_ref[...] = _ple(acc_scr[...], p_ref[...], gp_ref[...], win_ref, wgate_ref)


def dense_layer_tail(x, a, w_o, gain, wg, wu, wd, p, layer, ple_gain, w_in, w_gate):
    T, D = x.shape
    K = a.shape[1]
    Fd = wg.shape[1]
    P = p.shape[2]
    row = lambda n: pl.BlockSpec((ROW_TILE, n), lambda i: (i, 0))
    return pl.pallas_call(
        _dense_tail_kernel,
        name="dense_tail",
        grid=(T // ROW_TILE,),
        in_specs=[
            row(D), row(K), _resident((K, D)),
            _resident((1, D)), _resident((D, Fd)), _resident((D, Fd)), _resident((Fd, D)),
            pl.BlockSpec((None, ROW_TILE, P), lambda i: (layer, i, 0)),
            _resident((1, D)), _resident((P, D)), _resident((D, D)),
        ],
        out_specs=row(D),
        out_shape=jax.ShapeDtypeStruct((T, D), F32),
        scratch_shapes=[pltpu.VMEM((ROW_TILE, D), F32)],
        compiler_params=_cparams("parallel"),
    )(x, a, w_o, gain.reshape(1, D), wg, wu, wd, p, ple_gain.reshape(1, D), w_in, w_gate)


def _router_kernel(x_ref, a_ref, wo_ref, g_ref, whi_ref, wlo_ref, b_ref, x1_ref, h_ref, idx_ref, wt_ref, cnt_ref):
    x1 = x_ref[...] + _dot(a_ref[...], wo_ref[...])
    x1_ref[...] = x1
    hf = _rms(x1, g_ref[...])
    h_hi = hf.astype(BF16)
    h_lo = (hf - h_hi.astype(F32)).astype(BF16)
    h_ref[...] = h_hi
    logits = _dot(h_hi, whi_ref[...]) + _dot(h_hi, wlo_ref[...]) + _dot(h_lo, whi_ref[...]) + b_ref[...]
    lane = lax.broadcasted_iota(jnp.int32, logits.shape, 1)
    logits = jnp.where(lane < N_EXPERTS, logits, -jnp.inf)
    m1 = jnp.max(logits, axis=-1, keepdims=True)
    i1 = jnp.min(jnp.where(logits == m1, lane, LANES), axis=-1, keepdims=True)
    rest = jnp.where(lane == i1, -jnp.inf, logits)
    m2 = jnp.max(rest, axis=-1, keepdims=True)
    i2 = jnp.min(jnp.where(rest == m2, lane, LANES), axis=-1, keepdims=True)
    e2 = jnp.exp(m2 - m1)
    w1 = 1.0 / (1.0 + e2)
    w2 = e2 / (1.0 + e2)
    wt_ref[...] = jnp.where(lane == 0, w1, jnp.where(lane == 1, w2, 0.0))
    onehot = jnp.where(jnp.logical_or(lane == i1, lane == i2), 1.0, 0.0)
    tm = onehot.shape[0]
    earlier = (lax.broadcasted_iota(jnp.int32, (tm, tm), 0) > lax.broadcasted_iota(jnp.int32, (tm, tm), 1))
    prefix = _dot(jnp.where(earlier, 1.0, 0.0).astype(BF16), onehot.astype(BF16))
    r1 = jnp.sum(jnp.where(lane == i1, prefix, 0.0), axis=-1, keepdims=True).astype(jnp.int32)
    r2 = jnp.sum(jnp.where(lane == i2, prefix, 0.0), axis=-1, keepdims=True).astype(jnp.int32)
    idx_ref[...] = jnp.where(lane == 0, i1, jnp.where(lane == 1, i2, jnp.where(lane == 2, r1, jnp.where(lane == 3, r2, 0))))
    cnt_ref[...] = jnp.broadcast_to(jnp.sum(onehot, axis=0, keepdims=True), cnt_ref.shape)


def moe_router(x, a, w_o, gain, w_hi, w_lo, bias):
    T, D = x.shape
    K = a.shape[1]
    row = lambda n: pl.BlockSpec((ROW_TILE, n), lambda i: (i, 0))
    return pl.pallas_call(
        _router_kernel,
        name="moe_router",
        grid=(T // ROW_TILE,),
        in_specs=[row(D), row(K), _resident((K, D)), _resident((1, D)), _resident((D, LANES)), _resident((D, LANES)),
                  _resident((1, LANES))],
        out_specs=[row(D), row(D), row(LANES), row(LANES), pl.BlockSpec((SUBLANES, LANES), lambda i: (i, 0))],
        out_shape=[
            jax.ShapeDtypeStruct((T, D), F32),
            jax.ShapeDtypeStruct((T, D), BF16),
            jax.ShapeDtypeStruct((T, LANES), jnp.int32),
            jax.ShapeDtypeStruct((T, LANES), F32),
            jax.ShapeDtypeStruct((T // ROW_TILE * SUBLANES, LANES), F32),
        ],
        compiler_params=_cparams("parallel"),
    )(x, a, w_o, gain.reshape(1, D), w_hi, w_lo, bias)


def _moe_ffn_kernel(te_ref, tv_ref, h_ref, wg_ref, wu_ref, wd_ref, o_ref, acc_scr):
    i = pl.program_id(0)
    f = pl.program_id(1)
    last = pl.num_programs(1) - 1

    @pl.when(f == 0)
    def _():
        acc_scr[...] = jnp.zeros_like(acc_scr)

    @pl.when(tv_ref[i] > 0)
    def _():
        _swiglu_accumulate(h_ref[...], wg_ref, wu_ref, wd_ref, acc_scr)

    @pl.when(f == last)
    def _():
        o_ref[...] = acc_scr[...].astype(o_ref.dtype)


def moe_ffn(h_sorted, tile_expert, tile_valid, wg, wu, wd, n_chunks):
    R, D = h_sorted.shape
    E, _, Fe = wg.shape
    tf = Fe // n_chunks
    grid_spec = pltpu.PrefetchScalarGridSpec(
        num_scalar_prefetch=2,
        grid=(R // MOE_ROW_TILE, n_chunks),
        in_specs=[
            pl.BlockSpec((MOE_ROW_TILE, D), lambda i, f, te, tv: (i, 0)),
            pl.BlockSpec((None, D, tf), lambda i, f, te, tv: (te[i], 0, f)),
            pl.BlockSpec((None, D, tf), lambda i, f, te, tv: (te[i], 0, f)),
            pl.BlockSpec((None, tf, D), lambda i, f, te, tv: (te[i], f, 0)),
        ],
        out_specs=pl.BlockSpec((MOE_ROW_TILE, D), lambda i, f, te, tv: (i, 0)),
        scratch_shapes=[pltpu.VMEM((MOE_ROW_TILE, D), F32)],
    )
    return pl.pallas_call(
        _moe_ffn_kernel,
        name="moe_ffn",
        grid_spec=grid_spec,
        out_shape=jax.ShapeDtypeStruct((R, D), BF16),
        compiler_params=_cparams("parallel", "arbitrary"),
    )(tile_expert, tile_valid, h_sorted, wg, wu, wd)


def _moe_tail_kernel(x_ref, y0_ref, y1_ref, wt_ref, p_ref, g_ref, win_ref, wgate_ref, o_ref):
    wt = wt_ref[...]
    x = x_ref[...] + wt[:, 0:1] * y0_ref[...].astype(F32) + wt[:, 1:2] * y1_ref[...].astype(F32)
    o_ref[...] = _ple(x, p_ref[...], g_ref[...], win_ref, wgate_ref)


def moe_layer_tail(x, y0, y1, wt, p, layer, gain, w_in, w_gate):
    T, D = x.shape
    P = p.shape[2]
    row = lambda n: pl.BlockSpec((ROW_TILE, n), lambda i: (i, 0))
    return pl.pallas_call(
        _moe_tail_kernel,
        name="moe_tail",
        grid=(T // ROW_TILE,),
        in_specs=[row(D), row(D), row(D), row(LANES), pl.BlockSpec((None, ROW_TILE, P), lambda i: (layer, i, 0)),
                  _resident((1, D)), _resident((P, D)), _resident((D, D))],
        out_specs=row(D),
        out_shape=jax.ShapeDtypeStruct((T, D), F32),
        compiler_params=_cparams("parallel"),
    )(x, y0, y1, wt, p, gain.reshape(1, D), w_in, w_gate)


def _rope_cos_sin(pos, dim, theta):
    inv = theta ** (-jnp.arange(0, dim, 2, dtype=F32) / dim)
    ang = pos.astype(F32)[:, None] * inv[None, :]
    return jnp.cos(ang), jnp.sin(ang)


def _fold_tables(gain_lanes, cos_lanes, sin_lanes, partner):
    return gain_lanes[None, :] * cos_lanes, gain_lanes[partner][None, :] * sin_lanes


def _axial_tables(S, q_gain, k_gain):
    pos = jnp.arange(S)
    cr, sr = _rope_cos_sin(pos // GRID_W, AX_HEAD_DIM // 2, AX_THETA)
    cc, sc = _rope_cos_sin(pos % GRID_W, AX_HEAD_DIM // 2, AX_THETA)
    cos = jnp.concatenate([cr, cr, cc, cc], axis=1)
    sin = jnp.concatenate([-sr, sr, -sc, sc], axis=1)
    lane = np.arange(LANES)
    partner = np.where(lane % 64 < 32, lane + 32, lane - 32)
    return _fold_tables(q_gain, cos, sin, partner) + _fold_tables(k_gain, cos, sin, partner)


def _mla_tables(S, q_gain, k_gain):
    c, s = _rope_cos_sin(jnp.arange(S), MLA_ROPE, MLA_THETA)
    pad = LANES - MLA_QK
    cos = jnp.concatenate([jnp.ones((S, MLA_NOPE), F32), c, c, jnp.ones((S, pad), F32)], axis=1)
    sin = jnp.concatenate([jnp.zeros((S, MLA_NOPE), F32), -s, s, jnp.zeros((S, pad), F32)], axis=1)
    lane = np.arange(LANES)
    half = MLA_ROPE // 2
    partner = np.where((lane >= MLA_NOPE) & (lane < MLA_NOPE + half), lane + half,
                       np.where((lane >= MLA_NOPE + half) & (lane < MLA_QK), lane - half, lane))
    zpad = jnp.zeros((pad,), F32)
    gq = jnp.concatenate([q_gain, zpad])
    gk = jnp.concatenate([k_gain, zpad])
    return _fold_tables(gq, cos, sin, partner) + _fold_tables(gk, cos, sin, partner)


def _moe_dispatch(idx, cnt, n_tiles):
    T = idx.shape[0]
    tm = MOE_ROW_TILE
    n_rt = T // ROW_TILE
    e_pair = idx[:, :TOP_K]
    cnt = cnt.reshape(n_rt, SUBLANES, LANES)[:, 0, :N_EXPERTS].astype(jnp.int32)
    rt = jnp.arange(n_rt)
    before = jnp.sum(jnp.where((rt[None, :] < rt[:, None])[:, :, None], cnt[None, :, :], 0), axis=1)
    counts = jnp.sum(cnt, axis=0)
    padded = ((counts + tm - 1) // tm) * tm
    ex = jnp.arange(N_EXPERTS)
    ends = jnp.sum(jnp.where(ex[None, :] <= ex[:, None], padded[None, :], 0), axis=1)
    base = jnp.repeat((ends - padded)[None, :] + before, ROW_TILE, axis=0)
    dest = jnp.sum(jnp.where(e_pair[:, :, None] == ex[None, None, :], base[:, None, :], 0), axis=-1) + idx[:, TOP_K:2 * TOP_K]
    src_tok = jnp.zeros((n_tiles * tm,), jnp.int32).at[dest.reshape(-1)].set(jnp.arange(TOP_K * T, dtype=jnp.int32) // TOP_K)
    tile_start = jnp.arange(n_tiles, dtype=jnp.int32) * tm
    tile_expert = jnp.minimum(jnp.sum((tile_start[:, None] >= ends[None, :]).astype(jnp.int32), axis=1), N_EXPERTS - 1)
    tile_valid = (tile_start < ends[-1]).astype(jnp.int32)
    return dest, src_tok, tile_expert, tile_valid


def kernel(x, p, attn_norm, ffn_norm, ple_norm, ple_w_in, ple_w_gate, mla_w_down, mla_q_norm, mla_w_uq, mla_kv_norm, mla_w_ukv, mla_q_gain, mla_k_gain, mla_w_o, swa_w_qkv, swa_q_gain, swa_k_gain, swa_sink, swa_w_o, ax_w_qkv, ax_q_gain, ax_k_gain, ax_w_o, ffn_w_gate, ffn_w_up, ffn_w_down, moe_w_router, moe_b_router, moe_w_gate, moe_w_up, moe_w_down):
    B, S, D = x.shape
    depth = p.shape[0]
    T = B * S
    xt = x.reshape(T, D)
    bf = lambda a: a.astype(BF16)
    n_moe_tiles = (TOP_K * T) // MOE_ROW_TILE + N_EXPERTS
    slopes = jnp.asarray(2.0 ** (-8.0 * np.arange(1, SWA_HEADS + 1) / SWA_HEADS) * LOG2E, dtype=F32)
    p3 = p.reshape(depth, T, -1)

    for i in range(depth):
        kind = i % N_MIXERS
        j = i // N_MIXERS
        if kind == 0:
            wd = mla_w_down[j]
            zc = lambda n: jnp.zeros((D, n), F32)
            wd = jnp.concatenate([wd[:, :MLA_Q_RANK + MLA_KV_RANK], zc(MLA_NOPE), wd[:, MLA_Q_RANK + MLA_KV_RANK:],
                                  zc(LANES - MLA_QK)], axis=1)
            wuq = mla_w_uq[j].reshape(MLA_Q_RANK, MLA_HEADS, MLA_QK)
            wuq = jnp.pad(wuq, ((0, 0), (0, 0), (0, LANES - MLA_QK))).reshape(MLA_Q_RANK, MLA_HEADS * LANES)
            q, kv, kr = mla_proj(xt, attn_norm[i], bf(wd), mla_q_norm[j], mla_kv_norm[j], bf(wuq), bf(mla_w_ukv[j]))
            o = mla_attention(q, kv, kr, _mla_tables(S, mla_q_gain[j], mla_k_gain[j]), B, S)
            w_o = mla_w_o[j]
        elif kind == 1:
            qkv = norm_proj(xt, attn_norm[i], bf(swa_w_qkv[j]))
            gq = jnp.tile(swa_q_gain[j], 2).reshape(1, LANES)
            gk = jnp.tile(swa_k_gain[j], 2).reshape(1, LANES)
            o = swa_attention(qkv, slopes, swa_sink[j].astype(F32) * LOG2E, gq, gk, B, S)
            w_o = swa_w_o[j]
        else:
            qkv = norm_proj(xt, attn_norm[i], bf(ax_w_qkv[j]))
            o = axial_attention(qkv, _axial_tables(S, ax_q_gain[j], ax_k_gain[j]), B, S)
            w_o = ax_w_o[j]
        f = i // 2
        if i % 2 == 0:
            xt = dense_layer_tail(xt, o, bf(w_o), ffn_norm[i], bf(ffn_w_gate[f]), bf(ffn_w_up[f]), bf(ffn_w_down[f]),
                                  p3, i, ple_norm[i], bf(ple_w_in[i]), bf(ple_w_gate[i]))
        else:
            wr = jnp.pad(moe_w_router[f], ((0, 0), (0, LANES - N_EXPERTS)))
            wr_hi = bf(wr)
            wr_lo = bf(wr - wr_hi.astype(F32))
            br = jnp.pad(moe_b_router[f].astype(F32), (0, LANES - N_EXPERTS)).reshape(1, LANES)
            xt, h, idx, wt, cnt = moe_router(xt, o, bf(w_o), ffn_norm[i], wr_hi, wr_lo, br)
            dest, src_tok, tile_expert, tile_valid = _moe_dispatch(idx, cnt, n_moe_tiles)
            h_sorted = jnp.take(h, src_tok, axis=0)
            y = moe_ffn(h_sorted, tile_expert, tile_valid, bf(moe_w_gate[f]), bf(moe_w_up[f]), bf(moe_w_down[f]), 2)
            xt = moe_layer_tail(xt, jnp.take(y, dest[:, 0], axis=0), jnp.take(y, dest[:, 1], axis=0), wt,
                                p3, i, ple_norm[i], bf(ple_w_in[i]), bf(ple_w_gate[i]))
    return xt.reshape(B, S, D)
```

```python
import functools

import numpy as np
import jax
import jax.numpy as jnp
from jax import lax
from jax.experimental import pallas as pl
from jax.experimental.pallas import tpu as pltpu

F32 = jnp.float32
BF16 = jnp.bfloat16

EPS = 1e-6
GRID_W = 64
BLOCK_Q = 128

MLA_HEADS = 16
MLA_NOPE = 64
MLA_ROPE = 32
MLA_V = 64
MLA_Q_RANK = 256
MLA_KV_RANK = 128
MLA_THETA = 10000.0
MLA_QK = MLA_NOPE + MLA_ROPE

SWA_HEADS = 16
SWA_KV_HEADS = 4
SWA_HEAD_DIM = 64
SWA_WINDOW = 128

AX_HEADS = 8
AX_KV_HEADS = 4
AX_HEAD_DIM = 128
AX_THETA = 10000.0

N_EXPERTS = 8
TOP_K = 2
N_MIXERS = 3

LANES = 128
SUBLANES = 8
ROW_TILE = 512
MOE_ROW_TILE = 512
CHUNK_ALIGN = 2 * SUBLANES
ATTN_Q_TILE = 256
FFN_SUB = 256
VMEM_LIMIT = 56 * 1024 * 1024
LOG2E = 1.4426950408889634


def _cparams(*sem, flags=None):
    return pltpu.CompilerParams(dimension_semantics=sem, vmem_limit_bytes=VMEM_LIMIT, flags=flags)


ATTN_FLAGS = None


def _rms(xf, gain):
    ms = jnp.mean(xf * xf, axis=-1, keepdims=True)
    return xf * lax.rsqrt(ms + EPS) * gain


def _dot(a, b):
    return jnp.dot(a, b, preferred_element_type=F32)


def _dot_nt(a, b):
    return lax.dot_general(a, b, (((1,), (1,)), ((), ())), preferred_element_type=F32)


def _resident(shape):
    return pl.BlockSpec(shape, lambda *_: (0,) * len(shape), pipeline_mode=pl.Buffered(1))


def _norm_proj_kernel(x_ref, g_ref, w_ref, o_ref):
    h = _rms(x_ref[...], g_ref[...]).astype(BF16)
    o_ref[...] = _dot(h, w_ref[...]).astype(o_ref.dtype)


def norm_proj(x, gain, w):
    T, D = x.shape
    N = w.shape[1]
    return pl.pallas_call(
        _norm_proj_kernel,
        name="norm_proj",
        grid=(T // ROW_TILE,),
        in_specs=[
            pl.BlockSpec((ROW_TILE, D), lambda i: (i, 0)),
            _resident((1, D)),
            _resident((D, N)),
        ],
        out_specs=pl.BlockSpec((ROW_TILE, N), lambda i: (i, 0)),
        out_shape=jax.ShapeDtypeStruct((T, N), BF16),
        compiler_params=_cparams("parallel"),
    )(x, gain.reshape(1, D), w)


def _mla_proj_kernel(x_ref, g_ref, wd_ref, qn_ref, kvn_ref, wuq_ref, wukv_ref, q_ref, kv_ref, kr_ref):
    h = _rms(x_ref[...], g_ref[...]).astype(BF16)
    down = _dot(h, wd_ref[...])
    cq = _rms(down[:, :MLA_Q_RANK], qn_ref[...]).astype(BF16)
    ckv = _rms(down[:, MLA_Q_RANK:MLA_Q_RANK + MLA_KV_RANK], kvn_ref[...]).astype(BF16)
    q_ref[...] = _dot(cq, wuq_ref[...]).astype(BF16)
    kv_ref[...] = _dot(ckv, wukv_ref[...]).astype(BF16)
    kr_ref[...] = down[:, MLA_Q_RANK + MLA_KV_RANK:]


def mla_proj(x, gain, wd, qn, kvn, wuq, wukv):
    T, D = x.shape
    nd = wd.shape[1]
    nq = wuq.shape[1]
    nkv = wukv.shape[1]
    row = lambda n: pl.BlockSpec((ROW_TILE, n), lambda i: (i, 0))
    return pl.pallas_call(
        _mla_proj_kernel,
        name="mla_proj",
        grid=(T // ROW_TILE,),
        in_specs=[
            row(D),
            _resident((1, D)),
            _resident((D, nd)),
            _resident((1, MLA_Q_RANK)),
            _resident((1, MLA_KV_RANK)),
            _resident((MLA_Q_RANK, nq)),
            _resident((MLA_KV_RANK, nkv)),
        ],
        out_specs=[row(nq), row(nkv), row(LANES)],
        out_shape=[
            jax.ShapeDtypeStruct((T, nq), BF16),
            jax.ShapeDtypeStruct((T, nkv), BF16),
            jax.ShapeDtypeStruct((T, LANES), F32),
        ],
        compiler_params=_cparams("parallel"),
    )(x, gain.reshape(1, D), wd, qn.reshape(1, -1), kvn.reshape(1, -1), wuq, wukv)


def _two_unit_pipeline(n, scores, finish, emit):
    def step(i, prefetch):
        scores(i, 1)
        o0 = finish(i, 0)
        if prefetch:
            scores(i + 1, 0)
        o1 = finish(i, 1)
        emit(i, o0, o1)

    scores(0, 0)
    if n > 1:
        def body(i, carry):
            step(i, True)
            return carry

        lax.fori_loop(0, n - 1, body, 0)
    step(n - 1, False)


def _store_scores(s, s_buf, m_buf=None):
    s_buf[...] = s
    if m_buf is not None:
        m_buf[...] = jnp.max(s, axis=-1, keepdims=True)


def _softmax_numerators(s_buf, m_buf=None, extra_logit=None):
    m = m_buf[...] if m_buf is not None else jnp.max(s_buf[...], axis=-1, keepdims=True)
    if extra_logit is not None:
        m = jnp.maximum(m, extra_logit)
    return jnp.exp2(s_buf[...] - m).astype(BF16), m


def _axial_attn_kernel(q_ref, k_ref, v_ref, aq_ref, bq_ref, ak_ref, bk_ref, o_ref, k_scr, v_scr, s0_scr, s1_scr,
                       *, tq, scale):
    S = k_ref.shape[0]
    s_bufs = (s0_scr, s1_scr)
    lane = lax.broadcasted_iota(jnp.int32, (1, LANES), 1)
    first = (lane % 64) < 32

    def norm_rope(xf, a, b, extra):
        c = lax.rsqrt(jnp.mean(xf * xf, axis=-1, keepdims=True) + EPS) * extra
        partner = jnp.where(first, pltpu.roll(xf, LANES - 32, 1), pltpu.roll(xf, 32, 1))
        return (xf * a + partner * b) * c

    k_scr[...] = norm_rope(k_ref[...].astype(F32), ak_ref[...], bk_ref[...], 1.0).astype(BF16)
    v_scr[:, :LANES] = v_ref[...]
    v_scr[:, LANES:] = jnp.ones((S, LANES), BF16)

    def rows(i):
        return pl.ds(pl.multiple_of(i * tq, tq), tq)

    def scores(i, u):
        q = norm_rope(q_ref[rows(i), u * LANES:(u + 1) * LANES].astype(F32), aq_ref[rows(i), :], bq_ref[rows(i), :],
                      scale * LOG2E)
        _store_scores(_dot_nt(q.astype(BF16), k_scr[...]), s_bufs[u])

    def finish(i, u):
        p, _ = _softmax_numerators(s_bufs[u])
        o = _dot(p, v_scr[...])
        return o[:, :LANES] / o[:, LANES:]

    def emit(i, o0, o1):
        o_ref[rows(i), :LANES] = o0.astype(o_ref.dtype)
        o_ref[rows(i), LANES:] = o1.astype(o_ref.dtype)

    _two_unit_pipeline(S // tq, scores, finish, emit)


def axial_attention(qkv, tabs, B, S):
    R = AX_HEADS // AX_KV_HEADS
    assert R == 2
    tq = min(ATTN_Q_TILE, S)
    kern = functools.partial(_axial_attn_kernel, tq=tq, scale=AX_HEAD_DIM ** -0.5)
    tab = pl.BlockSpec((S, LANES), lambda b, g: (0, 0), pipeline_mode=pl.Buffered(1))
    return pl.pallas_call(
        kern,
        name="axial_attn",
        grid=(B, AX_KV_HEADS),
        in_specs=[
            pl.BlockSpec((S, R * LANES), lambda b, g: (b, g)),
            pl.BlockSpec((S, LANES), lambda b, g: (b, AX_HEADS + g)),
            pl.BlockSpec((S, LANES), lambda b, g: (b, AX_HEADS + AX_KV_HEADS + g)),
            tab, tab, tab, tab,
        ],
        out_specs=pl.BlockSpec((S, R * LANES), lambda b, g: (b, g)),
        out_shape=jax.ShapeDtypeStruct((B * S, AX_HEADS * AX_HEAD_DIM), BF16),
        scratch_shapes=[
            pltpu.VMEM((S, LANES), BF16),
            pltpu.VMEM((S, 2 * LANES), BF16),
            pltpu.VMEM((tq, S), F32),
            pltpu.VMEM((tq, S), F32),
        ],
        compiler_params=_cparams("parallel", "parallel", flags=ATTN_FLAGS),
    )(qkv, qkv, qkv, *tabs)


def _mla_attn_kernel(q_ref, kv_ref, kr_ref, aq_ref, bq_ref, ak_ref, bk_ref, o_ref, k_scr, v_scr, s0_scr, s1_scr,
                     m0_scr, m1_scr, *, tq, scale):
    S = kv_ref.shape[0]
    s_bufs = (s0_scr, s1_scr)
    m_bufs = (m0_scr, m1_scr)
    lane = lax.broadcasted_iota(jnp.int32, (1, LANES), 1)
    lo = lane < MLA_NOPE
    first = lane < MLA_NOPE + MLA_ROPE // 2
    half = MLA_ROPE // 2

    def norm_rope(xf, a, b, extra):
        ms = jnp.sum(xf * xf, axis=-1, keepdims=True) * (1.0 / MLA_QK)
        c = lax.rsqrt(ms + EPS) * extra
        partner = jnp.where(first, pltpu.roll(xf, LANES - half, 1), pltpu.roll(xf, half, 1))
        return (xf * a + partner * b) * c

    kr = kr_ref[...]
    for hh in range(2):
        kvh = kv_ref[:, hh * LANES:(hh + 1) * LANES].astype(F32)
        k = jnp.where(lo, kvh, kr)
        k_scr[hh] = norm_rope(k, ak_ref[...], bk_ref[...], 1.0).astype(BF16)
        vh = jnp.where(lo, pltpu.roll(kvh, MLA_V, 1), 1.0) if hh == 0 else jnp.where(lo, 1.0, kvh)
        v_scr[hh] = vh.astype(BF16)

    def rows(i):
        return pl.ds(pl.multiple_of(i * tq, tq), tq)

    def scores(i, u):
        q = norm_rope(q_ref[rows(i), u * LANES:(u + 1) * LANES].astype(F32), aq_ref[rows(i), :], bq_ref[rows(i), :],
                      scale * LOG2E)
        _store_scores(_dot_nt(q.astype(BF16), k_scr[u]), s_bufs[u], m_bufs[u])

    def finish(i, u):
        p, _ = _softmax_numerators(s_bufs[u], m_bufs[u])
        o = _dot(p, v_scr[u])
        return o / pltpu.roll(o, MLA_V, 1)

    def emit(i, o0, o1):
        o_ref[rows(i), :] = jnp.where(lo, o0, o1).astype(o_ref.dtype)

    _two_unit_pipeline(S // tq, scores, finish, emit)


def mla_attention(q, kv, kr, tabs, B, S):
    tq = min(ATTN_Q_TILE, S)
    kern = functools.partial(_mla_attn_kernel, tq=tq, scale=MLA_QK ** -0.5)
    tab = pl.BlockSpec((S, LANES), lambda b, g: (0, 0), pipeline_mode=pl.Buffered(1))
    return pl.pallas_call(
        kern,
        name="mla_attn",
        grid=(B, MLA_HEADS // 2),
        in_specs=[
            pl.BlockSpec((S, 2 * LANES), lambda b, g: (b, g)),
            pl.BlockSpec((S, 2 * LANES), lambda b, g: (b, g)),
            pl.BlockSpec((S, LANES), lambda b, g: (b, 0)),
            tab, tab, tab, tab,
        ],
        out_specs=pl.BlockSpec((S, LANES), lambda b, g: (b, g)),
        out_shape=jax.ShapeDtypeStruct((B * S, MLA_HEADS * MLA_V), BF16),
        scratch_shapes=[
            pltpu.VMEM((2, S, LANES), BF16),
            pltpu.VMEM((2, S, LANES), BF16),
            pltpu.VMEM((tq, S), F32),
            pltpu.VMEM((tq, S), F32),
            pltpu.VMEM((tq, 1), F32),
            pltpu.VMEM((tq, 1), F32),
        ],
        compiler_params=_cparams("parallel", "parallel", flags=ATTN_FLAGS),
    )(q, kv, kr, *tabs)


def _swa_attn_kernel(slope_ref, sink_ref, q_ref, k_ref, v_ref, gq_ref, gk_ref, o_ref, k_scr, v_scr, s0_scr, s1_scr,
                     *, scale):
    S = k_ref.shape[0]
    span = BLOCK_Q + 2 * SWA_WINDOW
    R = SWA_HEADS // SWA_KV_HEADS
    rows_u = R * BLOCK_Q
    s_bufs = (s0_scr, s1_scr)
    pid = pl.program_id(1)
    lane = lax.broadcasted_iota(jnp.int32, (1, LANES), 1)
    lo = lane < SWA_HEAD_DIM
    hi = jnp.logical_not(lo)

    def seg_norm(xf, gain):
        sq = xf * xf
        s_lo = jnp.sum(jnp.where(lo, sq, 0.0), axis=-1, keepdims=True)
        s_hi = jnp.sum(jnp.where(lo, 0.0, sq), axis=-1, keepdims=True)
        inv = 1.0 / SWA_HEAD_DIM
        c = jnp.where(lo, lax.rsqrt(s_lo * inv + EPS), lax.rsqrt(s_hi * inv + EPS))
        return xf * c * gain

    k_scr[...] = seg_norm(k_ref[...].astype(F32), gk_ref[...]).astype(BF16)
    v = v_ref[...].astype(F32)
    v_scr[0] = jnp.where(lo, v, 1.0).astype(BF16)
    v_scr[1] = jnp.where(lo, 1.0, v).astype(BF16)

    head_of_row = lax.broadcasted_iota(jnp.int32, (rows_u, 1), 0) // BLOCK_Q

    def head_column(ref, e):
        col = jnp.zeros((rows_u, 1), F32)
        for r in range(R):
            col = jnp.where(head_of_row == r, ref[pid * 2 * R + e * R + r], col)
        return col

    slope_cols = [head_column(slope_ref, e) for e in range(2)]
    sink_cols = [head_column(sink_ref, e) for e in range(2)]
    rel = (lax.broadcasted_iota(jnp.int32, (rows_u, span), 0) % BLOCK_Q
           - lax.broadcasted_iota(jnp.int32, (rows_u, span), 1))

    def rows(j):
        return pl.ds(pl.multiple_of(j * BLOCK_Q, BLOCK_Q), BLOCK_Q)

    def window(j):
        start = jnp.clip(j * BLOCK_Q - SWA_WINDOW, 0, S - span)
        return pl.multiple_of(start, BLOCK_Q)

    def scores(j, e):
        keep = lo if e == 0 else hi
        parts = []
        for pb in range(e * (R // 2), (e + 1) * (R // 2)):
            qp = seg_norm(q_ref[rows(j), pb * LANES:(pb + 1) * LANES].astype(F32), gq_ref[...]) * (scale * LOG2E)
            qr = pltpu.roll(qp, SWA_HEAD_DIM, 1)
            for i in range(2):
                parts.append(jnp.where(keep, qp if i == e else qr, 0.0))
        q4 = jnp.concatenate(parts, axis=0).astype(BF16)
        start = window(j)
        s = _dot_nt(q4, k_scr[pl.ds(start, span), :])
        dist = jnp.abs(rel + (j * BLOCK_Q - start))
        s = jnp.where(dist <= SWA_WINDOW, s - slope_cols[e] * dist.astype(F32), -jnp.inf)
        _store_scores(s, s_bufs[e])

    def finish(j, e):
        p, m = _softmax_numerators(s_bufs[e], extra_logit=sink_cols[e])
        o = _dot(p, v_scr[e, pl.ds(window(j), span), :])
        den = pltpu.roll(o, SWA_HEAD_DIM, 1) + jnp.exp2(sink_cols[e] - m)
        return o / den

    def emit(j, o0, o1):
        for e, o in ((0, o0), (1, o1)):
            orot = pltpu.roll(o, SWA_HEAD_DIM, 1)
            for k in range(R // 2):
                pb = e * (R // 2) + k
                even = (o if e == 0 else orot)[2 * k * BLOCK_Q:(2 * k + 1) * BLOCK_Q]
                odd = (o if e == 1 else orot)[(2 * k + 1) * BLOCK_Q:(2 * k + 2) * BLOCK_Q]
                o_ref[rows(j), pb * LANES:(pb + 1) * LANES] = jnp.where(lo, even, odd).astype(o_ref.dtype)

    _two_unit_pipeline(S // BLOCK_Q, scores, finish, emit)


def swa_attention(qkv, slopes, sink, gq, gk, B, S):
    n_steps = SWA_KV_HEADS // 2
    R = SWA_HEADS // SWA_KV_HEADS
    span = BLOCK_Q + 2 * SWA_WINDOW
    qw = SWA_HEADS * SWA_HEAD_DIM // n_steps
    kbase = SWA_HEADS * SWA_HEAD_DIM // LANES
    smem = pl.BlockSpec(memory_space=pltpu.SMEM)
    gain = pl.BlockSpec((1, LANES), lambda b, g: (0, 0))
    kern = functools.partial(_swa_attn_kernel, scale=SWA_HEAD_DIM ** -0.5)
    return pl.pallas_call(
        kern,
        name="swa_attn",
        grid=(B, n_steps),
        in_specs=[
            smem, smem,
            pl.BlockSpec((S, qw), lambda b, g: (b, g)),
            pl.BlockSpec((S, LANES), lambda b, g: (b, kbase + g)),
            pl.BlockSpec((S, LANES), lambda b, g: (b, kbase + n_steps + g)),
            gain, gain,
        ],
        out_specs=pl.BlockSpec((S, qw), lambda b, g: (b, g)),
        out_shape=jax.ShapeDtypeStruct((B * S, SWA_HEADS * SWA_HEAD_DIM), BF16),
        scratch_shapes=[
            pltpu.VMEM((S, LANES), BF16),
            pltpu.VMEM((2, S, LANES), BF16),
            pltpu.VMEM((R * BLOCK_Q, span), F32),
            pltpu.VMEM((R * BLOCK_Q, span), F32),
        ],
        compiler_params=_cparams("parallel", "parallel", flags=ATTN_FLAGS),
    )(slopes, sink, qkv, qkv, qkv, gq, gk)


def _swiglu_accumulate(h, wg_ref, wu_ref, wd_ref, acc_ref):
    for c in range(wg_ref.shape[1] // FFN_SUB):
        sl = slice(c * FFN_SUB, (c + 1) * FFN_SUB)
        g = _dot(h, wg_ref[:, sl])
        u = _dot(h, wu_ref[:, sl])
        a = (g * jax.nn.sigmoid(g) * u).astype(BF16)
        acc_ref[...] += _dot(a, wd_ref[sl, :])


def _ple(x, p, gain, win_ref, wgate_ref):
    gate = jax.nn.sigmoid(_dot(_rms(x, gain).astype(BF16), wgate_ref[...]))
    return x + _dot(p.astype(BF16), win_ref[...]) * gate


def _dense_tail_kernel(x_ref, a_ref, wo_ref, g_ref, wg_ref, wu_ref, wd_ref, p_ref, gp_ref, win_ref, wgate_ref, o_ref,
                       acc_scr):
    x = x_ref[...] + _dot(a_ref[...], wo_ref[...])
    acc_scr[...] = x
    _swiglu_accumulate(_rms(x, g_ref[...]).astype(BF16), wg_ref, wu_ref, wd_ref, acc_scr)
    o_ref[...] = _ple(acc_scr[...], p_ref[...], gp_ref[...], win_ref, wgate_ref)


def dense_layer_tail(x, a, w_o, gain, wg, wu, wd, p, layer, ple_gain, w_in, w_gate):
    T, D = x.shape
    K = a.shape[1]
    Fd = wg.shape[1]
    P = p.shape[2]
    row = lambda n: pl.BlockSpec((ROW_TILE, n), lambda i: (i, 0))
    return pl.pallas_call(
        _dense_tail_kernel,
        name="dense_tail",
        grid=(T // ROW_TILE,),
        in_specs=[
            row(D), row(K), _resident((K, D)),
            _resident((1, D)), _resident((D, Fd)), _resident((D, Fd)), _resident((Fd, D)),
            pl.BlockSpec((None, ROW_TILE, P), lambda i: (layer, i, 0)),
            _resident((1, D)), _resident((P, D)), _resident((D, D)),
        ],
        out_specs=row(D),
        out_shape=jax.ShapeDtypeStruct((T, D), F32),
        scratch_shapes=[pltpu.VMEM((ROW_TILE, D), F32)],
        compiler_params=_cparams("parallel"),
    )(x, a, w_o, gain.reshape(1, D), wg, wu, wd, p, ple_gain.reshape(1, D), w_in, w_gate)


def _router_kernel(x_ref, a_ref, wo_ref, g_ref, whi_ref, wlo_ref, b_ref, x1_ref, h_ref, idx_ref, wt_ref, cnt_ref):
    x1 = x_ref[...] + _dot(a_ref[...], wo_ref[...])
    x1_ref[...] = x1
    hf = _rms(x1, g_ref[...])
    h_hi = hf.astype(BF16)
    h_lo = (hf - h_hi.astype(F32)).astype(BF16)
    h_ref[...] = h_hi
    logits = _dot(h_hi, whi_ref[...]) + _dot(h_hi, wlo_ref[...]) + _dot(h_lo, whi_ref[...]) + b_ref[...]
    lane = lax.broadcasted_iota(jnp.int32, logits.shape, 1)
    logits = jnp.where(lane < N_EXPERTS, logits, -jnp.inf)
    m1 = jnp.max(logits, axis=-1, keepdims=True)
    i1 = jnp.min(jnp.where(logits == m1, lane, LANES), axis=-1, keepdims=True)
    rest = jnp.where(lane == i1, -jnp.inf, logits)
    m2 = jnp.max(rest, axis=-1, keepdims=True)
    i2 = jnp.min(jnp.where(rest == m2, lane, LANES), axis=-1, keepdims=True)
    e2 = jnp.exp(m2 - m1)
    w1 = 1.0 / (1.0 + e2)
    w2 = e2 / (1.0 + e2)
    wt_ref[...] = jnp.where(lane == 0, w1, jnp.where(lane == 1, w2, 0.0))
    onehot = jnp.where(jnp.logical_or(lane == i1, lane == i2), 1.0, 0.0)
    tm = onehot.shape[0]
    earlier = (lax.broadcasted_iota(jnp.int32, (tm, tm), 0) > lax.broadcasted_iota(jnp.int32, (tm, tm), 1))
    prefix = _dot(jnp.where(earlier, 1.0, 0.0).astype(BF16), onehot.astype(BF16))
    r1 = jnp.sum(jnp.where(lane == i1, prefix, 0.0), axis=-1, keepdims=True).astype(jnp.int32)
    r2 = jnp.sum(jnp.where(lane == i2, prefix, 0.0), axis=-1, keepdims=True).astype(jnp.int32)
    idx = jnp.where(lane == 0, i1, jnp.where(lane == 1, i2, jnp.where(lane == 2, r1, jnp.where(lane == 3, r2, 0))))
    idx_ref[...] = idx.T[:SUBLANES, :]
    cnt_ref[...] = jnp.broadcast_to(jnp.sum(onehot, axis=0, keepdims=True), cnt_ref.shape)


def moe_router(x, a, w_o, gain, w_hi, w_lo, bias):
    T, D = x.shape
    K = a.shape[1]
    row = lambda n: pl.BlockSpec((ROW_TILE, n), lambda i: (i, 0))
    return pl.pallas_call(
        _router_kernel,
        name="moe_router",
        grid=(T // ROW_TILE,),
        in_specs=[row(D), row(K), _resident((K, D)), _resident((1, D)), _resident((D, LANES)), _resident((D, LANES)),
                  _resident((1, LANES))],
        out_specs=[row(D), row(D), pl.BlockSpec((SUBLANES, ROW_TILE), lambda i: (i, 0)), row(LANES),
                   pl.BlockSpec((SUBLANES, LANES), lambda i: (i, 0))],
        out_shape=[
            jax.ShapeDtypeStruct((T, D), F32),
            jax.ShapeDtypeStruct((T, D), BF16),
            jax.ShapeDtypeStruct((T // ROW_TILE * SUBLANES, ROW_TILE), jnp.int32),
            jax.ShapeDtypeStruct((T, LANES), F32),
            jax.ShapeDtypeStruct((T // ROW_TILE * SUBLANES, LANES), F32),
        ],
        compiler_params=_cparams("parallel"),
    )(x, a, w_o, gain.reshape(1, D), w_hi, w_lo, bias)


def _moe_dispatch_kernel(n_ref, base_ref, h_ref, idx_ref, zeros_ref, hs_ref, dest_ref, loc_scr, sem):
    del zeros_ref
    i = pl.program_id(0)
    e0, e1, r0, r1 = (idx_ref[k:k + 1, :] for k in range(4))
    slot0, slot1, dest0, dest1 = r0, r1, r0, r1
    offs = []
    off = jnp.int32(0)
    for e in range(N_EXPERTS):
        offs.append(off)
        base = base_ref[i * N_EXPERTS + e]
        slot0 = slot0 + jnp.where(e0 == e, off, 0)
        slot1 = slot1 + jnp.where(e1 == e, off, 0)
        dest0 = dest0 + jnp.where(e0 == e, base, 0)
        dest1 = dest1 + jnp.where(e1 == e, base, 0)
        off = off + n_ref[i * N_EXPERTS + e]
    row = lax.broadcasted_iota(jnp.int32, (SUBLANES, e0.shape[1]), 0)
    dest_ref[...] = jnp.where(row == 0, dest0, jnp.where(row == 1, dest1, 0))
    slot = lax.broadcasted_iota(jnp.int32, (loc_scr.shape[0], e0.shape[1]), 0)
    perm = jnp.where(jnp.logical_or(slot == slot0, slot == slot1), 1.0, 0.0).astype(BF16)
    loc_scr[...] = _dot(perm, h_ref[...]).astype(BF16)

    def piece(src_row, dst_row):
        return pltpu.make_async_copy(loc_scr.at[pl.ds(src_row, CHUNK_ALIGN), :],
                                     hs_ref.at[pl.ds(dst_row, CHUNK_ALIGN), :], sem)

    for e in range(N_EXPERTS):
        base = base_ref[i * N_EXPERTS + e]

        def start(g, carry, e=e, base=base):
            piece(pl.multiple_of(offs[e] + g * CHUNK_ALIGN, CHUNK_ALIGN),
                  pl.multiple_of(base + g * CHUNK_ALIGN, CHUNK_ALIGN)).start()
            return carry

        lax.fori_loop(0, n_ref[i * N_EXPERTS + e] // CHUNK_ALIGN, start, 0)

    def wait(g, carry):
        piece(0, 0).wait()
        return carry

    lax.fori_loop(0, off // CHUNK_ALIGN, wait, 0)


def moe_dispatch(h, idx, chunk_rows, chunk_base, n_rows):
    T, D = h.shape
    n_rt = T // ROW_TILE
    loc_rows = TOP_K * ROW_TILE + N_EXPERTS * CHUNK_ALIGN
    grid_spec = pltpu.PrefetchScalarGridSpec(
        num_scalar_prefetch=2,
        grid=(n_rt,),
        in_specs=[
            pl.BlockSpec((ROW_TILE, D), lambda i, n, b: (i, 0)),
            pl.BlockSpec((SUBLANES, ROW_TILE), lambda i, n, b: (i, 0)),
            pl.BlockSpec(memory_space=pl.ANY),
        ],
        out_specs=[
            pl.BlockSpec(memory_space=pl.ANY),
            pl.BlockSpec((SUBLANES, ROW_TILE), lambda i, n, b: (i, 0)),
        ],
        scratch_shapes=[pltpu.VMEM((loc_rows, D), BF16), pltpu.SemaphoreType.DMA],
    )
    return pl.pallas_call(
        _moe_dispatch_kernel,
        name="moe_dispatch",
        grid_spec=grid_spec,
        out_shape=[
            jax.ShapeDtypeStruct((n_rows, D), BF16),
            jax.ShapeDtypeStruct((n_rt * SUBLANES, ROW_TILE), jnp.int32),
        ],
        input_output_aliases={4: 0},
        compiler_params=_cparams("arbitrary"),
    )(chunk_rows, chunk_base, h, idx, jnp.zeros((n_rows, D), BF16))


def _moe_ffn_kernel(te_ref, tv_ref, h_ref, wg_ref, wu_ref, wd_ref, o_ref, acc_scr):
    i = pl.program_id(0)
    f = pl.program_id(1)
    last = pl.num_programs(1) - 1

    @pl.when(f == 0)
    def _():
        acc_scr[...] = jnp.zeros_like(acc_scr)

    @pl.when(tv_ref[i] > 0)
    def _():
        _swiglu_accumulate(h_ref[...], wg_ref, wu_ref, wd_ref, acc_scr)

    @pl.when(f == last)
    def _():
        o_ref[...] = acc_scr[...].astype(o_ref.dtype)


def moe_ffn(h_sorted, tile_expert, tile_valid, wg, wu, wd, n_chunks):
    R, D = h_sorted.shape
    E, _, Fe = wg.shape
    tf = Fe // n_chunks

    def chunk(i, f, tv):
        return jnp.where(tv[i] > 0, f, n_chunks - 1)

    grid_spec = pltpu.PrefetchScalarGridSpec(
        num_scalar_prefetch=2,
        grid=(R // MOE_ROW_TILE, n_chunks),
        in_specs=[
            pl.BlockSpec((MOE_ROW_TILE, D), lambda i, f, te, tv: (i, 0)),
            pl.BlockSpec((None, D, tf), lambda i, f, te, tv: (te[i], 0, chunk(i, f, tv))),
            pl.BlockSpec((None, D, tf), lambda i, f, te, tv: (te[i], 0, chunk(i, f, tv))),
            pl.BlockSpec((None, tf, D), lambda i, f, te, tv: (te[i], chunk(i, f, tv), 0)),
        ],
        out_specs=pl.BlockSpec((MOE_ROW_TILE, D), lambda i, f, te, tv: (i, 0)),
        scratch_shapes=[pltpu.VMEM((MOE_ROW_TILE, D), F32)],
    )
    return pl.pallas_call(
        _moe_ffn_kernel,
        name="moe_ffn",
        grid_spec=grid_spec,
        out_shape=jax.ShapeDtypeStruct((R, D), BF16),
        compiler_params=_cparams("parallel", "arbitrary"),
    )(tile_expert, tile_valid, h_sorted, wg, wu, wd)


def _moe_tail_kernel(x_ref, y0_ref, y1_ref, wt_ref, p_ref, g_ref, win_ref, wgate_ref, o_ref):
    wt = wt_ref[...]
    x = x_ref[...] + wt[:, 0:1] * y0_ref[...].astype(F32) + wt[:, 1:2] * y1_ref[...].astype(F32)
    o_ref[...] = _ple(x, p_ref[...], g_ref[...], win_ref, wgate_ref)


def moe_layer_tail(x, y0, y1, wt, p, layer, gain, w_in, w_gate):
    T, D = x.shape
    P = p.shape[2]
    row = lambda n: pl.BlockSpec((ROW_TILE, n), lambda i: (i, 0))
    return pl.pallas_call(
        _moe_tail_kernel,
        name="moe_tail",
        grid=(T // ROW_TILE,),
        in_specs=[row(D), row(D), row(D), row(LANES), pl.BlockSpec((None, ROW_TILE, P), lambda i: (layer, i, 0)),
                  _resident((1, D)), _resident((P, D)), _resident((D, D))],
        out_specs=row(D),
        out_shape=jax.ShapeDtypeStruct((T, D), F32),
        compiler_params=_cparams("parallel"),
    )(x, y0, y1, wt, p, gain.reshape(1, D), w_in, w_gate)


def _rope_cos_sin(pos, dim, theta):
    inv = theta ** (-jnp.arange(0, dim, 2, dtype=F32) / dim)
    ang = pos.astype(F32)[:, None] * inv[None, :]
    return jnp.cos(ang), jnp.sin(ang)


def _fold_tables(gain_lanes, cos_lanes, sin_lanes, partner):
    return gain_lanes[None, :] * cos_lanes, gain_lanes[partner][None, :] * sin_lanes


def _axial_tables(S, q_gain, k_gain):
    pos = jnp.arange(S)
    cr, sr = _rope_cos_sin(pos // GRID_W, AX_HEAD_DIM // 2, AX_THETA)
    cc, sc = _rope_cos_sin(pos % GRID_W, AX_HEAD_DIM // 2, AX_THETA)
    cos = jnp.concatenate([cr, cr, cc, cc], axis=1)
    sin = jnp.concatenate([-sr, sr, -sc, sc], axis=1)
    lane = np.arange(LANES)
    partner = np.where(lane % 64 < 32, lane + 32, lane - 32)
    return _fold_tables(q_gain, cos, sin, partner) + _fold_tables(k_gain, cos, sin, partner)


def _mla_tables(S, q_gain, k_gain):
    c, s = _rope_cos_sin(jnp.arange(S), MLA_ROPE, MLA_THETA)
    pad = LANES - MLA_QK
    cos = jnp.concatenate([jnp.ones((S, MLA_NOPE), F32), c, c, jnp.ones((S, pad), F32)], axis=1)
    sin = jnp.concatenate([jnp.zeros((S, MLA_NOPE), F32), -s, s, jnp.zeros((S, pad), F32)], axis=1)
    lane = np.arange(LANES)
    half = MLA_ROPE // 2
    partner = np.where((lane >= MLA_NOPE) & (lane < MLA_NOPE + half), lane + half,
                       np.where((lane >= MLA_NOPE + half) & (lane < MLA_QK), lane - half, lane))
    zpad = jnp.zeros((pad,), F32)
    gq = jnp.concatenate([q_gain, zpad])
    gk = jnp.concatenate([k_gain, zpad])
    return _fold_tables(gq, cos, sin, partner) + _fold_tables(gk, cos, sin, partner)


def _moe_layout(cnt, n_rt, n_tiles):
    tm = MOE_ROW_TILE
    cnt = cnt.reshape(n_rt, SUBLANES, LANES)[:, 0, :N_EXPERTS].astype(jnp.int32)
    chunk_rows = ((cnt + CHUNK_ALIGN - 1) // CHUNK_ALIGN) * CHUNK_ALIGN
    rt = jnp.arange(n_rt)
    before = jnp.sum(jnp.where((rt[None, :] < rt[:, None])[:, :, None], chunk_rows[None, :, :], 0), axis=1)
    region = ((jnp.sum(chunk_rows, axis=0) + tm - 1) // tm) * tm
    ex = jnp.arange(N_EXPERTS)
    ends = jnp.sum(jnp.where(ex[None, :] <= ex[:, None], region[None, :], 0), axis=1)
    chunk_base = (ends - region)[None, :] + before
    tile_start = jnp.arange(n_tiles, dtype=jnp.int32) * tm
    tile_expert = jnp.minimum(jnp.sum((tile_start[:, None] >= ends[None, :]).astype(jnp.int32), axis=1), N_EXPERTS - 1)
    tile_valid = (tile_start < ends[-1]).astype(jnp.int32)
    return chunk_rows.reshape(-1), chunk_base.reshape(-1), tile_expert, tile_valid


def kernel(x, p, attn_norm, ffn_norm, ple_norm, ple_w_in, ple_w_gate, mla_w_down, mla_q_norm, mla_w_uq, mla_kv_norm, mla_w_ukv, mla_q_gain, mla_k_gain, mla_w_o, swa_w_qkv, swa_q_gain, swa_k_gain, swa_sink, swa_w_o, ax_w_qkv, ax_q_gain, ax_k_gain, ax_w_o, ffn_w_gate, ffn_w_up, ffn_w_down, moe_w_router, moe_b_router, moe_w_gate, moe_w_up, moe_w_down):
    B, S, D = x.shape
    depth = p.shape[0]
    T = B * S
    xt = x.reshape(T, D)
    bf = lambda a: a.astype(BF16)
    n_rt = T // ROW_TILE
    n_moe_tiles = (TOP_K * T + n_rt * N_EXPERTS * (CHUNK_ALIGN - 1)) // MOE_ROW_TILE + N_EXPERTS
    slopes = jnp.asarray(2.0 ** (-8.0 * np.arange(1, SWA_HEADS + 1) / SWA_HEADS) * LOG2E, dtype=F32)
    p3 = p.reshape(depth, T, -1)

    for i in range(depth):
        kind = i % N_MIXERS
        j = i // N_MIXERS
        if kind == 0:
            wd = mla_w_down[j]
            zc = lambda n: jnp.zeros((D, n), F32)
            wd = jnp.concatenate([wd[:, :MLA_Q_RANK + MLA_KV_RANK], zc(MLA_NOPE), wd[:, MLA_Q_RANK + MLA_KV_RANK:],
                                  zc(LANES - MLA_QK)], axis=1)
            wuq = mla_w_uq[j].reshape(MLA_Q_RANK, MLA_HEADS, MLA_QK)
            wuq = jnp.pad(wuq, ((0, 0), (0, 0), (0, LANES - MLA_QK))).reshape(MLA_Q_RANK, MLA_HEADS * LANES)
            q, kv, kr = mla_proj(xt, attn_norm[i], bf(wd), mla_q_norm[j], mla_kv_norm[j], bf(wuq), bf(mla_w_ukv[j]))
            o = mla_attention(q, kv, kr, _mla_tables(S, mla_q_gain[j], mla_k_gain[j]), B, S)
            w_o = mla_w_o[j]
        elif kind == 1:
            qkv = norm_proj(xt, attn_norm[i], bf(swa_w_qkv[j]))
            gq = jnp.tile(swa_q_gain[j], 2).reshape(1, LANES)
            gk = jnp.tile(swa_k_gain[j], 2).reshape(1, LANES)
            o = swa_attention(qkv, slopes, swa_sink[j].astype(F32) * LOG2E, gq, gk, B, S)
            w_o = swa_w_o[j]
        else:
            qkv = norm_proj(xt, attn_norm[i], bf(ax_w_qkv[j]))
            o = axial_attention(qkv, _axial_tables(S, ax_q_gain[j], ax_k_gain[j]), B, S)
            w_o = ax_w_o[j]
        f = i // 2
        if i % 2 == 0:
            xt = dense_layer_tail(xt, o, bf(w_o), ffn_norm[i], bf(ffn_w_gate[f]), bf(ffn_w_up[f]), bf(ffn_w_down[f]),
                                  p3, i, ple_norm[i], bf(ple_w_in[i]), bf(ple_w_gate[i]))
        else:
            wr = jnp.pad(moe_w_router[f], ((0, 0), (0, LANES - N_EXPERTS)))
            wr_hi = bf(wr)
            wr_lo = bf(wr - wr_hi.astype(F32))
            br = jnp.pad(moe_b_router[f].astype(F32), (0, LANES - N_EXPERTS)).reshape(1, LANES)
            xt, h, idx, wt, cnt = moe_router(xt, o, bf(w_o), ffn_norm[i], wr_hi, wr_lo, br)
            chunk_rows, chunk_base, tile_expert, tile_valid = _moe_layout(cnt, n_rt, n_moe_tiles)
            h_sorted, dest = moe_dispatch(h, idx, chunk_rows, chunk_base, n_moe_tiles * MOE_ROW_TILE)
            y = moe_ffn(h_sorted, tile_expert, tile_valid, bf(moe_w_gate[f]), bf(moe_w_up[f]), bf(moe_w_down[f]), 2)
            dest = dest.reshape(n_rt, SUBLANES, ROW_TILE)
            y0 = jnp.take(y, dest[:, 0, :].reshape(T), axis=0)
            y1 = jnp.take(y, dest[:, 1, :].reshape(T), axis=0)
            xt = moe_layer_tail(xt, y0, y1, wt, p3, i, ple_norm[i], bf(ple_w_in[i]), bf(ple_w_gate[i]))
    return xt.reshape(B, S, D)
```

```python
import functools

import numpy as np
import jax
import jax.numpy as jnp
from jax import lax
from jax.experimental import pallas as pl
from jax.experimental.pallas import tpu as pltpu

F32 = jnp.float32
BF16 = jnp.bfloat16

EPS = 1e-6
GRID_W = 64
BLOCK_Q = 128

MLA_HEADS = 16
MLA_NOPE = 64
MLA_ROPE = 32
MLA_V = 64
MLA_Q_RANK = 256
MLA_KV_RANK = 128
MLA_THETA = 10000.0
MLA_QK = MLA_NOPE + MLA_ROPE

SWA_HEADS = 16
SWA_KV_HEADS = 4
SWA_HEAD_DIM = 64
SWA_WINDOW = 128

AX_HEADS = 8
AX_KV_HEADS = 4
AX_HEAD_DIM = 128
AX_THETA = 10000.0

N_EXPERTS = 8
TOP_K = 2
N_MIXERS = 3

LANES = 128
SUBLANES = 8
ROW_TILE = 512
MOE_ROW_TILE = 512
CHUNK_ALIGN = 2 * SUBLANES
ATTN_Q_TILE = 256
FFN_SUB = 256
VMEM_LIMIT = 56 * 1024 * 1024
LOG2E = 1.4426950408889634


def _cparams(*sem, flags=None):
    return pltpu.CompilerParams(dimension_semantics=sem, vmem_limit_bytes=VMEM_LIMIT, flags=flags)


ATTN_FLAGS = None


def _rms(xf, gain):
    ms = jnp.mean(xf * xf, axis=-1, keepdims=True)
    return xf * lax.rsqrt(ms + EPS) * gain


def _dot(a, b):
    return jnp.dot(a, b, preferred_element_type=F32)


def _dot_nt(a, b):
    return lax.dot_general(a, b, (((1,), (1,)), ((), ())), preferred_element_type=F32)


def _resident(shape):
    return pl.BlockSpec(shape, lambda *_: (0,) * len(shape), pipeline_mode=pl.Buffered(1))


def _norm_proj_kernel(x_ref, g_ref, w_ref, o_ref):
    h = _rms(x_ref[...], g_ref[...]).astype(BF16)
    o_ref[...] = _dot(h, w_ref[...]).astype(o_ref.dtype)


def norm_proj(x, gain, w):
    T, D = x.shape
    N = w.shape[1]
    return pl.pallas_call(
        _norm_proj_kernel,
        name="norm_proj",
        grid=(T // ROW_TILE,),
        in_specs=[
            pl.BlockSpec((ROW_TILE, D), lambda i: (i, 0)),
            _resident((1, D)),
            _resident((D, N)),
        ],
        out_specs=pl.BlockSpec((ROW_TILE, N), lambda i: (i, 0)),
        out_shape=jax.ShapeDtypeStruct((T, N), BF16),
        compiler_params=_cparams("parallel"),
    )(x, gain.reshape(1, D), w)


def _mla_proj_kernel(x_ref, g_ref, wd_ref, qn_ref, kvn_ref, wuq_ref, wukv_ref, q_ref, kv_ref, kr_ref):
    h = _rms(x_ref[...], g_ref[...]).astype(BF16)
    down = _dot(h, wd_ref[...])
    cq = _rms(down[:, :MLA_Q_RANK], qn_ref[...]).astype(BF16)
    ckv = _rms(down[:, MLA_Q_RANK:MLA_Q_RANK + MLA_KV_RANK], kvn_ref[...]).astype(BF16)
    q_ref[...] = _dot(cq, wuq_ref[...]).astype(BF16)
    kv_ref[...] = _dot(ckv, wukv_ref[...]).astype(BF16)
    kr_ref[...] = down[:, MLA_Q_RANK + MLA_KV_RANK:].astype(BF16)


def mla_proj(x, gain, wd, qn, kvn, wuq, wukv):
    T, D = x.shape
    nd = wd.shape[1]
    nq = wuq.shape[1]
    nkv = wukv.shape[1]
    row = lambda n: pl.BlockSpec((ROW_TILE, n), lambda i: (i, 0))
    return pl.pallas_call(
        _mla_proj_kernel,
        name="mla_proj",
        grid=(T // ROW_TILE,),
        in_specs=[
            row(D),
            _resident((1, D)),
            _resident((D, nd)),
            _resident((1, MLA_Q_RANK)),
            _resident((1, MLA_KV_RANK)),
            _resident((MLA_Q_RANK, nq)),
            _resident((MLA_KV_RANK, nkv)),
        ],
        out_specs=[row(nq), row(nkv), row(LANES)],
        out_shape=[
            jax.ShapeDtypeStruct((T, nq), BF16),
            jax.ShapeDtypeStruct((T, nkv), BF16),
            jax.ShapeDtypeStruct((T, LANES), BF16),
        ],
        compiler_params=_cparams("parallel"),
    )(x, gain.reshape(1, D), wd, qn.reshape(1, -1), kvn.reshape(1, -1), wuq, wukv)


def _two_unit_pipeline(n, scores, finish, emit):
    def step(i, prefetch):
        scores(i, 1)
        o0 = finish(i, 0)
        if prefetch:
            scores(i + 1, 0)
        o1 = finish(i, 1)
        emit(i, o0, o1)

    scores(0, 0)
    if n > 1:
        def body(i, carry):
            step(i, True)
            return carry

        lax.fori_loop(0, n - 1, body, 0)
    step(n - 1, False)


def _store_scores(s, s_buf, m_buf=None):
    s_buf[...] = s
    if m_buf is not None:
        m_buf[...] = jnp.max(s, axis=-1, keepdims=True)


def _softmax_numerators(s_buf, m_buf=None, extra_logit=None):
    m = m_buf[...] if m_buf is not None else jnp.max(s_buf[...], axis=-1, keepdims=True)
    if extra_logit is not None:
        m = jnp.maximum(m, extra_logit)
    return jnp.exp2(s_buf[...] - m).astype(BF16), m


def _norm_rope(x, perm_ref, a, b, inv_dim, extra):
    xf = x.astype(F32)
    c = lax.rsqrt(jnp.sum(xf * xf, axis=-1, keepdims=True) * inv_dim + EPS) * extra
    return (xf * a + _dot(x, perm_ref[...]) * b) * c


def _axial_attn_kernel(q_ref, k_ref, v_ref, perm_ref, aq_ref, bq_ref, ak_ref, bk_ref, o_ref, q_scr, k_scr, v_scr,
                       s0_scr, s1_scr, *, tq, scale):
    S = k_ref.shape[0]
    s_bufs = (s0_scr, s1_scr)
    inv_dim = 1.0 / AX_HEAD_DIM

    k_scr[...] = _norm_rope(k_ref[...], perm_ref, ak_ref[...], bk_ref[...], inv_dim, 1.0).astype(BF16)
    for u in range(2):
        q_scr[u] = _norm_rope(q_ref[:, u * LANES:(u + 1) * LANES], perm_ref, aq_ref[...], bq_ref[...], inv_dim,
                              scale * LOG2E).astype(BF16)
    v_scr[:, :LANES] = v_ref[...]
    v_scr[:, LANES:] = jnp.ones((S, LANES), BF16)

    def rows(i):
        return pl.ds(pl.multiple_of(i * tq, tq), tq)

    def scores(i, u):
        _store_scores(_dot_nt(q_scr[u, rows(i), :], k_scr[...]), s_bufs[u])

    def finish(i, u):
        p, _ = _softmax_numerators(s_bufs[u])
        o = _dot(p, v_scr[...])
        return o[:, :LANES] / o[:, LANES:]

    def emit(i, o0, o1):
        o_ref[rows(i), :LANES] = o0.astype(o_ref.dtype)
        o_ref[rows(i), LANES:] = o1.astype(o_ref.dtype)

    _two_unit_pipeline(S // tq, scores, finish, emit)


def axial_attention(qkv, tabs, B, S):
    R = AX_HEADS // AX_KV_HEADS
    assert R == 2
    tq = min(ATTN_Q_TILE, S)
    kern = functools.partial(_axial_attn_kernel, tq=tq, scale=AX_HEAD_DIM ** -0.5)
    tab = pl.BlockSpec((S, LANES), lambda b, g: (0, 0), pipeline_mode=pl.Buffered(1))
    return pl.pallas_call(
        kern,
        name="axial_attn",
        grid=(B, AX_KV_HEADS),
        in_specs=[
            pl.BlockSpec((S, R * LANES), lambda b, g: (b, g)),
            pl.BlockSpec((S, LANES), lambda b, g: (b, AX_HEADS + g)),
            pl.BlockSpec((S, LANES), lambda b, g: (b, AX_HEADS + AX_KV_HEADS + g)),
            _resident((LANES, LANES)), tab, tab, tab, tab,
        ],
        out_specs=pl.BlockSpec((S, R * LANES), lambda b, g: (b, g)),
        out_shape=jax.ShapeDtypeStruct((B * S, AX_HEADS * AX_HEAD_DIM), BF16),
        scratch_shapes=[
            pltpu.VMEM((R, S, LANES), BF16),
            pltpu.VMEM((S, LANES), BF16),
            pltpu.VMEM((S, 2 * LANES), BF16),
            pltpu.VMEM((tq, S), F32),
            pltpu.VMEM((tq, S), F32),
        ],
        compiler_params=_cparams("parallel", "parallel", flags=ATTN_FLAGS),
    )(qkv, qkv, qkv, *tabs)


def _mla_attn_kernel(q_ref, kv_ref, kr_ref, perm_ref, aq_ref, bq_ref, ak_ref, bk_ref, o_ref, q_scr, k_scr, v_scr,
                     s0_scr, s1_scr, m0_scr, m1_scr, *, tq, scale):
    S = kv_ref.shape[0]
    s_bufs = (s0_scr, s1_scr)
    m_bufs = (m0_scr, m1_scr)
    lane = lax.broadcasted_iota(jnp.int32, (1, LANES), 1)
    lo = lane < MLA_NOPE
    inv_dim = 1.0 / MLA_QK

    kr = kr_ref[...]
    krf = kr.astype(F32)
    rope_part = krf * ak_ref[...] + _dot(kr, perm_ref[...]) * bk_ref[...]
    ss_rope = jnp.sum(krf * krf, axis=-1, keepdims=True)
    for hh in range(2):
        kvh = kv_ref[:, hh * LANES:(hh + 1) * LANES].astype(F32)
        ss = jnp.sum(jnp.where(lo, kvh * kvh, 0.0), axis=-1, keepdims=True) + ss_rope
        k = jnp.where(lo, kvh * ak_ref[...], rope_part)
        k_scr[hh] = (k * lax.rsqrt(ss * inv_dim + EPS)).astype(BF16)
        vh = jnp.where(lo, pltpu.roll(kvh, MLA_V, 1), 1.0) if hh == 0 else jnp.where(lo, 1.0, kvh)
        v_scr[hh] = vh.astype(BF16)
        q_scr[hh] = _norm_rope(q_ref[:, hh * LANES:(hh + 1) * LANES], perm_ref, aq_ref[...], bq_ref[...], inv_dim,
                               scale * LOG2E).astype(BF16)

    def rows(i):
        return pl.ds(pl.multiple_of(i * tq, tq), tq)

    def scores(i, u):
        _store_scores(_dot_nt(q_scr[u, rows(i), :], k_scr[u]), s_bufs[u], m_bufs[u])

    def finish(i, u):
        p, _ = _softmax_numerators(s_bufs[u], m_bufs[u])
        o = _dot(p, v_scr[u])
        return o / pltpu.roll(o, MLA_V, 1)

    def emit(i, o0, o1):
        o_ref[rows(i), :] = jnp.where(lo, o0, o1).astype(o_ref.dtype)

    _two_unit_pipeline(S // tq, scores, finish, emit)


def mla_attention(q, kv, kr, tabs, B, S):
    tq = min(ATTN_Q_TILE, S)
    kern = functools.partial(_mla_attn_kernel, tq=tq, scale=MLA_QK ** -0.5)
    tab = pl.BlockSpec((S, LANES), lambda b, g: (0, 0), pipeline_mode=pl.Buffered(1))
    return pl.pallas_call(
        kern,
        name="mla_attn",
        grid=(B, MLA_HEADS // 2),
        in_specs=[
            pl.BlockSpec((S, 2 * LANES), lambda b, g: (b, g)),
            pl.BlockSpec((S, 2 * LANES), lambda b, g: (b, g)),
            pl.BlockSpec((S, LANES), lambda b, g: (b, 0)),
            _resident((LANES, LANES)), tab, tab, tab, tab,
        ],
        out_specs=pl.BlockSpec((S, LANES), lambda b, g: (b, g)),
        out_shape=jax.ShapeDtypeStruct((B * S, MLA_HEADS * MLA_V), BF16),
        scratch_shapes=[
            pltpu.VMEM((2, S, LANES), BF16),
            pltpu.VMEM((2, S, LANES), BF16),
            pltpu.VMEM((2, S, LANES), BF16),
            pltpu.VMEM((tq, S), F32),
            pltpu.VMEM((tq, S), F32),
            pltpu.VMEM((tq, 1), F32),
            pltpu.VMEM((tq, 1), F32),
        ],
        compiler_params=_cparams("parallel", "parallel", flags=ATTN_FLAGS),
    )(q, kv, kr, *tabs)


def _swa_attn_kernel(slope_ref, sink_ref, q_ref, k_ref, v_ref, half_ref, swap_ref, gq_ref, gk_ref, o_ref,
                     q_scr, k_scr, v_scr, s0_scr, s1_scr, *, scale):
    S = k_ref.shape[0]
    span = BLOCK_Q + 2 * SWA_WINDOW
    R = SWA_HEADS // SWA_KV_HEADS
    rows_u = R * BLOCK_Q
    s_bufs = (s0_scr, s1_scr)
    pid = pl.program_id(1)
    lane = lax.broadcasted_iota(jnp.int32, (1, LANES), 1)
    lo = lane < SWA_HEAD_DIM
    hi = jnp.logical_not(lo)

    def seg_norm(x, gain):
        xf = x.astype(F32)
        sq = xf * xf
        sq_hi = sq.astype(BF16)
        sq_lo = (sq - sq_hi.astype(F32)).astype(BF16)
        ss = _dot(sq_hi, half_ref[...]) + _dot(sq_lo, half_ref[...])
        return xf * lax.rsqrt(ss * (1.0 / SWA_HEAD_DIM) + EPS) * gain

    k_scr[...] = seg_norm(k_ref[...], gk_ref[...]).astype(BF16)
    v = v_ref[...].astype(F32)
    v_scr[0] = jnp.where(lo, v, 1.0).astype(BF16)
    v_scr[1] = jnp.where(lo, 1.0, v).astype(BF16)
    for pb in range(R):
        e = pb // (R // 2)
        keep = lo if e == 0 else hi
        qp = seg_norm(q_ref[:, pb * LANES:(pb + 1) * LANES], gq_ref[...]) * (scale * LOG2E)
        qr = _dot(qp.astype(BF16), swap_ref[...])
        for i in range(2):
            qz = jnp.where(keep, qp if i == e else qr, 0.0).astype(BF16)
            r = (pb % (R // 2)) * 2 + i
            for j in range(S // BLOCK_Q):
                q_scr[e, j, r * BLOCK_Q:(r + 1) * BLOCK_Q, :] = qz[j * BLOCK_Q:(j + 1) * BLOCK_Q, :]

    head_of_row = lax.broadcasted_iota(jnp.int32, (rows_u, 1), 0) // BLOCK_Q

    def head_column(ref, e):
        col = jnp.zeros((rows_u, 1), F32)
        for r in range(R):
            col = jnp.where(head_of_row == r, ref[pid * 2 * R + e * R + r], col)
        return col

    slope_cols = [head_column(slope_ref, e) for e in range(2)]
    sink_cols = [head_column(sink_ref, e) for e in range(2)]
    rel = (lax.broadcasted_iota(jnp.int32, (rows_u, span), 0) % BLOCK_Q
           - lax.broadcasted_iota(jnp.int32, (rows_u, span), 1))

    def rows(j):
        return pl.ds(pl.multiple_of(j * BLOCK_Q, BLOCK_Q), BLOCK_Q)

    def window(j):
        start = jnp.clip(j * BLOCK_Q - SWA_WINDOW, 0, S - span)
        return pl.multiple_of(start, BLOCK_Q)

    def scores(j, e):
        start = window(j)
        s = _dot_nt(q_scr[e, j], k_scr[pl.ds(start, span), :])
        dist = jnp.abs(rel + (j * BLOCK_Q - start))
        s = jnp.where(dist <= SWA_WINDOW, s - slope_cols[e] * dist.astype(F32), -jnp.inf)
        _store_scores(s, s_bufs[e])

    def finish(j, e):
        p, m = _softmax_numerators(s_bufs[e], extra_logit=sink_cols[e])
        o = _dot(p, v_scr[e, pl.ds(window(j), span), :])
        den = pltpu.roll(o, SWA_HEAD_DIM, 1) + jnp.exp2(sink_cols[e] - m)
        return o / den

    def emit(j, o0, o1):
        for e, o in ((0, o0), (1, o1)):
            orot = pltpu.roll(o, SWA_HEAD_DIM, 1)
            for k in range(R // 2):
                pb = e * (R // 2) + k
                even = (o if e == 0 else orot)[2 * k * BLOCK_Q:(2 * k + 1) * BLOCK_Q]
                odd = (o if e == 1 else orot)[(2 * k + 1) * BLOCK_Q:(2 * k + 2) * BLOCK_Q]
                o_ref[rows(j), pb * LANES:(pb + 1) * LANES] = jnp.where(lo, even, odd).astype(o_ref.dtype)

    _two_unit_pipeline(S // BLOCK_Q, scores, finish, emit)


def swa_attention(qkv, slopes, sink, gq, gk, B, S):
    n_steps = SWA_KV_HEADS // 2
    R = SWA_HEADS // SWA_KV_HEADS
    span = BLOCK_Q + 2 * SWA_WINDOW
    qw = SWA_HEADS * SWA_HEAD_DIM // n_steps
    kbase = SWA_HEADS * SWA_HEAD_DIM // LANES
    smem = pl.BlockSpec(memory_space=pltpu.SMEM)
    gain = pl.BlockSpec((1, LANES), lambda b, g: (0, 0))
    kern = functools.partial(_swa_attn_kernel, scale=SWA_HEAD_DIM ** -0.5)
    lane = np.arange(LANES)
    half_ones = jnp.asarray(lane[:, None] // SWA_HEAD_DIM == lane[None, :] // SWA_HEAD_DIM, dtype=BF16)
    swap = _partner_matrix((lane + SWA_HEAD_DIM) % LANES)
    return pl.pallas_call(
        kern,
        name="swa_attn",
        grid=(B, n_steps),
        in_specs=[
            smem, smem,
            pl.BlockSpec((S, qw), lambda b, g: (b, g)),
            pl.BlockSpec((S, LANES), lambda b, g: (b, kbase + g)),
            pl.BlockSpec((S, LANES), lambda b, g: (b, kbase + n_steps + g)),
            _resident((LANES, LANES)), _resident((LANES, LANES)),
            gain, gain,
        ],
        out_specs=pl.BlockSpec((S, qw), lambda b, g: (b, g)),
        out_shape=jax.ShapeDtypeStruct((B * S, SWA_HEADS * SWA_HEAD_DIM), BF16),
        scratch_shapes=[
            pltpu.VMEM((2, S // BLOCK_Q, R * BLOCK_Q, LANES), BF16),
            pltpu.VMEM((S, LANES), BF16),
            pltpu.VMEM((2, S, LANES), BF16),
            pltpu.VMEM((R * BLOCK_Q, span), F32),
            pltpu.VMEM((R * BLOCK_Q, span), F32),
        ],
        compiler_params=_cparams("parallel", "parallel", flags=ATTN_FLAGS),
    )(slopes, sink, qkv, qkv, qkv, half_ones, swap, gq, gk)


def _swiglu_accumulate(h, wg_ref, wu_ref, wd_ref, acc_ref):
    for c in range(wg_ref.shape[1] // FFN_SUB):
        sl = slice(c * FFN_SUB, (c + 1) * FFN_SUB)
        g = _dot(h, wg_ref[:, sl])
        u = _dot(h, wu_ref[:, sl])
        a = (g * jax.nn.sigmoid(g) * u).astype(BF16)
        acc_ref[...] += _dot(a, wd_ref[sl, :])


def _ple(x, p, gain, win_ref, wgate_ref):
    gate = jax.nn.sigmoid(_dot(_rms(x, gain).astype(BF16), wgate_ref[...]))
    return x + _dot(p.astype(BF16), win_ref[...]) * gate


def _dense_tail_kernel(x_ref, a_ref, wo_ref, g_ref, wg_ref, wu_ref, wd_ref, p_ref, gp_ref, win_ref, wgate_ref, o_ref,
                       acc_scr):
    x = x_ref[...] + _dot(a_ref[...], wo_ref[...])
    acc_scr[...] = x
    _swiglu_accumulate(_rms(x, g_ref[...]).astype(BF16), wg_ref, wu_ref, wd_ref, acc_scr)
    o_ref[...] = _ple(acc_scr[...], p_ref[...], gp_ref[...], win_ref, wgate_ref)


def dense_layer_tail(x, a, w_o, gain, wg, wu, wd, p, layer, ple_gain, w_in, w_gate):
    T, D = x.shape
    K = a.shape[1]
    Fd = wg.shape[1]
    P = p.shape[2]
    row = lambda n: pl.BlockSpec((ROW_TILE, n), lambda i: (i, 0))
    return pl.pallas_call(
        _dense_tail_kernel,
        name="dense_tail",
        grid=(T // ROW_TILE,),
        in_specs=[
            row(D), row(K), _resident((K, D)),
            _resident((1, D)), _resident((D, Fd)), _resident((D, Fd)), _resident((Fd, D)),
            pl.BlockSpec((None, ROW_TILE, P), lambda i: (layer, i, 0)),
            _resident((1, D)), _resident((P, D)), _resident((D, D)),
        ],
        out_specs=row(D),
        out_shape=jax.ShapeDtypeStruct((T, D), F32),
        scratch_shapes=[pltpu.VMEM((ROW_TILE, D), F32)],
        compiler_params=_cparams("parallel"),
    )(x, a, w_o, gain.reshape(1, D), wg, wu, wd, p, ple_gain.reshape(1, D), w_in, w_gate)


def _router_kernel(x_ref, a_ref, wo_ref, g_ref, whi_ref, wlo_ref, b_ref, x1_ref, h_ref, idx_ref, wt_ref, cnt_ref):
    x1 = x_ref[...] + _dot(a_ref[...], wo_ref[...])
    x1_ref[...] = x1
    hf = _rms(x1, g_ref[...])
    h_hi = hf.astype(BF16)
    h_lo = (hf - h_hi.astype(F32)).astype(BF16)
    h_ref[...] = h_hi
    logits = _dot(h_hi, whi_ref[...]) + _dot(h_hi, wlo_ref[...]) + _dot(h_lo, whi_ref[...]) + b_ref[...]
    lane = lax.broadcasted_iota(jnp.int32, logits.shape, 1)
    logits = jnp.where(lane < N_EXPERTS, logits, -jnp.inf)
    m1 = jnp.max(logits, axis=-1, keepdims=True)
    i1 = jnp.min(jnp.where(logits == m1, lane, LANES), axis=-1, keepdims=True)
    rest = jnp.where(lane == i1, -jnp.inf, logits)
    m2 = jnp.max(rest, axis=-1, keepdims=True)
    i2 = jnp.min(jnp.where(rest == m2, lane, LANES), axis=-1, keepdims=True)
    e2 = jnp.exp(m2 - m1)
    w1 = 1.0 / (1.0 + e2)
    w2 = e2 / (1.0 + e2)
    wt_ref[...] = jnp.where(lane == 0, w1, jnp.where(lane == 1, w2, 0.0))
    onehot = jnp.where(jnp.logical_or(lane == i1, lane == i2), 1.0, 0.0)
    tm = onehot.shape[0]
    earlier = (lax.broadcasted_iota(jnp.int32, (tm, tm), 0) > lax.broadcasted_iota(jnp.int32, (tm, tm), 1))
    prefix = _dot(jnp.where(earlier, 1.0, 0.0).astype(BF16), onehot.astype(BF16))
    r1 = jnp.sum(jnp.where(lane == i1, prefix, 0.0), axis=-1, keepdims=True).astype(jnp.int32)
    r2 = jnp.sum(jnp.where(lane == i2, prefix, 0.0), axis=-1, keepdims=True).astype(jnp.int32)
    idx = jnp.where(lane == 0, i1, jnp.where(lane == 1, i2, jnp.where(lane == 2, r1, jnp.where(lane == 3, r2, 0))))
    idx_ref[...] = idx.T[:SUBLANES, :]
    cnt_ref[...] = jnp.broadcast_to(jnp.sum(onehot, axis=0, keepdims=True), cnt_ref.shape)


def moe_router(x, a, w_o, gain, w_hi, w_lo, bias):
    T, D = x.shape
    K = a.shape[1]
    row = lambda n: pl.BlockSpec((ROW_TILE, n), lambda i: (i, 0))
    return pl.pallas_call(
        _router_kernel,
        name="moe_router",
        grid=(T // ROW_TILE,),
        in_specs=[row(D), row(K), _resident((K, D)), _resident((1, D)), _resident((D, LANES)), _resident((D, LANES)),
                  _resident((1, LANES))],
        out_specs=[row(D), row(D), pl.BlockSpec((SUBLANES, ROW_TILE), lambda i: (i, 0)), row(LANES),
                   pl.BlockSpec((SUBLANES, LANES), lambda i: (i, 0))],
        out_shape=[
            jax.ShapeDtypeStruct((T, D), F32),
            jax.ShapeDtypeStruct((T, D), BF16),
            jax.ShapeDtypeStruct((T // ROW_TILE * SUBLANES, ROW_TILE), jnp.int32),
            jax.ShapeDtypeStruct((T, LANES), F32),
            jax.ShapeDtypeStruct((T // ROW_TILE * SUBLANES, LANES), F32),
        ],
        compiler_params=_cparams("parallel"),
    )(x, a, w_o, gain.reshape(1, D), w_hi, w_lo, bias)


def _moe_dispatch_kernel(n_ref, base_ref, h_ref, idx_ref, zeros_ref, hs_ref, dest_ref, loc_scr, sem):
    del zeros_ref
    i = pl.program_id(0)
    e0, e1, r0, r1 = (idx_ref[k:k + 1, :] for k in range(4))
    slot0, slot1, dest0, dest1 = r0, r1, r0, r1
    offs = []
    off = jnp.int32(0)
    for e in range(N_EXPERTS):
        offs.append(off)
        base = base_ref[i * N_EXPERTS + e]
        slot0 = slot0 + jnp.where(e0 == e, off, 0)
        slot1 = slot1 + jnp.where(e1 == e, off, 0)
        dest0 = dest0 + jnp.where(e0 == e, base, 0)
        dest1 = dest1 + jnp.where(e1 == e, base, 0)
        off = off + n_ref[i * N_EXPERTS + e]
    row = lax.broadcasted_iota(jnp.int32, (SUBLANES, e0.shape[1]), 0)
    dest_ref[...] = jnp.where(row == 0, dest0, jnp.where(row == 1, dest1, 0))
    slot = lax.broadcasted_iota(jnp.int32, (loc_scr.shape[0], e0.shape[1]), 0)
    perm = jnp.where(jnp.logical_or(slot == slot0, slot == slot1), 1.0, 0.0).astype(BF16)
    loc_scr[...] = _dot(perm, h_ref[...]).astype(BF16)

    def piece(src_row, dst_row):
        return pltpu.make_async_copy(loc_scr.at[pl.ds(src_row, CHUNK_ALIGN), :],
                                     hs_ref.at[pl.ds(dst_row, CHUNK_ALIGN), :], sem)

    for e in range(N_EXPERTS):
        base = base_ref[i * N_EXPERTS + e]

        def start(g, carry, e=e, base=base):
            piece(pl.multiple_of(offs[e] + g * CHUNK_ALIGN, CHUNK_ALIGN),
                  pl.multiple_of(base + g * CHUNK_ALIGN, CHUNK_ALIGN)).start()
            return carry

        lax.fori_loop(0, n_ref[i * N_EXPERTS + e] // CHUNK_ALIGN, start, 0)

    def wait(g, carry):
        piece(0, 0).wait()
        return carry

    lax.fori_loop(0, off // CHUNK_ALIGN, wait, 0)


def moe_dispatch(h, idx, chunk_rows, chunk_base, n_rows):
    T, D = h.shape
    n_rt = T // ROW_TILE
    loc_rows = TOP_K * ROW_TILE + N_EXPERTS * CHUNK_ALIGN
    grid_spec = pltpu.PrefetchScalarGridSpec(
        num_scalar_prefetch=2,
        grid=(n_rt,),
        in_specs=[
            pl.BlockSpec((ROW_TILE, D), lambda i, n, b: (i, 0)),
            pl.BlockSpec((SUBLANES, ROW_TILE), lambda i, n, b: (i, 0)),
            pl.BlockSpec(memory_space=pl.ANY),
        ],
        out_specs=[
            pl.BlockSpec(memory_space=pl.ANY),
            pl.BlockSpec((SUBLANES, ROW_TILE), lambda i, n, b: (i, 0)),
        ],
        scratch_shapes=[pltpu.VMEM((loc_rows, D), BF16), pltpu.SemaphoreType.DMA],
    )
    return pl.pallas_call(
        _moe_dispatch_kernel,
        name="moe_dispatch",
        grid_spec=grid_spec,
        out_shape=[
            jax.ShapeDtypeStruct((n_rows, D), BF16),
            jax.ShapeDtypeStruct((n_rt * SUBLANES, ROW_TILE), jnp.int32),
        ],
        input_output_aliases={4: 0},
        compiler_params=_cparams("arbitrary"),
    )(chunk_rows, chunk_base, h, idx, jnp.zeros((n_rows, D), BF16))


def _moe_ffn_kernel(te_ref, tv_ref, h_ref, wg_ref, wu_ref, wd_ref, o_ref, acc_scr):
    i = pl.program_id(0)
    f = pl.program_id(1)
    last = pl.num_programs(1) - 1

    @pl.when(f == 0)
    def _():
        acc_scr[...] = jnp.zeros_like(acc_scr)

    @pl.when(tv_ref[i] > 0)
    def _():
        _swiglu_accumulate(h_ref[...], wg_ref, wu_ref, wd_ref, acc_scr)

    @pl.when(f == last)
    def _():
        o_ref[...] = acc_scr[...].astype(o_ref.dtype)


def moe_ffn(h_sorted, tile_expert, tile_valid, wg, wu, wd, n_chunks):
    R, D = h_sorted.shape
    E, _, Fe = wg.shape
    tf = Fe // n_chunks

    def chunk(i, f, tv):
        return jnp.where(tv[i] > 0, f, n_chunks - 1)

    grid_spec = pltpu.PrefetchScalarGridSpec(
        num_scalar_prefetch=2,
        grid=(R // MOE_ROW_TILE, n_chunks),
        in_specs=[
            pl.BlockSpec((MOE_ROW_TILE, D), lambda i, f, te, tv: (i, 0)),
            pl.BlockSpec((None, D, tf), lambda i, f, te, tv: (te[i], 0, chunk(i, f, tv))),
            pl.BlockSpec((None, D, tf), lambda i, f, te, tv: (te[i], 0, chunk(i, f, tv))),
            pl.BlockSpec((None, tf, D), lambda i, f, te, tv: (te[i], chunk(i, f, tv), 0)),
        ],
        out_specs=pl.BlockSpec((MOE_ROW_TILE, D), lambda i, f, te, tv: (i, 0)),
        scratch_shapes=[pltpu.VMEM((MOE_ROW_TILE, D), F32)],
    )
    return pl.pallas_call(
        _moe_ffn_kernel,
        name="moe_ffn",
        grid_spec=grid_spec,
        out_shape=jax.ShapeDtypeStruct((R, D), BF16),
        compiler_params=_cparams("parallel", "arbitrary"),
    )(tile_expert, tile_valid, h_sorted, wg, wu, wd)


def _moe_tail_kernel(x_ref, y0_ref, y1_ref, wt_ref, p_ref, g_ref, win_ref, wgate_ref, o_ref):
    wt = wt_ref[...]
    x = x_ref[...] + wt[:, 0:1] * y0_ref[...].astype(F32) + wt[:, 1:2] * y1_ref[...].astype(F32)
    o_ref[...] = _ple(x, p_ref[...], g_ref[...], win_ref, wgate_ref)


def moe_layer_tail(x, y0, y1, wt, p, layer, gain, w_in, w_gate):
    T, D = x.shape
    P = p.shape[2]
    row = lambda n: pl.BlockSpec((ROW_TILE, n), lambda i: (i, 0))
    return pl.pallas_call(
        _moe_tail_kernel,
        name="moe_tail",
        grid=(T // ROW_TILE,),
        in_specs=[row(D), row(D), row(D), row(LANES), pl.BlockSpec((None, ROW_TILE, P), lambda i: (layer, i, 0)),
                  _resident((1, D)), _resident((P, D)), _resident((D, D))],
        out_specs=row(D),
        out_shape=jax.ShapeDtypeStruct((T, D), F32),
        compiler_params=_cparams("parallel"),
    )(x, y0, y1, wt, p, gain.reshape(1, D), w_in, w_gate)


def _rope_cos_sin(pos, dim, theta):
    inv = theta ** (-jnp.arange(0, dim, 2, dtype=F32) / dim)
    ang = pos.astype(F32)[:, None] * inv[None, :]
    return jnp.cos(ang), jnp.sin(ang)


def _fold_tables(gain_lanes, cos_lanes, sin_lanes, partner):
    return gain_lanes[None, :] * cos_lanes, gain_lanes[partner][None, :] * sin_lanes


def _partner_matrix(partner):
    m = np.zeros((LANES, LANES), np.float32)
    m[partner, np.arange(LANES)] = 1.0
    return jnp.asarray(m, dtype=BF16)


def _axial_tables(S, q_gain, k_gain):
    pos = jnp.arange(S)
    cr, sr = _rope_cos_sin(pos // GRID_W, AX_HEAD_DIM // 2, AX_THETA)
    cc, sc = _rope_cos_sin(pos % GRID_W, AX_HEAD_DIM // 2, AX_THETA)
    cos = jnp.concatenate([cr, cr, cc, cc], axis=1)
    sin = jnp.concatenate([-sr, sr, -sc, sc], axis=1)
    lane = np.arange(LANES)
    partner = np.where(lane % 64 < 32, lane + 32, lane - 32)
    return (_partner_matrix(partner),) + _fold_tables(q_gain, cos, sin, partner) + _fold_tables(k_gain, cos, sin, partner)


def _mla_tables(S, q_gain, k_gain):
    c, s = _rope_cos_sin(jnp.arange(S), MLA_ROPE, MLA_THETA)
    pad = LANES - MLA_QK
    cos = jnp.concatenate([jnp.ones((S, MLA_NOPE), F32), c, c, jnp.ones((S, pad), F32)], axis=1)
    sin = jnp.concatenate([jnp.zeros((S, MLA_NOPE), F32), -s, s, jnp.zeros((S, pad), F32)], axis=1)
    lane = np.arange(LANES)
    half = MLA_ROPE // 2
    partner = np.where((lane >= MLA_NOPE) & (lane < MLA_NOPE + half), lane + half,
                       np.where((lane >= MLA_NOPE + half) & (lane < MLA_QK), lane - half, lane))
    zpad = jnp.zeros((pad,), F32)
    gq = jnp.concatenate([q_gain, zpad])
    gk = jnp.concatenate([k_gain, zpad])
    return (_partner_matrix(partner),) + _fold_tables(gq, cos, sin, partner) + _fold_tables(gk, cos, sin, partner)


def _moe_layout(cnt, n_rt, n_tiles):
    tm = MOE_ROW_TILE
    cnt = cnt.reshape(n_rt, SUBLANES, LANES)[:, 0, :N_EXPERTS].astype(jnp.int32)
    chunk_rows = ((cnt + CHUNK_ALIGN - 1) // CHUNK_ALIGN) * CHUNK_ALIGN
    rt = jnp.arange(n_rt)
    before = jnp.sum(jnp.where((rt[None, :] < rt[:, None])[:, :, None], chunk_rows[None, :, :], 0), axis=1)
    region = ((jnp.sum(chunk_rows, axis=0) + tm - 1) // tm) * tm
    ex = jnp.arange(N_EXPERTS)
    ends = jnp.sum(jnp.where(ex[None, :] <= ex[:, None], region[None, :], 0), axis=1)
    chunk_base = (ends - region)[None, :] + before
    tile_start = jnp.arange(n_tiles, dtype=jnp.int32) * tm
    tile_expert = jnp.minimum(jnp.sum((tile_start[:, None] >= ends[None, :]).astype(jnp.int32), axis=1), N_EXPERTS - 1)
    tile_valid = (tile_start < ends[-1]).astype(jnp.int32)
    return chunk_rows.reshape(-1), chunk_base.reshape(-1), tile_expert, tile_valid


def kernel(x, p, attn_norm, ffn_norm, ple_norm, ple_w_in, ple_w_gate, mla_w_down, mla_q_norm, mla_w_uq, mla_kv_norm, mla_w_ukv, mla_q_gain, mla_k_gain, mla_w_o, swa_w_qkv, swa_q_gain, swa_k_gain, swa_sink, swa_w_o, ax_w_qkv, ax_q_gain, ax_k_gain, ax_w_o, ffn_w_gate, ffn_w_up, ffn_w_down, moe_w_router, moe_b_router, moe_w_gate, moe_w_up, moe_w_down):
    B, S, D = x.shape
    depth = p.shape[0]
    T = B * S
    xt = x.reshape(T, D)
    bf = lambda a: a.astype(BF16)
    n_rt = T // ROW_TILE
    n_moe_tiles = (TOP_K * T + n_rt * N_EXPERTS * (CHUNK_ALIGN - 1)) // MOE_ROW_TILE + N_EXPERTS
    slopes = jnp.asarray(2.0 ** (-8.0 * np.arange(1, SWA_HEADS + 1) / SWA_HEADS) * LOG2E, dtype=F32)
    p3 = p.reshape(depth, T, -1)

    for i in range(depth):
        kind = i % N_MIXERS
        j = i // N_MIXERS
        if kind == 0:
            wd = mla_w_down[j]
            zc = lambda n: jnp.zeros((D, n), F32)
            wd = jnp.concatenate([wd[:, :MLA_Q_RANK + MLA_KV_RANK], zc(MLA_NOPE), wd[:, MLA_Q_RANK + MLA_KV_RANK:],
                                  zc(LANES - MLA_QK)], axis=1)
            wuq = mla_w_uq[j].reshape(MLA_Q_RANK, MLA_HEADS, MLA_QK)
            wuq = jnp.pad(wuq, ((0, 0), (0, 0), (0, LANES - MLA_QK))).reshape(MLA_Q_RANK, MLA_HEADS * LANES)
            q, kv, kr = mla_proj(xt, attn_norm[i], bf(wd), mla_q_norm[j], mla_kv_norm[j], bf(wuq), bf(mla_w_ukv[j]))
            o = mla_attention(q, kv, kr, _mla_tables(S, mla_q_gain[j], mla_k_gain[j]), B, S)
            w_o = mla_w_o[j]
        elif kind == 1:
            qkv = norm_proj(xt, attn_norm[i], bf(swa_w_qkv[j]))
            gq = jnp.tile(swa_q_gain[j], 2).reshape(1, LANES)
            gk = jnp.tile(swa_k_gain[j], 2).reshape(1, LANES)
            o = swa_attention(qkv, slopes, swa_sink[j].astype(F32) * LOG2E, gq, gk, B, S)
            w_o = swa_w_o[j]
        else:
            qkv = norm_proj(xt, attn_norm[i], bf(ax_w_qkv[j]))
            o = axial_attention(qkv, _axial_tables(S, ax_q_gain[j], ax_k_gain[j]), B, S)
            w_o = ax_w_o[j]
        f = i // 2
        if i % 2 == 0:
            xt = dense_layer_tail(xt, o, bf(w_o), ffn_norm[i], bf(ffn_w_gate[f]), bf(ffn_w_up[f]), bf(ffn_w_down[f]),
                                  p3, i, ple_norm[i], bf(ple_w_in[i]), bf(ple_w_gate[i]))
        else:
            wr = jnp.pad(moe_w_router[f], ((0, 0), (0, LANES - N_EXPERTS)))
            wr_hi = bf(wr)
            wr_lo = bf(wr - wr_hi.astype(F32))
            br = jnp.pad(moe_b_router[f].astype(F32), (0, LANES - N_EXPERTS)).reshape(1, LANES)
            xt, h, idx, wt, cnt = moe_router(xt, o, bf(w_o), ffn_norm[i], wr_hi, wr_lo, br)
            chunk_rows, chunk_base, tile_expert, tile_valid = _moe_layout(cnt, n_rt, n_moe_tiles)
            h_sorted, dest = moe_dispatch(h, idx, chunk_rows, chunk_base, n_moe_tiles * MOE_ROW_TILE)
            y = moe_ffn(h_sorted, tile_expert, tile_valid, bf(moe_w_gate[f]), bf(moe_w_up[f]), bf(moe_w_down[f]), 2)
            dest = dest.reshape(n_rt, SUBLANES, ROW_TILE)
            y0 = jnp.take(y, dest[:, 0, :].reshape(T), axis=0)
            y1 = jnp.take(y, dest[:, 1, :].reshape(T), axis=0)
            xt = moe_layer_tail(xt, y0, y1, wt, p3, i, ple_norm[i], bf(ple_w_in[i]), bf(ple_w_gate[i]))
    return xt.reshape(B, S, D)
```

```python
import functools

import numpy as np
import jax
import jax.numpy as jnp
from jax import lax
from jax.experimental import pallas as pl
from jax.experimental.pallas import tpu as pltpu

F32 = jnp.float32
BF16 = jnp.bfloat16

EPS = 1e-6
GRID_W = 64
BLOCK_Q = 128

MLA_HEADS = 16
MLA_NOPE = 64
MLA_ROPE = 32
MLA_V = 64
MLA_Q_RANK = 256
MLA_KV_RANK = 128
MLA_THETA = 10000.0
MLA_QK = MLA_NOPE + MLA_ROPE

SWA_HEADS = 16
SWA_KV_HEADS = 4
SWA_HEAD_DIM = 64
SWA_WINDOW = 128

AX_HEADS = 8
AX_KV_HEADS = 4
AX_HEAD_DIM = 128
AX_THETA = 10000.0

N_EXPERTS = 8
TOP_K = 2
N_MIXERS = 3

LANES = 128
SUBLANES = 8
ROW_TILE = 512
MOE_ROW_TILE = 512
CHUNK_ALIGN = 2 * SUBLANES
ATTN_Q_TILE = 256
FFN_SUB = 256
VMEM_LIMIT = 56 * 1024 * 1024
LOG2E = 1.4426950408889634


def _cparams(*sem, flags=None):
    return pltpu.CompilerParams(dimension_semantics=sem, vmem_limit_bytes=VMEM_LIMIT, flags=flags)


ATTN_FLAGS = None


def _rms(xf, gain):
    ms = jnp.mean(xf * xf, axis=-1, keepdims=True)
    return xf * lax.rsqrt(ms + EPS) * gain


def _dot(a, b):
    return jnp.dot(a, b, preferred_element_type=F32)


def _dot_nt(a, b):
    return lax.dot_general(a, b, (((1,), (1,)), ((), ())), preferred_element_type=F32)


def _resident(shape):
    return pl.BlockSpec(shape, lambda *_: (0,) * len(shape), pipeline_mode=pl.Buffered(1))


def _norm_proj_kernel(x_ref, g_ref, w_ref, o_ref):
    h = _rms(x_ref[...], g_ref[...]).astype(BF16)
    o_ref[...] = _dot(h, w_ref[...]).astype(o_ref.dtype)


def norm_proj(x, gain, w):
    T, D = x.shape
    N = w.shape[1]
    return pl.pallas_call(
        _norm_proj_kernel,
        name="norm_proj",
        grid=(T // ROW_TILE,),
        in_specs=[
            pl.BlockSpec((ROW_TILE, D), lambda i: (i, 0)),
            _resident((1, D)),
            _resident((D, N)),
        ],
        out_specs=pl.BlockSpec((ROW_TILE, N), lambda i: (i, 0)),
        out_shape=jax.ShapeDtypeStruct((T, N), BF16),
        compiler_params=_cparams("parallel"),
    )(x, gain.reshape(1, D), w)


def _mla_proj_kernel(x_ref, g_ref, wd_ref, qn_ref, kvn_ref, wuq_ref, wukv_ref, q_ref, kv_ref, kr_ref):
    h = _rms(x_ref[...], g_ref[...]).astype(BF16)
    down = _dot(h, wd_ref[...])
    cq = _rms(down[:, :MLA_Q_RANK], qn_ref[...]).astype(BF16)
    ckv = _rms(down[:, MLA_Q_RANK:MLA_Q_RANK + MLA_KV_RANK], kvn_ref[...]).astype(BF16)
    q_ref[...] = _dot(cq, wuq_ref[...]).astype(BF16)
    kv_ref[...] = _dot(ckv, wukv_ref[...]).astype(BF16)
    kr_ref[...] = down[:, MLA_Q_RANK + MLA_KV_RANK:].astype(BF16)


def mla_proj(x, gain, wd, qn, kvn, wuq, wukv):
    T, D = x.shape
    nd = wd.shape[1]
    nq = wuq.shape[1]
    nkv = wukv.shape[1]
    row = lambda n: pl.BlockSpec((ROW_TILE, n), lambda i: (i, 0))
    return pl.pallas_call(
        _mla_proj_kernel,
        name="mla_proj",
        grid=(T // ROW_TILE,),
        in_specs=[
            row(D),
            _resident((1, D)),
            _resident((D, nd)),
            _resident((1, MLA_Q_RANK)),
            _resident((1, MLA_KV_RANK)),
            _resident((MLA_Q_RANK, nq)),
            _resident((MLA_KV_RANK, nkv)),
        ],
        out_specs=[row(nq), row(nkv), row(LANES)],
        out_shape=[
            jax.ShapeDtypeStruct((T, nq), BF16),
            jax.ShapeDtypeStruct((T, nkv), BF16),
            jax.ShapeDtypeStruct((T, LANES), BF16),
        ],
        compiler_params=_cparams("parallel"),
    )(x, gain.reshape(1, D), wd, qn.reshape(1, -1), kvn.reshape(1, -1), wuq, wukv)


def _two_unit_pipeline(n, scores, finish, emit):
    scores(0, 0)
    for i in range(n):
        scores(i, 1)
        o0 = finish(i, 0)
        o1 = finish(i, 1)
        if i + 1 < n:
            scores(i + 1, 0)
        emit(i, o0, o1)


def _store_scores(s, s_buf, m_buf=None):
    s_buf[...] = s
    if m_buf is not None:
        m_buf[...] = jnp.max(s, axis=-1, keepdims=True)


def _softmax_numerators(s_buf, m_buf=None, extra_logit=None):
    m = m_buf[...] if m_buf is not None else jnp.max(s_buf[...], axis=-1, keepdims=True)
    if extra_logit is not None:
        m = jnp.maximum(m, extra_logit)
    return jnp.exp2(s_buf[...] - m).astype(BF16), m


def _norm_rope(x, perm_ref, a, b, inv_dim, extra):
    xf = x.astype(F32)
    c = lax.rsqrt(jnp.sum(xf * xf, axis=-1, keepdims=True) * inv_dim + EPS) * extra
    return (xf * a + _dot(x, perm_ref[...]) * b) * c


def _axial_attn_kernel(q_ref, k_ref, v_ref, perm_ref, aq_ref, bq_ref, ak_ref, bk_ref, o_ref, q_scr, k_scr, v_scr,
                       s0_scr, s1_scr, *, tq, scale):
    S = k_ref.shape[0]
    s_bufs = (s0_scr, s1_scr)
    inv_dim = 1.0 / AX_HEAD_DIM

    k_scr[...] = _norm_rope(k_ref[...], perm_ref, ak_ref[...], bk_ref[...], inv_dim, 1.0).astype(BF16)
    for u in range(2):
        q_scr[u] = _norm_rope(q_ref[:, u * LANES:(u + 1) * LANES], perm_ref, aq_ref[...], bq_ref[...], inv_dim,
                              scale * LOG2E).astype(BF16)
    v_scr[:, :LANES] = v_ref[...]
    v_scr[:, LANES:] = jnp.ones((S, LANES), BF16)

    def rows(i):
        return pl.ds(i * tq, tq)

    def scores(i, u):
        _store_scores(_dot_nt(q_scr[u, rows(i), :], k_scr[...]), s_bufs[u])

    def finish(i, u):
        p, _ = _softmax_numerators(s_bufs[u])
        o = _dot(p, v_scr[...])
        return o[:, :LANES] / o[:, LANES:]

    def emit(i, o0, o1):
        o_ref[rows(i), :LANES] = o0.astype(o_ref.dtype)
        o_ref[rows(i), LANES:] = o1.astype(o_ref.dtype)

    _two_unit_pipeline(S // tq, scores, finish, emit)


def axial_attention(qkv, tabs, B, S):
    R = AX_HEADS // AX_KV_HEADS
    assert R == 2
    tq = min(ATTN_Q_TILE, S)
    kern = functools.partial(_axial_attn_kernel, tq=tq, scale=AX_HEAD_DIM ** -0.5)
    tab = pl.BlockSpec((S, LANES), lambda b, g: (0, 0), pipeline_mode=pl.Buffered(1))
    return pl.pallas_call(
        kern,
        name="axial_attn",
        grid=(B, AX_KV_HEADS),
        in_specs=[
            pl.BlockSpec((S, R * LANES), lambda b, g: (b, g)),
            pl.BlockSpec((S, LANES), lambda b, g: (b, AX_HEADS + g)),
            pl.BlockSpec((S, LANES), lambda b, g: (b, AX_HEADS + AX_KV_HEADS + g)),
            _resident((LANES, LANES)), tab, tab, tab, tab,
        ],
        out_specs=pl.BlockSpec((S, R * LANES), lambda b, g: (b, g)),
        out_shape=jax.ShapeDtypeStruct((B * S, AX_HEADS * AX_HEAD_DIM), BF16),
        scratch_shapes=[
            pltpu.VMEM((R, S, LANES), BF16),
            pltpu.VMEM((S, LANES), BF16),
            pltpu.VMEM((S, 2 * LANES), BF16),
            pltpu.VMEM((tq, S), F32),
            pltpu.VMEM((tq, S), F32),
        ],
        compiler_params=_cparams("parallel", "parallel", flags=ATTN_FLAGS),
    )(qkv, qkv, qkv, *tabs)


def _mla_attn_kernel(q_ref, kv_ref, kr_ref, perm_ref, aq_ref, bq_ref, ak_ref, bk_ref, o_ref, q_scr, k_scr, v_scr,
                     s0_scr, s1_scr, m0_scr, m1_scr, *, tq, scale):
    S = kv_ref.shape[0]
    s_bufs = (s0_scr, s1_scr)
    m_bufs = (m0_scr, m1_scr)
    lane = lax.broadcasted_iota(jnp.int32, (1, LANES), 1)
    lo = lane < MLA_NOPE
    inv_dim = 1.0 / MLA_QK

    kr = kr_ref[...]
    krf = kr.astype(F32)
    rope_part = krf * ak_ref[...] + _dot(kr, perm_ref[...]) * bk_ref[...]
    ss_rope = jnp.sum(krf * krf, axis=-1, keepdims=True)
    for hh in range(2):
        kvh = kv_ref[:, hh * LANES:(hh + 1) * LANES].astype(F32)
        ss = jnp.sum(jnp.where(lo, kvh * kvh, 0.0), axis=-1, keepdims=True) + ss_rope
        k = jnp.where(lo, kvh * ak_ref[...], rope_part)
        k_scr[hh] = (k * lax.rsqrt(ss * inv_dim + EPS)).astype(BF16)
        vh = jnp.where(lo, pltpu.roll(kvh, MLA_V, 1), 1.0) if hh == 0 else jnp.where(lo, 1.0, kvh)
        v_scr[hh] = vh.astype(BF16)
        q_scr[hh] = _norm_rope(q_ref[:, hh * LANES:(hh + 1) * LANES], perm_ref, aq_ref[...], bq_ref[...], inv_dim,
                               scale * LOG2E).astype(BF16)

    def rows(i):
        return pl.ds(i * tq, tq)

    def scores(i, u):
        _store_scores(_dot_nt(q_scr[u, rows(i), :], k_scr[u]), s_bufs[u], m_bufs[u])

    def finish(i, u):
        p, _ = _softmax_numerators(s_bufs[u], m_bufs[u])
        o = _dot(p, v_scr[u])
        return o / pltpu.roll(o, MLA_V, 1)

    def emit(i, o0, o1):
        o_ref[rows(i), :] = jnp.where(lo, o0, o1).astype(o_ref.dtype)

    _two_unit_pipeline(S // tq, scores, finish, emit)


def mla_attention(q, kv, kr, tabs, B, S):
    tq = min(ATTN_Q_TILE, S)
    kern = functools.partial(_mla_attn_kernel, tq=tq, scale=MLA_QK ** -0.5)
    tab = pl.BlockSpec((S, LANES), lambda b, g: (0, 0), pipeline_mode=pl.Buffered(1))
    return pl.pallas_call(
        kern,
        name="mla_attn",
        grid=(B, MLA_HEADS // 2),
        in_specs=[
            pl.BlockSpec((S, 2 * LANES), lambda b, g: (b, g)),
            pl.BlockSpec((S, 2 * LANES), lambda b, g: (b, g)),
            pl.BlockSpec((S, LANES), lambda b, g: (b, 0)),
            _resident((LANES, LANES)), tab, tab, tab, tab,
        ],
        out_specs=pl.BlockSpec((S, LANES), lambda b, g: (b, g)),
        out_shape=jax.ShapeDtypeStruct((B * S, MLA_HEADS * MLA_V), BF16),
        scratch_shapes=[
            pltpu.VMEM((2, S, LANES), BF16),
            pltpu.VMEM((2, S, LANES), BF16),
            pltpu.VMEM((2, S, LANES), BF16),
            pltpu.VMEM((tq, S), F32),
            pltpu.VMEM((tq, S), F32),
            pltpu.VMEM((tq, 1), F32),
            pltpu.VMEM((tq, 1), F32),
        ],
        compiler_params=_cparams("parallel", "parallel", flags=ATTN_FLAGS),
    )(q, kv, kr, *tabs)


def _swa_attn_kernel(slope_ref, sink_ref, q_ref, k_ref, v_ref, half_ref, swap_ref, gq_ref, gk_ref, o_ref,
                     q_scr, k_scr, v_scr, s0_scr, s1_scr, *, scale):
    S = k_ref.shape[0]
    span = BLOCK_Q + 2 * SWA_WINDOW
    R = SWA_HEADS // SWA_KV_HEADS
    rows_u = R * BLOCK_Q
    s_bufs = (s0_scr, s1_scr)
    pid = pl.program_id(1)
    lane = lax.broadcasted_iota(jnp.int32, (1, LANES), 1)
    lo = lane < SWA_HEAD_DIM
    hi = jnp.logical_not(lo)

    def seg_norm(x, gain):
        xf = x.astype(F32)
        sq = xf * xf
        sq_hi = sq.astype(BF16)
        sq_lo = (sq - sq_hi.astype(F32)).astype(BF16)
        ss = _dot(sq_hi, half_ref[...]) + _dot(sq_lo, half_ref[...])
        return xf * lax.rsqrt(ss * (1.0 / SWA_HEAD_DIM) + EPS) * gain

    k_scr[...] = seg_norm(k_ref[...], gk_ref[...]).astype(BF16)
    v = v_ref[...].astype(F32)
    v_scr[0] = jnp.where(lo, v, 1.0).astype(BF16)
    v_scr[1] = jnp.where(lo, 1.0, v).astype(BF16)
    for pb in range(R):
        e = pb // (R // 2)
        keep = lo if e == 0 else hi
        qp = seg_norm(q_ref[:, pb * LANES:(pb + 1) * LANES], gq_ref[...]) * (scale * LOG2E)
        qr = _dot(qp.astype(BF16), swap_ref[...])
        for i in range(2):
            qz = jnp.where(keep, qp if i == e else qr, 0.0).astype(BF16)
            r = (pb % (R // 2)) * 2 + i
            for j in range(S // BLOCK_Q):
                q_scr[e, j, r * BLOCK_Q:(r + 1) * BLOCK_Q, :] = qz[j * BLOCK_Q:(j + 1) * BLOCK_Q, :]

    head_of_row = lax.broadcasted_iota(jnp.int32, (rows_u, 1), 0) // BLOCK_Q

    def head_column(ref, e):
        col = jnp.zeros((rows_u, 1), F32)
        for r in range(R):
            col = jnp.where(head_of_row == r, ref[pid * 2 * R + e * R + r], col)
        return col

    slope_cols = [head_column(slope_ref, e) for e in range(2)]
    sink_cols = [head_column(sink_ref, e) for e in range(2)]
    rel = (lax.broadcasted_iota(jnp.int32, (rows_u, span), 0) % BLOCK_Q
           - lax.broadcasted_iota(jnp.int32, (rows_u, span), 1))

    def rows(j):
        return pl.ds(j * BLOCK_Q, BLOCK_Q)

    def window(j):
        return min(max(j * BLOCK_Q - SWA_WINDOW, 0), S - span)

    def scores(j, e):
        start = window(j)
        s = _dot_nt(q_scr[e, j], k_scr[pl.ds(start, span), :])
        dist = jnp.abs(rel + (j * BLOCK_Q - start))
        s = jnp.where(dist <= SWA_WINDOW, s - slope_cols[e] * dist.astype(F32), -jnp.inf)
        _store_scores(s, s_bufs[e])

    def finish(j, e):
        p, m = _softmax_numerators(s_bufs[e], extra_logit=sink_cols[e])
        o = _dot(p, v_scr[e, pl.ds(window(j), span), :])
        den = pltpu.roll(o, SWA_HEAD_DIM, 1) + jnp.exp2(sink_cols[e] - m)
        return o / den

    def emit(j, o0, o1):
        for e, o in ((0, o0), (1, o1)):
            orot = pltpu.roll(o, SWA_HEAD_DIM, 1)
            for k in range(R // 2):
                pb = e * (R // 2) + k
                even = (o if e == 0 else orot)[2 * k * BLOCK_Q:(2 * k + 1) * BLOCK_Q]
                odd = (o if e == 1 else orot)[(2 * k + 1) * BLOCK_Q:(2 * k + 2) * BLOCK_Q]
                o_ref[rows(j), pb * LANES:(pb + 1) * LANES] = jnp.where(lo, even, odd).astype(o_ref.dtype)

    _two_unit_pipeline(S // BLOCK_Q, scores, finish, emit)


def swa_attention(qkv, slopes, sink, gq, gk, B, S):
    n_steps = SWA_KV_HEADS // 2
    R = SWA_HEADS // SWA_KV_HEADS
    span = BLOCK_Q + 2 * SWA_WINDOW
    qw = SWA_HEADS * SWA_HEAD_DIM // n_steps
    kbase = SWA_HEADS * SWA_HEAD_DIM // LANES
    smem = pl.BlockSpec(memory_space=pltpu.SMEM)
    gain = pl.BlockSpec((1, LANES), lambda b, g: (0, 0))
    kern = functools.partial(_swa_attn_kernel, scale=SWA_HEAD_DIM ** -0.5)
    lane = np.arange(LANES)
    half_ones = jnp.asarray(lane[:, None] // SWA_HEAD_DIM == lane[None, :] // SWA_HEAD_DIM, dtype=BF16)
    swap = _partner_matrix((lane + SWA_HEAD_DIM) % LANES)
    return pl.pallas_call(
        kern,
        name="swa_attn",
        grid=(B, n_steps),
        in_specs=[
            smem, smem,
            pl.BlockSpec((S, qw), lambda b, g: (b, g)),
            pl.BlockSpec((S, LANES), lambda b, g: (b, kbase + g)),
            pl.BlockSpec((S, LANES), lambda b, g: (b, kbase + n_steps + g)),
            _resident((LANES, LANES)), _resident((LANES, LANES)),
            gain, gain,
        ],
        out_specs=pl.BlockSpec((S, qw), lambda b, g: (b, g)),
        out_shape=jax.ShapeDtypeStruct((B * S, SWA_HEADS * SWA_HEAD_DIM), BF16),
        scratch_shapes=[
            pltpu.VMEM((2, S // BLOCK_Q, R * BLOCK_Q, LANES), BF16),
            pltpu.VMEM((S, LANES), BF16),
            pltpu.VMEM((2, S, LANES), BF16),
            pltpu.VMEM((R * BLOCK_Q, span), F32),
            pltpu.VMEM((R * BLOCK_Q, span), F32),
        ],
        compiler_params=_cparams("parallel", "parallel", flags=ATTN_FLAGS),
    )(slopes, sink, qkv, qkv, qkv, half_ones, swap, gq, gk)


def _swiglu_accumulate(h, wg_ref, wu_ref, wd_ref, acc_ref):
    for c in range(wg_ref.shape[1] // FFN_SUB):
        sl = slice(c * FFN_SUB, (c + 1) * FFN_SUB)
        g = _dot(h, wg_ref[:, sl])
        u = _dot(h, wu_ref[:, sl])
        a = (g * jax.nn.sigmoid(g) * u).astype(BF16)
        acc_ref[...] += _dot(a, wd_ref[sl, :])


def _ple(x, p, gain, win_ref, wgate_ref):
    gate = jax.nn.sigmoid(_dot(_rms(x, gain).astype(BF16), wgate_ref[...]))
    return x + _dot(p.astype(BF16), win_ref[...]) * gate


def _dense_tail_kernel(x_ref, a_ref, wo_ref, g_ref, wg_ref, wu_ref, wd_ref, p_ref, gp_ref, win_ref, wgate_ref, o_ref,
                       acc_scr):
    x = x_ref[...] + _dot(a_ref[...], wo_ref[...])
    acc_scr[...] = x
    _swiglu_accumulate(_rms(x, g_ref[...]).astype(BF16), wg_ref, wu_ref, wd_ref, acc_scr)
    o_ref[...] = _ple(acc_scr[...], p_ref[...], gp_ref[...], win_ref, wgate_ref)


def dense_layer_tail(x, a, w_o, gain, wg, wu, wd, p, layer, ple_gain, w_in, w_gate):
    T, D = x.shape
    K = a.shape[1]
    Fd = wg.shape[1]
    P = p.shape[2]
    row = lambda n: pl.BlockSpec((ROW_TILE, n), lambda i: (i, 0))
    return pl.pallas_call(
        _dense_tail_kernel,
        name="dense_tail",
        grid=(T // ROW_TILE,),
        in_specs=[
            row(D), row(K), _resident((K, D)),
            _resident((1, D)), _resident((D, Fd)), _resident((D, Fd)), _resident((Fd, D)),
            pl.BlockSpec((None, ROW_TILE, P), lambda i: (layer, i, 0)),
            _resident((1, D)), _resident((P, D)), _resident((D, D)),
        ],
        out_specs=row(D),
        out_shape=jax.ShapeDtypeStruct((T, D), F32),
        scratch_shapes=[pltpu.VMEM((ROW_TILE, D), F32)],
        compiler_params=_cparams("parallel"),
    )(x, a, w_o, gain.reshape(1, D), wg, wu, wd, p, ple_gain.reshape(1, D), w_in, w_gate)


def _router_kernel(x_ref, a_ref, wo_ref, g_ref, whi_ref, wlo_ref, b_ref, x1_ref, h_ref, idx_ref, wt_ref, cnt_ref):
    x1 = x_ref[...] + _dot(a_ref[...], wo_ref[...])
    x1_ref[...] = x1
    hf = _rms(x1, g_ref[...])
    h_hi = hf.astype(BF16)
    h_lo = (hf - h_hi.astype(F32)).astype(BF16)
    h_ref[...] = h_hi
    logits = _dot(h_hi, whi_ref[...]) + _dot(h_hi, wlo_ref[...]) + _dot(h_lo, whi_ref[...]) + b_ref[...]
    lane = lax.broadcasted_iota(jnp.int32, logits.shape, 1)
    logits = jnp.where(lane < N_EXPERTS, logits, -jnp.inf)
    m1 = jnp.max(logits, axis=-1, keepdims=True)
    i1 = jnp.min(jnp.where(logits == m1, lane, LANES), axis=-1, keepdims=True)
    rest = jnp.where(lane == i1, -jnp.inf, logits)
    m2 = jnp.max(rest, axis=-1, keepdims=True)
    i2 = jnp.min(jnp.where(rest == m2, lane, LANES), axis=-1, keepdims=True)
    e2 = jnp.exp(m2 - m1)
    w1 = 1.0 / (1.0 + e2)
    w2 = e2 / (1.0 + e2)
    wt_ref[...] = jnp.where(lane == 0, w1, jnp.where(lane == 1, w2, 0.0))
    onehot = jnp.where(jnp.logical_or(lane == i1, lane == i2), 1.0, 0.0)
    tm = onehot.shape[0]
    earlier = (lax.broadcasted_iota(jnp.int32, (tm, tm), 0) > lax.broadcasted_iota(jnp.int32, (tm, tm), 1))
    prefix = _dot(jnp.where(earlier, 1.0, 0.0).astype(BF16), onehot.astype(BF16))
    r1 = jnp.sum(jnp.where(lane == i1, prefix, 0.0), axis=-1, keepdims=True).astype(jnp.int32)
    r2 = jnp.sum(jnp.where(lane == i2, prefix, 0.0), axis=-1, keepdims=True).astype(jnp.int32)
    idx = jnp.where(lane == 0, i1, jnp.where(lane == 1, i2, jnp.where(lane == 2, r1, jnp.where(lane == 3, r2, 0))))
    idx_ref[...] = idx.T[:SUBLANES, :]
    cnt_ref[...] = jnp.broadcast_to(jnp.sum(onehot, axis=0, keepdims=True), cnt_ref.shape)


def moe_router(x, a, w_o, gain, w_hi, w_lo, bias):
    T, D = x.shape
    K = a.shape[1]
    row = lambda n: pl.BlockSpec((ROW_TILE, n), lambda i: (i, 0))
    return pl.pallas_call(
        _router_kernel,
        name="moe_router",
        grid=(T // ROW_TILE,),
        in_specs=[row(D), row(K), _resident((K, D)), _resident((1, D)), _resident((D, LANES)), _resident((D, LANES)),
                  _resident((1, LANES))],
        out_specs=[row(D), row(D), pl.BlockSpec((SUBLANES, ROW_TILE), lambda i: (i, 0)), row(LANES),
                   pl.BlockSpec((SUBLANES, LANES), lambda i: (i, 0))],
        out_shape=[
            jax.ShapeDtypeStruct((T, D), F32),
            jax.ShapeDtypeStruct((T, D), BF16),
            jax.ShapeDtypeStruct((T // ROW_TILE * SUBLANES, ROW_TILE), jnp.int32),
            jax.ShapeDtypeStruct((T, LANES), F32),
            jax.ShapeDtypeStruct((T // ROW_TILE * SUBLANES, LANES), F32),
        ],
        compiler_params=_cparams("parallel"),
    )(x, a, w_o, gain.reshape(1, D), w_hi, w_lo, bias)


def _moe_dispatch_kernel(n_ref, base_ref, h_ref, idx_ref, zeros_ref, hs_ref, dest_ref, loc_scr, sem):
    del zeros_ref
    i = pl.program_id(0)
    e0, e1, r0, r1 = (idx_ref[k:k + 1, :] for k in range(4))
    slot0, slot1, dest0, dest1 = r0, r1, r0, r1
    offs = []
    off = jnp.int32(0)
    for e in range(N_EXPERTS):
        offs.append(off)
        base = base_ref[i * N_EXPERTS + e]
        slot0 = slot0 + jnp.where(e0 == e, off, 0)
        slot1 = slot1 + jnp.where(e1 == e, off, 0)
        dest0 = dest0 + jnp.where(e0 == e, base, 0)
        dest1 = dest1 + jnp.where(e1 == e, base, 0)
        off = off + n_ref[i * N_EXPERTS + e]
    row = lax.broadcasted_iota(jnp.int32, (SUBLANES, e0.shape[1]), 0)
    dest_ref[...] = jnp.where(row == 0, dest0, jnp.where(row == 1, dest1, 0))
    slot = lax.broadcasted_iota(jnp.int32, (loc_scr.shape[0], e0.shape[1]), 0)
    perm = jnp.where(jnp.logical_or(slot == slot0, slot == slot1), 1.0, 0.0).astype(BF16)
    loc_scr[...] = _dot(perm, h_ref[...]).astype(BF16)

    def piece(src_row, dst_row):
        return pltpu.make_async_copy(loc_scr.at[pl.ds(src_row, CHUNK_ALIGN), :],
                                     hs_ref.at[pl.ds(dst_row, CHUNK_ALIGN), :], sem)

    for e in range(N_EXPERTS):
        base = base_ref[i * N_EXPERTS + e]

        def start(g, carry, e=e, base=base):
            piece(pl.multiple_of(offs[e] + g * CHUNK_ALIGN, CHUNK_ALIGN),
                  pl.multiple_of(base + g * CHUNK_ALIGN, CHUNK_ALIGN)).start()
            return carry

        lax.fori_loop(0, n_ref[i * N_EXPERTS + e] // CHUNK_ALIGN, start, 0)

    def wait(g, carry):
        piece(0, 0).wait()
        return carry

    lax.fori_loop(0, off // CHUNK_ALIGN, wait, 0)


def moe_dispatch(h, idx, chunk_rows, chunk_base, n_rows):
    T, D = h.shape
    n_rt = T // ROW_TILE
    loc_rows = TOP_K * ROW_TILE + N_EXPERTS * CHUNK_ALIGN
    grid_spec = pltpu.PrefetchScalarGridSpec(
        num_scalar_prefetch=2,
        grid=(n_rt,),
        in_specs=[
            pl.BlockSpec((ROW_TILE, D), lambda i, n, b: (i, 0)),
            pl.BlockSpec((SUBLANES, ROW_TILE), lambda i, n, b: (i, 0)),
            pl.BlockSpec(memory_space=pl.ANY),
        ],
        out_specs=[
            pl.BlockSpec(memory_space=pl.ANY),
            pl.BlockSpec((SUBLANES, ROW_TILE), lambda i, n, b: (i, 0)),
        ],
        scratch_shapes=[pltpu.VMEM((loc_rows, D), BF16), pltpu.SemaphoreType.DMA],
    )
    return pl.pallas_call(
        _moe_dispatch_kernel,
        name="moe_dispatch",
        grid_spec=grid_spec,
        out_shape=[
            jax.ShapeDtypeStruct((n_rows, D), BF16),
            jax.ShapeDtypeStruct((n_rt * SUBLANES, ROW_TILE), jnp.int32),
        ],
        input_output_aliases={4: 0},
        compiler_params=_cparams("arbitrary"),
    )(chunk_rows, chunk_base, h, idx, jnp.zeros((n_rows, D), BF16))


def _moe_ffn_kernel(te_ref, tv_ref, h_ref, wg_ref, wu_ref, wd_ref, o_ref, acc_scr):
    i = pl.program_id(0)
    f = pl.program_id(1)
    last = pl.num_programs(1) - 1

    @pl.when(f == 0)
    def _():
        acc_scr[...] = jnp.zeros_like(acc_scr)

    @pl.when(tv_ref[i] > 0)
    def _():
        _swiglu_accumulate(h_ref[...], wg_ref, wu_ref, wd_ref, acc_scr)

    @pl.when(f == last)
    def _():
        o_ref[...] = acc_scr[...].astype(o_ref.dtype)


def moe_ffn(h_sorted, tile_expert, tile_valid, wg, wu, wd, layer, n_chunks):
    R, D = h_sorted.shape
    Fe = wg.shape[3]
    tf = Fe // n_chunks

    def chunk(i, f, tv):
        return jnp.where(tv[i] > 0, f, n_chunks - 1)

    grid_spec = pltpu.PrefetchScalarGridSpec(
        num_scalar_prefetch=2,
        grid=(R // MOE_ROW_TILE, n_chunks),
        in_specs=[
            pl.BlockSpec((MOE_ROW_TILE, D), lambda i, f, te, tv: (i, 0)),
            pl.BlockSpec((None, None, D, tf), lambda i, f, te, tv: (layer, te[i], 0, chunk(i, f, tv))),
            pl.BlockSpec((None, None, D, tf), lambda i, f, te, tv: (layer, te[i], 0, chunk(i, f, tv))),
            pl.BlockSpec((None, None, tf, D), lambda i, f, te, tv: (layer, te[i], chunk(i, f, tv), 0)),
        ],
        out_specs=pl.BlockSpec((MOE_ROW_TILE, D), lambda i, f, te, tv: (i, 0)),
        scratch_shapes=[pltpu.VMEM((MOE_ROW_TILE, D), F32)],
    )
    return pl.pallas_call(
        _moe_ffn_kernel,
        name="moe_ffn",
        grid_spec=grid_spec,
        out_shape=jax.ShapeDtypeStruct((R, D), BF16),
        compiler_params=_cparams("parallel", "arbitrary"),
    )(tile_expert, tile_valid, h_sorted, wg, wu, wd)


def _moe_tail_kernel(x_ref, y0_ref, y1_ref, wt_ref, p_ref, g_ref, win_ref, wgate_ref, o_ref):
    wt = wt_ref[...]
    x = x_ref[...] + wt[:, 0:1] * y0_ref[...].astype(F32) + wt[:, 1:2] * y1_ref[...].astype(F32)
    o_ref[...] = _ple(x, p_ref[...], g_ref[...], win_ref, wgate_ref)


def moe_layer_tail(x, y0, y1, wt, p, layer, gain, w_in, w_gate):
    T, D = x.shape
    P = p.shape[2]
    row = lambda n: pl.BlockSpec((ROW_TILE, n), lambda i: (i, 0))
    return pl.pallas_call(
        _moe_tail_kernel,
        name="moe_tail",
        grid=(T // ROW_TILE,),
        in_specs=[row(D), row(D), row(D), row(LANES), pl.BlockSpec((None, ROW_TILE, P), lambda i: (layer, i, 0)),
                  _resident((1, D)), _resident((P, D)), _resident((D, D))],
        out_specs=row(D),
        out_shape=jax.ShapeDtypeStruct((T, D), F32),
        compiler_params=_cparams("parallel"),
    )(x, y0, y1, wt, p, gain.reshape(1, D), w_in, w_gate)


def _rope_cos_sin(pos, dim, theta):
    inv = theta ** (-jnp.arange(0, dim, 2, dtype=F32) / dim)
    ang = pos.astype(F32)[:, None] * inv[None, :]
    return jnp.cos(ang), jnp.sin(ang)


def _fold_tables(gain_lanes, cos_lanes, sin_lanes, partner):
    return gain_lanes[None, :] * cos_lanes, gain_lanes[partner][None, :] * sin_lanes


def _partner_matrix(partner):
    m = np.zeros((LANES, LANES), np.float32)
    m[partner, np.arange(LANES)] = 1.0
    return jnp.asarray(m, dtype=BF16)


def _axial_tables(S, q_gain, k_gain):
    pos = jnp.arange(S)
    cr, sr = _rope_cos_sin(pos // GRID_W, AX_HEAD_DIM // 2, AX_THETA)
    cc, sc = _rope_cos_sin(pos % GRID_W, AX_HEAD_DIM // 2, AX_THETA)
    cos = jnp.concatenate([cr, cr, cc, cc], axis=1)
    sin = jnp.concatenate([-sr, sr, -sc, sc], axis=1)
    lane = np.arange(LANES)
    partner = np.where(lane % 64 < 32, lane + 32, lane - 32)
    return (_partner_matrix(partner),) + _fold_tables(q_gain, cos, sin, partner) + _fold_tables(k_gain, cos, sin, partner)


def _mla_tables(S, q_gain, k_gain):
    c, s = _rope_cos_sin(jnp.arange(S), MLA_ROPE, MLA_THETA)
    pad = LANES - MLA_QK
    cos = jnp.concatenate([jnp.ones((S, MLA_NOPE), F32), c, c, jnp.ones((S, pad), F32)], axis=1)
    sin = jnp.concatenate([jnp.zeros((S, MLA_NOPE), F32), -s, s, jnp.zeros((S, pad), F32)], axis=1)
    lane = np.arange(LANES)
    half = MLA_ROPE // 2
    partner = np.where((lane >= MLA_NOPE) & (lane < MLA_NOPE + half), lane + half,
                       np.where((lane >= MLA_NOPE + half) & (lane < MLA_QK), lane - half, lane))
    zpad = jnp.zeros((pad,), F32)
    gq = jnp.concatenate([q_gain, zpad])
    gk = jnp.concatenate([k_gain, zpad])
    return (_partner_matrix(partner),) + _fold_tables(gq, cos, sin, partner) + _fold_tables(gk, cos, sin, partner)


def _moe_layout(cnt, n_rt, n_tiles):
    tm = MOE_ROW_TILE
    cnt = cnt.reshape(n_rt, SUBLANES, LANES)[:, 0, :N_EXPERTS].astype(jnp.int32)
    chunk_rows = ((cnt + CHUNK_ALIGN - 1) // CHUNK_ALIGN) * CHUNK_ALIGN
    rt = jnp.arange(n_rt)
    before = jnp.sum(jnp.where((rt[None, :] < rt[:, None])[:, :, None], chunk_rows[None, :, :], 0), axis=1)
    region = ((jnp.sum(chunk_rows, axis=0) + tm - 1) // tm) * tm
    ex = jnp.arange(N_EXPERTS)
    ends = jnp.sum(jnp.where(ex[None, :] <= ex[:, None], region[None, :], 0), axis=1)
    chunk_base = (ends - region)[None, :] + before
    tile_start = jnp.arange(n_tiles, dtype=jnp.int32) * tm
    tile_expert = jnp.minimum(jnp.sum((tile_start[:, None] >= ends[None, :]).astype(jnp.int32), axis=1), N_EXPERTS - 1)
    tile_valid = (tile_start < ends[-1]).astype(jnp.int32)
    return chunk_rows.reshape(-1), chunk_base.reshape(-1), tile_expert, tile_valid


def kernel(x, p, attn_norm, ffn_norm, ple_norm, ple_w_in, ple_w_gate, mla_w_down, mla_q_norm, mla_w_uq, mla_kv_norm, mla_w_ukv, mla_q_gain, mla_k_gain, mla_w_o, swa_w_qkv, swa_q_gain, swa_k_gain, swa_sink, swa_w_o, ax_w_qkv, ax_q_gain, ax_k_gain, ax_w_o, ffn_w_gate, ffn_w_up, ffn_w_down, moe_w_router, moe_b_router, moe_w_gate, moe_w_up, moe_w_down):
    B, S, D = x.shape
    depth = p.shape[0]
    T = B * S
    xt = x.reshape(T, D)
    bf = lambda a: a.astype(BF16)
    n_rt = T // ROW_TILE
    n_moe_tiles = (TOP_K * T + n_rt * N_EXPERTS * (CHUNK_ALIGN - 1)) // MOE_ROW_TILE + N_EXPERTS
    slopes = jnp.asarray(2.0 ** (-8.0 * np.arange(1, SWA_HEADS + 1) / SWA_HEADS) * LOG2E, dtype=F32)
    p3 = p.reshape(depth, T, -1)
    moe_wg, moe_wu, moe_wd = bf(moe_w_gate), bf(moe_w_up), bf(moe_w_down)

    for i in range(depth):
        kind = i % N_MIXERS
        j = i // N_MIXERS
        if kind == 0:
            wd = mla_w_down[j]
            zc = lambda n: jnp.zeros((D, n), F32)
            wd = jnp.concatenate([wd[:, :MLA_Q_RANK + MLA_KV_RANK], zc(MLA_NOPE), wd[:, MLA_Q_RANK + MLA_KV_RANK:],
                                  zc(LANES - MLA_QK)], axis=1)
            wuq = mla_w_uq[j].reshape(MLA_Q_RANK, MLA_HEADS, MLA_QK)
            wuq = jnp.pad(wuq, ((0, 0), (0, 0), (0, LANES - MLA_QK))).reshape(MLA_Q_RANK, MLA_HEADS * LANES)
            q, kv, kr = mla_proj(xt, attn_norm[i], bf(wd), mla_q_norm[j], mla_kv_norm[j], bf(wuq), bf(mla_w_ukv[j]))
            o = mla_attention(q, kv, kr, _mla_tables(S, mla_q_gain[j], mla_k_gain[j]), B, S)
            w_o = mla_w_o[j]
        elif kind == 1:
            qkv = norm_proj(xt, attn_norm[i], bf(swa_w_qkv[j]))
            gq = jnp.tile(swa_q_gain[j], 2).reshape(1, LANES)
            gk = jnp.tile(swa_k_gain[j], 2).reshape(1, LANES)
            o = swa_attention(qkv, slopes, swa_sink[j].astype(F32) * LOG2E, gq, gk, B, S)
            w_o = swa_w_o[j]
        else:
            qkv = norm_proj(xt, attn_norm[i], bf(ax_w_qkv[j]))
            o = axial_attention(qkv, _axial_tables(S, ax_q_gain[j], ax_k_gain[j]), B, S)
            w_o = ax_w_o[j]
        f = i // 2
        if i % 2 == 0:
            xt = dense_layer_tail(xt, o, bf(w_o), ffn_norm[i], bf(ffn_w_gate[f]), bf(ffn_w_up[f]), bf(ffn_w_down[f]),
                                  p3, i, ple_norm[i], bf(ple_w_in[i]), bf(ple_w_gate[i]))
        else:
            wr = jnp.pad(moe_w_router[f], ((0, 0), (0, LANES - N_EXPERTS)))
            wr_hi = bf(wr)
            wr_lo = bf(wr - wr_hi.astype(F32))
            br = jnp.pad(moe_b_router[f].astype(F32), (0, LANES - N_EXPERTS)).reshape(1, LANES)
            xt, h, idx, wt, cnt = moe_router(xt, o, bf(w_o), ffn_norm[i], wr_hi, wr_lo, br)
            chunk_rows, chunk_base, tile_expert, tile_valid = _moe_layout(cnt, n_rt, n_moe_tiles)
            h_sorted, dest = moe_dispatch(h, idx, chunk_rows, chunk_base, n_moe_tiles * MOE_ROW_TILE)
            y = moe_ffn(h_sorted, tile_expert, tile_valid, moe_wg, moe_wu, moe_wd, f, 2)
            dest = dest.reshape(n_rt, SUBLANES, ROW_TILE)
            y0 = jnp.take(y, dest[:, 0, :].reshape(T), axis=0, mode="clip")
            y1 = jnp.take(y, dest[:, 1, :].reshape(T), axis=0, mode="clip")
            xt = moe_layer_tail(xt, y0, y1, wt, p3, i, ple_norm[i], bf(ple_w_in[i]), bf(ple_w_gate[i]))
    return xt.reshape(B, S, D)
```

```python
import functools

import numpy as np
import jax
import jax.numpy as jnp
from jax import lax
from jax.experimental import pallas as pl
from jax.experimental.pallas import tpu as pltpu

F32 = jnp.float32
BF16 = jnp.bfloat16

EPS = 1e-6
GRID_W = 64
BLOCK_Q = 128

MLA_HEADS = 16
MLA_NOPE = 64
MLA_ROPE = 32
MLA_V = 64
MLA_Q_RANK = 256
MLA_KV_RANK = 128
MLA_THETA = 10000.0
MLA_QK = MLA_NOPE + MLA_ROPE

SWA_HEADS = 16
SWA_KV_HEADS = 4
SWA_HEAD_DIM = 64
SWA_WINDOW = 128

AX_HEADS = 8
AX_KV_HEADS = 4
AX_HEAD_DIM = 128
AX_THETA = 10000.0

N_EXPERTS = 8
TOP_K = 2
N_MIXERS = 3

LANES = 128
SUBLANES = 8
ROW_TILE = 512
MOE_ROW_TILE = 512
CHUNK_ALIGN = 2 * SUBLANES
ATTN_Q_TILE = 256
FFN_SUB = 256
VMEM_LIMIT = 56 * 1024 * 1024
LOG2E = 1.4426950408889634


def _cparams(*sem):
    return pltpu.CompilerParams(dimension_semantics=sem, vmem_limit_bytes=VMEM_LIMIT)


def _rms(xf, gain):
    ms = jnp.mean(xf * xf, axis=-1, keepdims=True)
    return xf * lax.rsqrt(ms + EPS) * gain


def _dot(a, b):
    return jnp.dot(a, b, preferred_element_type=F32)


def _dot_nt(a, b):
    return lax.dot_general(a, b, (((1,), (1,)), ((), ())), preferred_element_type=F32)


def _resident(shape):
    return pl.BlockSpec(shape, lambda *_: (0,) * len(shape), pipeline_mode=pl.Buffered(1))


def _norm_proj_kernel(x_ref, g_ref, w_ref, o_ref):
    h = _rms(x_ref[...], g_ref[...]).astype(BF16)
    o_ref[...] = _dot(h, w_ref[...]).astype(o_ref.dtype)


def norm_proj(x, gain, w):
    T, D = x.shape
    N = w.shape[1]
    return pl.pallas_call(
        _norm_proj_kernel,
        name="norm_proj",
        grid=(T // ROW_TILE,),
        in_specs=[
            pl.BlockSpec((ROW_TILE, D), lambda i: (i, 0)),
            _resident((1, D)),
            _resident((D, N)),
        ],
        out_specs=pl.BlockSpec((ROW_TILE, N), lambda i: (i, 0)),
        out_shape=jax.ShapeDtypeStruct((T, N), BF16),
        compiler_params=_cparams("parallel"),
    )(x, gain.reshape(1, D), w)


def _mla_proj_kernel(x_ref, g_ref, wd_ref, qn_ref, kvn_ref, wuq_ref, wukv_ref, q_ref, kv_ref, kr_ref):
    h = _rms(x_ref[...], g_ref[...]).astype(BF16)
    down = _dot(h, wd_ref[...])
    cq = _rms(down[:, :MLA_Q_RANK], qn_ref[...]).astype(BF16)
    ckv = _rms(down[:, MLA_Q_RANK:MLA_Q_RANK + MLA_KV_RANK], kvn_ref[...]).astype(BF16)
    q_ref[...] = _dot(cq, wuq_ref[...]).astype(BF16)
    kv_ref[...] = _dot(ckv, wukv_ref[...]).astype(BF16)
    kr_ref[...] = down[:, MLA_Q_RANK + MLA_KV_RANK:].astype(BF16)


def mla_proj(x, gain, wd, qn, kvn, wuq, wukv):
    T, D = x.shape
    nd = wd.shape[1]
    nq = wuq.shape[1]
    nkv = wukv.shape[1]
    row = lambda n: pl.BlockSpec((ROW_TILE, n), lambda i: (i, 0))
    return pl.pallas_call(
        _mla_proj_kernel,
        name="mla_proj",
        grid=(T // ROW_TILE,),
        in_specs=[
            row(D),
            _resident((1, D)),
            _resident((D, nd)),
            _resident((1, MLA_Q_RANK)),
            _resident((1, MLA_KV_RANK)),
            _resident((MLA_Q_RANK, nq)),
            _resident((MLA_KV_RANK, nkv)),
        ],
        out_specs=[row(nq), row(nkv), row(LANES)],
        out_shape=[
            jax.ShapeDtypeStruct((T, nq), BF16),
            jax.ShapeDtypeStruct((T, nkv), BF16),
            jax.ShapeDtypeStruct((T, LANES), BF16),
        ],
        compiler_params=_cparams("parallel"),
    )(x, gain.reshape(1, D), wd, qn.reshape(1, -1), kvn.reshape(1, -1), wuq, wukv)


def _two_unit_pipeline(n, scores, finish, emit):
    scores(0, 0)
    for i in range(n):
        scores(i, 1)
        o0 = finish(i, 0)
        o1 = finish(i, 1)
        if i + 1 < n:
            scores(i + 1, 0)
        emit(i, o0, o1)


def _store_scores(s, s_buf, m_buf=None):
    s_buf[...] = s
    if m_buf is not None:
        m_buf[...] = jnp.max(s, axis=-1, keepdims=True)


def _softmax_numerators(s_buf, m_buf=None, extra_logit=None):
    m = m_buf[...] if m_buf is not None else jnp.max(s_buf[...], axis=-1, keepdims=True)
    if extra_logit is not None:
        m = jnp.maximum(m, extra_logit)
    return jnp.exp2(s_buf[...] - m).astype(BF16), m


def _norm_rope(x, perm_ref, a, b, inv_dim, extra):
    xf = x.astype(F32)
    c = lax.rsqrt(jnp.sum(xf * xf, axis=-1, keepdims=True) * inv_dim + EPS) * extra
    return (xf * a + _dot(x, perm_ref[...]) * b) * c


def _axial_attn_kernel(q_ref, k_ref, v_ref, perm_ref, aq_ref, bq_ref, ak_ref, bk_ref, o_ref, q_scr, k_scr, v_scr,
                       s0_scr, s1_scr, *, tq, scale):
    S = k_ref.shape[0]
    s_bufs = (s0_scr, s1_scr)
    inv_dim = 1.0 / AX_HEAD_DIM

    k_scr[...] = _norm_rope(k_ref[...], perm_ref, ak_ref[...], bk_ref[...], inv_dim, 1.0).astype(BF16)
    for u in range(2):
        q_scr[u] = _norm_rope(q_ref[:, u * LANES:(u + 1) * LANES], perm_ref, aq_ref[...], bq_ref[...], inv_dim,
                              scale * LOG2E).astype(BF16)
    v_scr[:, :LANES] = v_ref[...]
    v_scr[:, LANES:] = jnp.ones((S, LANES), BF16)

    def rows(i):
        return pl.ds(i * tq, tq)

    def scores(i, u):
        _store_scores(_dot_nt(q_scr[u, rows(i), :], k_scr[...]), s_bufs[u])

    def finish(i, u):
        p, _ = _softmax_numerators(s_bufs[u])
        o = _dot(p, v_scr[...])
        return o[:, :LANES] / o[:, LANES:]

    def emit(i, o0, o1):
        o_ref[rows(i), :LANES] = o0.astype(o_ref.dtype)
        o_ref[rows(i), LANES:] = o1.astype(o_ref.dtype)

    _two_unit_pipeline(S // tq, scores, finish, emit)


def axial_attention(qkv, tabs, B, S):
    R = AX_HEADS // AX_KV_HEADS
    assert R == 2
    tq = min(ATTN_Q_TILE, S)
    kern = functools.partial(_axial_attn_kernel, tq=tq, scale=AX_HEAD_DIM ** -0.5)
    tab = pl.BlockSpec((S, LANES), lambda b, g: (0, 0), pipeline_mode=pl.Buffered(1))
    return pl.pallas_call(
        kern,
        name="axial_attn",
        grid=(B, AX_KV_HEADS),
        in_specs=[
            pl.BlockSpec((S, R * LANES), lambda b, g: (b, g)),
            pl.BlockSpec((S, LANES), lambda b, g: (b, AX_HEADS + g)),
            pl.BlockSpec((S, LANES), lambda b, g: (b, AX_HEADS + AX_KV_HEADS + g)),
            _resident((LANES, LANES)), tab, tab, tab, tab,
        ],
        out_specs=pl.BlockSpec((S, R * LANES), lambda b, g: (b, g)),
        out_shape=jax.ShapeDtypeStruct((B * S, AX_HEADS * AX_HEAD_DIM), BF16),
        scratch_shapes=[
            pltpu.VMEM((R, S, LANES), BF16),
            pltpu.VMEM((S, LANES), BF16),
            pltpu.VMEM((S, 2 * LANES), BF16),
            pltpu.VMEM((tq, S), F32),
            pltpu.VMEM((tq, S), F32),
        ],
        compiler_params=_cparams("parallel", "parallel"),
    )(qkv, qkv, qkv, *tabs)


def _mla_attn_kernel(q_ref, kv_ref, kr_ref, perm_ref, aq_ref, bq_ref, ak_ref, bk_ref, o_ref, q_scr, k_scr, v_scr,
                     s0_scr, s1_scr, m0_scr, m1_scr, *, tq, scale):
    S = kv_ref.shape[0]
    s_bufs = (s0_scr, s1_scr)
    m_bufs = (m0_scr, m1_scr)
    lane = lax.broadcasted_iota(jnp.int32, (1, LANES), 1)
    lo = lane < MLA_NOPE
    inv_dim = 1.0 / MLA_QK

    kr = kr_ref[...]
    krf = kr.astype(F32)
    rope_part = krf * ak_ref[...] + _dot(kr, perm_ref[...]) * bk_ref[...]
    ss_rope = jnp.sum(krf * krf, axis=-1, keepdims=True)
    for hh in range(2):
        kvh = kv_ref[:, hh * LANES:(hh + 1) * LANES].astype(F32)
        ss = jnp.sum(jnp.where(lo, kvh * kvh, 0.0), axis=-1, keepdims=True) + ss_rope
        k = jnp.where(lo, kvh * ak_ref[...], rope_part)
        k_scr[hh] = (k * lax.rsqrt(ss * inv_dim + EPS)).astype(BF16)
        vh = jnp.where(lo, pltpu.roll(kvh, MLA_V, 1), 1.0) if hh == 0 else jnp.where(lo, 1.0, kvh)
        v_scr[hh] = vh.astype(BF16)
        q_scr[hh] = _norm_rope(q_ref[:, hh * LANES:(hh + 1) * LANES], perm_ref, aq_ref[...], bq_ref[...], inv_dim,
                               scale * LOG2E).astype(BF16)

    def rows(i):
        return pl.ds(i * tq, tq)

    def scores(i, u):
        _store_scores(_dot_nt(q_scr[u, rows(i), :], k_scr[u]), s_bufs[u], m_bufs[u])

    def finish(i, u):
        p, _ = _softmax_numerators(s_bufs[u], m_bufs[u])
        o = _dot(p, v_scr[u])
        return o / pltpu.roll(o, MLA_V, 1)

    def emit(i, o0, o1):
        o_ref[rows(i), :] = jnp.where(lo, o0, o1).astype(o_ref.dtype)

    _two_unit_pipeline(S // tq, scores, finish, emit)


def mla_attention(q, kv, kr, tabs, B, S):
    tq = min(ATTN_Q_TILE, S)
    kern = functools.partial(_mla_attn_kernel, tq=tq, scale=MLA_QK ** -0.5)
    tab = pl.BlockSpec((S, LANES), lambda b, g: (0, 0), pipeline_mode=pl.Buffered(1))
    return pl.pallas_call(
        kern,
        name="mla_attn",
        grid=(B, MLA_HEADS // 2),
        in_specs=[
            pl.BlockSpec((S, 2 * LANES), lambda b, g: (b, g)),
            pl.BlockSpec((S, 2 * LANES), lambda b, g: (b, g)),
            pl.BlockSpec((S, LANES), lambda b, g: (b, 0)),
            _resident((LANES, LANES)), tab, tab, tab, tab,
        ],
        out_specs=pl.BlockSpec((S, LANES), lambda b, g: (b, g)),
        out_shape=jax.ShapeDtypeStruct((B * S, MLA_HEADS * MLA_V), BF16),
        scratch_shapes=[
            pltpu.VMEM((2, S, LANES), BF16),
            pltpu.VMEM((2, S, LANES), BF16),
            pltpu.VMEM((2, S, LANES), BF16),
            pltpu.VMEM((tq, S), F32),
            pltpu.VMEM((tq, S), F32),
            pltpu.VMEM((tq, 1), F32),
            pltpu.VMEM((tq, 1), F32),
        ],
        compiler_params=_cparams("parallel", "parallel"),
    )(q, kv, kr, *tabs)


def _swa_attn_kernel(sink_ref, q_ref, k_ref, v_ref, bias_ref, half_ref, swap_ref, gq_ref, gk_ref, o_ref,
                     q_scr, k_scr, v_scr, s0_scr, s1_scr, *, scale):
    S = k_ref.shape[0]
    span = BLOCK_Q + 2 * SWA_WINDOW
    R = SWA_HEADS // SWA_KV_HEADS
    rows_u = R * BLOCK_Q
    s_bufs = (s0_scr, s1_scr)
    pid = pl.program_id(0)
    lane = lax.broadcasted_iota(jnp.int32, (1, LANES), 1)
    lo = lane < SWA_HEAD_DIM
    hi = jnp.logical_not(lo)

    def seg_norm(x, gain):
        xf = x.astype(F32)
        sq = xf * xf
        sq_hi = sq.astype(BF16)
        sq_lo = (sq - sq_hi.astype(F32)).astype(BF16)
        ss = _dot(sq_hi, half_ref[...]) + _dot(sq_lo, half_ref[...])
        return xf * lax.rsqrt(ss * (1.0 / SWA_HEAD_DIM) + EPS) * gain

    k_scr[...] = seg_norm(k_ref[...], gk_ref[...]).astype(BF16)
    v = v_ref[...].astype(F32)
    v_scr[0] = jnp.where(lo, v, 1.0).astype(BF16)
    v_scr[1] = jnp.where(lo, 1.0, v).astype(BF16)
    for pb in range(R):
        e = pb // (R // 2)
        keep = lo if e == 0 else hi
        qp = seg_norm(q_ref[:, pb * LANES:(pb + 1) * LANES], gq_ref[...]) * (scale * LOG2E)
        qr = _dot(qp.astype(BF16), swap_ref[...])
        for i in range(2):
            qz = jnp.where(keep, qp if i == e else qr, 0.0).astype(BF16)
            r = (pb % (R // 2)) * 2 + i
            for j in range(S // BLOCK_Q):
                q_scr[e, j, r * BLOCK_Q:(r + 1) * BLOCK_Q, :] = qz[j * BLOCK_Q:(j + 1) * BLOCK_Q, :]

    head_of_row = lax.broadcasted_iota(jnp.int32, (rows_u, 1), 0) // BLOCK_Q

    def head_column(ref, e):
        col = jnp.zeros((rows_u, 1), F32)
        for r in range(R):
            col = jnp.where(head_of_row == r, ref[pid * 2 * R + e * R + r], col)
        return col

    sink_cols = [head_column(sink_ref, e) for e in range(2)]

    def rows(j):
        return pl.ds(j * BLOCK_Q, BLOCK_Q)

    def window(j):
        return min(max(j * BLOCK_Q - SWA_WINDOW, 0), S - span)

    def scores(j, e):
        start = window(j)
        bias = bias_ref[e, (j * BLOCK_Q - start) // SWA_WINDOW]
        _store_scores(_dot_nt(q_scr[e, j], k_scr[pl.ds(start, span), :]) + bias, s_bufs[e])

    def finish(j, e):
        p, m = _softmax_numerators(s_bufs[e], extra_logit=sink_cols[e])
        o = _dot(p, v_scr[e, pl.ds(window(j), span), :])
        den = pltpu.roll(o, SWA_HEAD_DIM, 1) + jnp.exp2(sink_cols[e] - m)
        return o / den

    def emit(j, o0, o1):
        for e, o in ((0, o0), (1, o1)):
            orot = pltpu.roll(o, SWA_HEAD_DIM, 1)
            for k in range(R // 2):
                pb = e * (R // 2) + k
                even = (o if e == 0 else orot)[2 * k * BLOCK_Q:(2 * k + 1) * BLOCK_Q]
                odd = (o if e == 1 else orot)[(2 * k + 1) * BLOCK_Q:(2 * k + 2) * BLOCK_Q]
                o_ref[rows(j), pb * LANES:(pb + 1) * LANES] = jnp.where(lo, even, odd).astype(o_ref.dtype)

    _two_unit_pipeline(S // BLOCK_Q, scores, finish, emit)


def swa_attention(qkv, slopes, sink, gq, gk, B, S):
    n_steps = SWA_KV_HEADS // 2
    R = SWA_HEADS // SWA_KV_HEADS
    span = BLOCK_Q + 2 * SWA_WINDOW
    qw = SWA_HEADS * SWA_HEAD_DIM // n_steps
    kbase = SWA_HEADS * SWA_HEAD_DIM // LANES
    smem = pl.BlockSpec(memory_space=pltpu.SMEM)
    gain = pl.BlockSpec((1, LANES), lambda g, b: (0, 0))
    kern = functools.partial(_swa_attn_kernel, scale=SWA_HEAD_DIM ** -0.5)
    lane = np.arange(LANES)
    half_ones = jnp.asarray(lane[:, None] // SWA_HEAD_DIM == lane[None, :] // SWA_HEAD_DIM, dtype=BF16)
    swap = _partner_matrix((lane + SWA_HEAD_DIM) % LANES)
    t_s = np.arange(BLOCK_Q)[:, None] - np.arange(span)[None, :]
    dist = np.abs(np.stack([t_s + c * SWA_WINDOW for c in range(3)]))
    bias = jnp.where(jnp.asarray(dist <= SWA_WINDOW)[None], -slopes[:, None, None, None] * jnp.asarray(dist, F32)[None],
                     -jnp.inf)
    bias = bias.reshape(n_steps, 2, R, 3, BLOCK_Q, span).transpose(0, 1, 3, 2, 4, 5)
    bias = bias.reshape(n_steps, 2, 3, R * BLOCK_Q, span)
    return pl.pallas_call(
        kern,
        name="swa_attn",
        grid=(n_steps, B),
        in_specs=[
            smem,
            pl.BlockSpec((S, qw), lambda g, b: (b, g)),
            pl.BlockSpec((S, LANES), lambda g, b: (b, kbase + g)),
            pl.BlockSpec((S, LANES), lambda g, b: (b, kbase + n_steps + g)),
            pl.BlockSpec((None, 2, 3, R * BLOCK_Q, span), lambda g, b: (g, 0, 0, 0, 0)),
            _resident((LANES, LANES)), _resident((LANES, LANES)),
            gain, gain,
        ],
        out_specs=pl.BlockSpec((S, qw), lambda g, b: (b, g)),
        out_shape=jax.ShapeDtypeStruct((B * S, SWA_HEADS * SWA_HEAD_DIM), BF16),
        scratch_shapes=[
            pltpu.VMEM((2, S // BLOCK_Q, R * BLOCK_Q, LANES), BF16),
            pltpu.VMEM((S, LANES), BF16),
            pltpu.VMEM((2, S, LANES), BF16),
            pltpu.VMEM((R * BLOCK_Q, span), F32),
            pltpu.VMEM((R * BLOCK_Q, span), F32),
        ],
        compiler_params=_cparams("parallel", "parallel"),
    )(sink, qkv, qkv, qkv, bias, half_ones, swap, gq, gk)


def _swiglu_accumulate(h, wg_ref, wu_ref, wd_ref, acc_ref):
    for c in range(wg_ref.shape[1] // FFN_SUB):
        sl = slice(c * FFN_SUB, (c + 1) * FFN_SUB)
        g = _dot(h, wg_ref[:, sl])
        u = _dot(h, wu_ref[:, sl])
        a = (g * jax.nn.sigmoid(g) * u).astype(BF16)
        acc_ref[...] += _dot(a, wd_ref[sl, :])


def _ple(x, p, gain, win_ref, wgate_ref):
    gate = jax.nn.sigmoid(_dot(_rms(x, gain).astype(BF16), wgate_ref[...]))
    return x + _dot(p.astype(BF16), win_ref[...]) * gate


def _dense_tail_kernel(x_ref, a_ref, wo_ref, g_ref, wg_ref, wu_ref, wd_ref, p_ref, gp_ref, win_ref, wgate_ref, o_ref,
                       acc_scr):
    x = x_ref[...] + _dot(a_ref[...], wo_ref[...])
    acc_scr[...] = x
    _swiglu_accumulate(_rms(x, g_ref[...]).astype(BF16), wg_ref, wu_ref, wd_ref, acc_scr)
    o_ref[...] = _ple(acc_scr[...], p_ref[...], gp_ref[...], win_ref, wgate_ref)


def dense_layer_tail(x, a, w_o, gain, wg, wu, wd, p, layer, ple_gain, w_in, w_gate):
    T, D = x.shape
    K = a.shape[1]
    Fd = wg.shape[1]
    P = p.shape[2]
    row = lambda n: pl.BlockSpec((ROW_TILE, n), lambda i: (i, 0))
    return pl.pallas_call(
        _dense_tail_kernel,
        name="dense_tail",
        grid=(T // ROW_TILE,),
        in_specs=[
            row(D), row(K), _resident((K, D)),
            _resident((1, D)), _resident((D, Fd)), _resident((D, Fd)), _resident((Fd, D)),
            pl.BlockSpec((None, ROW_TILE, P), lambda i: (layer, i, 0)),
            _resident((1, D)), _resident((P, D)), _resident((D, D)),
        ],
        out_specs=row(D),
        out_shape=jax.ShapeDtypeStruct((T, D), F32),
        scratch_shapes=[pltpu.VMEM((ROW_TILE, D), F32)],
        compiler_params=_cparams("parallel"),
    )(x, a, w_o, gain.reshape(1, D), wg, wu, wd, p, ple_gain.reshape(1, D), w_in, w_gate)


def _router_kernel(x_ref, a_ref, wo_ref, g_ref, whi_ref, wlo_ref, b_ref, x1_ref, h_ref, idx_ref, wt_ref, cnt_ref):
    x1 = x_ref[...] + _dot(a_ref[...], wo_ref[...])
    x1_ref[...] = x1
    hf = _rms(x1, g_ref[...])
    h_hi = hf.astype(BF16)
    h_lo = (hf - h_hi.astype(F32)).astype(BF16)
    h_ref[...] = h_hi
    logits = _dot(h_hi, whi_ref[...]) + _dot(h_hi, wlo_ref[...]) + _dot(h_lo, whi_ref[...]) + b_ref[...]
    lane = lax.broadcasted_iota(jnp.int32, logits.shape, 1)
    logits = jnp.where(lane < N_EXPERTS, logits, -jnp.inf)
    m1 = jnp.max(logits, axis=-1, keepdims=True)
    i1 = jnp.min(jnp.where(logits == m1, lane, LANES), axis=-1, keepdims=True)
    rest = jnp.where(lane == i1, -jnp.inf, logits)
    m2 = jnp.max(rest, axis=-1, keepdims=True)
    i2 = jnp.min(jnp.where(rest == m2, lane, LANES), axis=-1, keepdims=True)
    e2 = jnp.exp(m2 - m1)
    w1 = 1.0 / (1.0 + e2)
    w2 = e2 / (1.0 + e2)
    wt_ref[...] = jnp.where(lane == 0, w1, jnp.where(lane == 1, w2, 0.0))
    onehot = jnp.where(jnp.logical_or(lane == i1, lane == i2), 1.0, 0.0)
    tm = onehot.shape[0]
    earlier = (lax.broadcasted_iota(jnp.int32, (tm, tm), 0) > lax.broadcasted_iota(jnp.int32, (tm, tm), 1))
    prefix = _dot(jnp.where(earlier, 1.0, 0.0).astype(BF16), onehot.astype(BF16))
    r1 = jnp.sum(jnp.where(lane == i1, prefix, 0.0), axis=-1, keepdims=True).astype(jnp.int32)
    r2 = jnp.sum(jnp.where(lane == i2, prefix, 0.0), axis=-1, keepdims=True).astype(jnp.int32)
    idx = jnp.where(lane == 0, i1, jnp.where(lane == 1, i2, jnp.where(lane == 2, r1, jnp.where(lane == 3, r2, 0))))
    idx_ref[...] = idx.T[:SUBLANES, :]
    cnt_ref[...] = jnp.broadcast_to(jnp.sum(onehot, axis=0, keepdims=True), cnt_ref.shape)


def moe_router(x, a, w_o, gain, w_hi, w_lo, bias):
    T, D = x.shape
    K = a.shape[1]
    row = lambda n: pl.BlockSpec((ROW_TILE, n), lambda i: (i, 0))
    return pl.pallas_call(
        _router_kernel,
        name="moe_router",
        grid=(T // ROW_TILE,),
        in_specs=[row(D), row(K), _resident((K, D)), _resident((1, D)), _resident((D, LANES)), _resident((D, LANES)),
                  _resident((1, LANES))],
        out_specs=[row(D), row(D), pl.BlockSpec((SUBLANES, ROW_TILE), lambda i: (i, 0)), row(LANES),
                   pl.BlockSpec((SUBLANES, LANES), lambda i: (i, 0))],
        out_shape=[
            jax.ShapeDtypeStruct((T, D), F32),
            jax.ShapeDtypeStruct((T, D), BF16),
            jax.ShapeDtypeStruct((T // ROW_TILE * SUBLANES, ROW_TILE), jnp.int32),
            jax.ShapeDtypeStruct((T, LANES), F32),
            jax.ShapeDtypeStruct((T // ROW_TILE * SUBLANES, LANES), F32),
        ],
        compiler_params=_cparams("parallel"),
    )(x, a, w_o, gain.reshape(1, D), w_hi, w_lo, bias)


def _moe_dispatch_kernel(n_ref, base_ref, h_ref, idx_ref, zeros_ref, hs_ref, dest_ref, loc_scr, sem):
    del zeros_ref
    i = pl.program_id(0)
    e0, e1, r0, r1 = (idx_ref[k:k + 1, :] for k in range(4))
    slot0, slot1, dest0, dest1 = r0, r1, r0, r1
    offs = []
    off = jnp.int32(0)
    for e in range(N_EXPERTS):
        offs.append(off)
        base = base_ref[i * N_EXPERTS + e]
        slot0 = slot0 + jnp.where(e0 == e, off, 0)
        slot1 = slot1 + jnp.where(e1 == e, off, 0)
        dest0 = dest0 + jnp.where(e0 == e, base, 0)
        dest1 = dest1 + jnp.where(e1 == e, base, 0)
        off = off + n_ref[i * N_EXPERTS + e]
    row = lax.broadcasted_iota(jnp.int32, (SUBLANES, e0.shape[1]), 0)
    dest_ref[...] = jnp.where(row == 0, dest0, jnp.where(row == 1, dest1, 0))
    slot = lax.broadcasted_iota(jnp.int32, (loc_scr.shape[0], e0.shape[1]), 0)
    perm = jnp.where(jnp.logical_or(slot == slot0, slot == slot1), 1.0, 0.0).astype(BF16)
    loc_scr[...] = _dot(perm, h_ref[...]).astype(BF16)

    def piece(src_row, dst_row):
        return pltpu.make_async_copy(loc_scr.at[pl.ds(src_row, CHUNK_ALIGN), :],
                                     hs_ref.at[pl.ds(dst_row, CHUNK_ALIGN), :], sem)

    for e in range(N_EXPERTS):
        base = base_ref[i * N_EXPERTS + e]

        def start(g, carry, e=e, base=base):
            piece(pl.multiple_of(offs[e] + g * CHUNK_ALIGN, CHUNK_ALIGN),
                  pl.multiple_of(base + g * CHUNK_ALIGN, CHUNK_ALIGN)).start()
            return carry

        lax.fori_loop(0, n_ref[i * N_EXPERTS + e] // CHUNK_ALIGN, start, 0)

    def wait(g, carry):
        piece(0, 0).wait()
        return carry

    lax.fori_loop(0, off // CHUNK_ALIGN, wait, 0)


def moe_dispatch(h, idx, chunk_rows, chunk_base, n_rows):
    T, D = h.shape
    n_rt = T // ROW_TILE
    loc_rows = TOP_K * ROW_TILE + N_EXPERTS * CHUNK_ALIGN
    grid_spec = pltpu.PrefetchScalarGridSpec(
        num_scalar_prefetch=2,
        grid=(n_rt,),
        in_specs=[
            pl.BlockSpec((ROW_TILE, D), lambda i, n, b: (i, 0)),
            pl.BlockSpec((SUBLANES, ROW_TILE), lambda i, n, b: (i, 0)),
            pl.BlockSpec(memory_space=pl.ANY),
        ],
        out_specs=[
            pl.BlockSpec(memory_space=pl.ANY),
            pl.BlockSpec((SUBLANES, ROW_TILE), lambda i, n, b: (i, 0)),
        ],
        scratch_shapes=[pltpu.VMEM((loc_rows, D), BF16), pltpu.SemaphoreType.DMA],
    )
    return pl.pallas_call(
        _moe_dispatch_kernel,
        name="moe_dispatch",
        grid_spec=grid_spec,
        out_shape=[
            jax.ShapeDtypeStruct((n_rows, D), BF16),
            jax.ShapeDtypeStruct((n_rt * SUBLANES, ROW_TILE), jnp.int32),
        ],
        input_output_aliases={4: 0},
        compiler_params=_cparams("arbitrary"),
    )(chunk_rows, chunk_base, h, idx, jnp.zeros((n_rows, D), BF16))


def _moe_ffn_kernel(te_ref, tv_ref, h_ref, wg_ref, wu_ref, wd_ref, o_ref, acc_scr):
    i = pl.program_id(0)
    f = pl.program_id(1)
    last = pl.num_programs(1) - 1

    @pl.when(f == 0)
    def _():
        acc_scr[...] = jnp.zeros_like(acc_scr)

    @pl.when(tv_ref[i] > 0)
    def _():
        _swiglu_accumulate(h_ref[...], wg_ref, wu_ref, wd_ref, acc_scr)

    @pl.when(f == last)
    def _():
        o_ref[...] = acc_scr[...].astype(o_ref.dtype)


def moe_ffn(h_sorted, tile_expert, tile_valid, wg, wu, wd, layer, n_chunks):
    R, D = h_sorted.shape
    Fe = wg.shape[3]
    tf = Fe // n_chunks

    def chunk(i, f, tv):
        return jnp.where(tv[i] > 0, f, n_chunks - 1)

    grid_spec = pltpu.PrefetchScalarGridSpec(
        num_scalar_prefetch=2,
        grid=(R // MOE_ROW_TILE, n_chunks),
        in_specs=[
            pl.BlockSpec((MOE_ROW_TILE, D), lambda i, f, te, tv: (i, 0)),
            pl.BlockSpec((None, None, D, tf), lambda i, f, te, tv: (layer, te[i], 0, chunk(i, f, tv))),
            pl.BlockSpec((None, None, D, tf), lambda i, f, te, tv: (layer, te[i], 0, chunk(i, f, tv))),
            pl.BlockSpec((None, None, tf, D), lambda i, f, te, tv: (layer, te[i], chunk(i, f, tv), 0)),
        ],
        out_specs=pl.BlockSpec((MOE_ROW_TILE, D), lambda i, f, te, tv: (i, 0)),
        scratch_shapes=[pltpu.VMEM((MOE_ROW_TILE, D), F32)],
    )
    return pl.pallas_call(
        _moe_ffn_kernel,
        name="moe_ffn",
        grid_spec=grid_spec,
        out_shape=jax.ShapeDtypeStruct((R, D), BF16),
        compiler_params=_cparams("parallel", "arbitrary"),
    )(tile_expert, tile_valid, h_sorted, wg, wu, wd)


def _moe_tail_kernel(x_ref, y0_ref, y1_ref, wt_ref, p_ref, g_ref, win_ref, wgate_ref, o_ref):
    wt = wt_ref[...]
    x = x_ref[...] + wt[:, 0:1] * y0_ref[...].astype(F32) + wt[:, 1:2] * y1_ref[...].astype(F32)
    o_ref[...] = _ple(x, p_ref[...], g_ref[...], win_ref, wgate_ref)


def moe_layer_tail(x, y0, y1, wt, p, layer, gain, w_in, w_gate):
    T, D = x.shape
    P = p.shape[2]
    row = lambda n: pl.BlockSpec((ROW_TILE, n), lambda i: (i, 0))
    return pl.pallas_call(
        _moe_tail_kernel,
        name="moe_tail",
        grid=(T // ROW_TILE,),
        in_specs=[row(D), row(D), row(D), row(LANES), pl.BlockSpec((None, ROW_TILE, P), lambda i: (layer, i, 0)),
                  _resident((1, D)), _resident((P, D)), _resident((D, D))],
        out_specs=row(D),
        out_shape=jax.ShapeDtypeStruct((T, D), F32),
        compiler_params=_cparams("parallel"),
    )(x, y0, y1, wt, p, gain.reshape(1, D), w_in, w_gate)


def _rope_cos_sin(pos, dim, theta):
    inv = theta ** (-jnp.arange(0, dim, 2, dtype=F32) / dim)
    ang = pos.astype(F32)[:, None] * inv[None, :]
    return jnp.cos(ang), jnp.sin(ang)


def _fold_tables(gain_lanes, cos_lanes, sin_lanes, partner):
    return gain_lanes[None, :] * cos_lanes, gain_lanes[partner][None, :] * sin_lanes


def _partner_matrix(partner):
    m = np.zeros((LANES, LANES), np.float32)
    m[partner, np.arange(LANES)] = 1.0
    return jnp.asarray(m, dtype=BF16)


def _axial_tables(S, q_gain, k_gain):
    pos = jnp.arange(S)
    cr, sr = _rope_cos_sin(pos // GRID_W, AX_HEAD_DIM // 2, AX_THETA)
    cc, sc = _rope_cos_sin(pos % GRID_W, AX_HEAD_DIM // 2, AX_THETA)
    cos = jnp.concatenate([cr, cr, cc, cc], axis=1)
    sin = jnp.concatenate([-sr, sr, -sc, sc], axis=1)
    lane = np.arange(LANES)
    partner = np.where(lane % 64 < 32, lane + 32, lane - 32)
    return (_partner_matrix(partner),) + _fold_tables(q_gain, cos, sin, partner) + _fold_tables(k_gain, cos, sin, partner)


def _mla_tables(S, q_gain, k_gain):
    c, s = _rope_cos_sin(jnp.arange(S), MLA_ROPE, MLA_THETA)
    pad = LANES - MLA_QK
    cos = jnp.concatenate([jnp.ones((S, MLA_NOPE), F32), c, c, jnp.ones((S, pad), F32)], axis=1)
    sin = jnp.concatenate([jnp.zeros((S, MLA_NOPE), F32), -s, s, jnp.zeros((S, pad), F32)], axis=1)
    lane = np.arange(LANES)
    half = MLA_ROPE // 2
    partner = np.where((lane >= MLA_NOPE) & (lane < MLA_NOPE + half), lane + half,
                       np.where((lane >= MLA_NOPE + half) & (lane < MLA_QK), lane - half, lane))
    zpad = jnp.zeros((pad,), F32)
    gq = jnp.concatenate([q_gain, zpad])
    gk = jnp.concatenate([k_gain, zpad])
    return (_partner_matrix(partner),) + _fold_tables(gq, cos, sin, partner) + _fold_tables(gk, cos, sin, partner)


def _moe_layout(cnt, n_rt, n_tiles):
    tm = MOE_ROW_TILE
    cnt = cnt.reshape(n_rt, SUBLANES, LANES)[:, 0, :N_EXPERTS].astype(jnp.int32)
    chunk_rows = ((cnt + CHUNK_ALIGN - 1) // CHUNK_ALIGN) * CHUNK_ALIGN
    rt = jnp.arange(n_rt)
    before = jnp.sum(jnp.where((rt[None, :] < rt[:, None])[:, :, None], chunk_rows[None, :, :], 0), axis=1)
    region = ((jnp.sum(chunk_rows, axis=0) + tm - 1) // tm) * tm
    ex = jnp.arange(N_EXPERTS)
    ends = jnp.sum(jnp.where(ex[None, :] <= ex[:, None], region[None, :], 0), axis=1)
    chunk_base = (ends - region)[None, :] + before
    tile_start = jnp.arange(n_tiles, dtype=jnp.int32) * tm
    tile_expert = jnp.minimum(jnp.sum((tile_start[:, None] >= ends[None, :]).astype(jnp.int32), axis=1), N_EXPERTS - 1)
    tile_valid = (tile_start < ends[-1]).astype(jnp.int32)
    return chunk_rows.reshape(-1), chunk_base.reshape(-1), tile_expert, tile_valid


def kernel(x, p, attn_norm, ffn_norm, ple_norm, ple_w_in, ple_w_gate, mla_w_down, mla_q_norm, mla_w_uq, mla_kv_norm, mla_w_ukv, mla_q_gain, mla_k_gain, mla_w_o, swa_w_qkv, swa_q_gain, swa_k_gain, swa_sink, swa_w_o, ax_w_qkv, ax_q_gain, ax_k_gain, ax_w_o, ffn_w_gate, ffn_w_up, ffn_w_down, moe_w_router, moe_b_router, moe_w_gate, moe_w_up, moe_w_down):
    B, S, D = x.shape
    depth = p.shape[0]
    T = B * S
    xt = x.reshape(T, D)
    bf = lambda a: a.astype(BF16)
    n_rt = T // ROW_TILE
    n_moe_tiles = (TOP_K * T + n_rt * N_EXPERTS * (CHUNK_ALIGN - 1)) // MOE_ROW_TILE + N_EXPERTS
    slopes = jnp.asarray(2.0 ** (-8.0 * np.arange(1, SWA_HEADS + 1) / SWA_HEADS) * LOG2E, dtype=F32)
    p3 = p.reshape(depth, T, -1)
    moe_wg, moe_wu, moe_wd = bf(moe_w_gate), bf(moe_w_up), bf(moe_w_down)

    for i in range(depth):
        kind = i % N_MIXERS
        j = i // N_MIXERS
        if kind == 0:
            wd = mla_w_down[j]
            zc = lambda n: jnp.zeros((D, n), F32)
            wd = jnp.concatenate([wd[:, :MLA_Q_RANK + MLA_KV_RANK], zc(MLA_NOPE), wd[:, MLA_Q_RANK + MLA_KV_RANK:],
                                  zc(LANES - MLA_QK)], axis=1)
            wuq = mla_w_uq[j].reshape(MLA_Q_RANK, MLA_HEADS, MLA_QK)
            wuq = jnp.pad(wuq, ((0, 0), (0, 0), (0, LANES - MLA_QK))).reshape(MLA_Q_RANK, MLA_HEADS * LANES)
            q, kv, kr = mla_proj(xt, attn_norm[i], bf(wd), mla_q_norm[j], mla_kv_norm[j], bf(wuq), bf(mla_w_ukv[j]))
            o = mla_attention(q, kv, kr, _mla_tables(S, mla_q_gain[j], mla_k_gain[j]), B, S)
            w_o = mla_w_o[j]
        elif kind == 1:
            qkv = norm_proj(xt, attn_norm[i], bf(swa_w_qkv[j]))
            gq = jnp.tile(swa_q_gain[j], 2).reshape(1, LANES)
            gk = jnp.tile(swa_k_gain[j], 2).reshape(1, LANES)
            o = swa_attention(qkv, slopes, swa_sink[j].astype(F32) * LOG2E, gq, gk, B, S)
            w_o = swa_w_o[j]
        else:
            qkv = norm_proj(xt, attn_norm[i], bf(ax_w_qkv[j]))
            o = axial_attention(qkv, _axial_tables(S, ax_q_gain[j], ax_k_gain[j]), B, S)
            w_o = ax_w_o[j]
        f = i // 2
        if i % 2 == 0:
            xt = dense_layer_tail(xt, o, bf(w_o), ffn_norm[i], bf(ffn_w_gate[f]), bf(ffn_w_up[f]), bf(ffn_w_down[f]),
                                  p3, i, ple_norm[i], bf(ple_w_in[i]), bf(ple_w_gate[i]))
        else:
            wr = jnp.pad(moe_w_router[f], ((0, 0), (0, LANES - N_EXPERTS)))
            wr_hi = bf(wr)
            wr_lo = bf(wr - wr_hi.astype(F32))
            br = jnp.pad(moe_b_router[f].astype(F32), (0, LANES - N_EXPERTS)).reshape(1, LANES)
            xt, h, idx, wt, cnt = moe_router(xt, o, bf(w_o), ffn_norm[i], wr_hi, wr_lo, br)
            chunk_rows, chunk_base, tile_expert, tile_valid = _moe_layout(cnt, n_rt, n_moe_tiles)
            h_sorted, dest = moe_dispatch(h, idx, chunk_rows, chunk_base, n_moe_tiles * MOE_ROW_TILE)
            y = moe_ffn(h_sorted, tile_expert, tile_valid, moe_wg, moe_wu, moe_wd, f, 2)
            dest = dest.reshape(n_rt, SUBLANES, ROW_TILE)
            y0 = jnp.take(y, dest[:, 0, :].reshape(T), axis=0, mode="clip")
            y1 = jnp.take(y, dest[:, 1, :].reshape(T), axis=0, mode="clip")
            xt = moe_layer_tail(xt, y0, y1, wt, p3, i, ple_norm[i], bf(ple_w_in[i]), bf(ple_w_gate[i]))
    return xt.reshape(B, S, D)
```

```python
import functools

import numpy as np
import jax
import jax.numpy as jnp
from jax import lax
from jax.experimental import pallas as pl
from jax.experimental.pallas import tpu as pltpu

F32 = jnp.float32
BF16 = jnp.bfloat16

EPS = 1e-6
GRID_W = 64
BLOCK_Q = 128

MLA_HEADS = 16
MLA_NOPE = 64
MLA_ROPE = 32
MLA_V = 64
MLA_Q_RANK = 256
MLA_KV_RANK = 128
MLA_THETA = 10000.0
MLA_QK = MLA_NOPE + MLA_ROPE

SWA_HEADS = 16
SWA_KV_HEADS = 4
SWA_HEAD_DIM = 64
SWA_WINDOW = 128

AX_HEADS = 8
AX_KV_HEADS = 4
AX_HEAD_DIM = 128
AX_THETA = 10000.0

N_EXPERTS = 8
TOP_K = 2
N_MIXERS = 3

LANES = 128
SUBLANES = 8
ROW_TILE = 512
MOE_ROW_TILE = 512
CHUNK_ALIGN = 2 * SUBLANES
ATTN_Q_TILE = 256
FFN_SUB = 256
VMEM_LIMIT = 56 * 1024 * 1024
LOG2E = 1.4426950408889634


def _cparams(*sem):
    return pltpu.CompilerParams(dimension_semantics=sem, vmem_limit_bytes=VMEM_LIMIT)


def _rms(xf, gain):
    ms = jnp.mean(xf * xf, axis=-1, keepdims=True)
    return xf * lax.rsqrt(ms + EPS) * gain


def _dot(a, b):
    return jnp.dot(a, b, preferred_element_type=F32)


def _dot_nt(a, b):
    return lax.dot_general(a, b, (((1,), (1,)), ((), ())), preferred_element_type=F32)


def _resident(shape):
    return pl.BlockSpec(shape, lambda *_: (0,) * len(shape), pipeline_mode=pl.Buffered(1))


def _norm_proj_kernel(x_ref, g_ref, w_ref, o_ref):
    h = _rms(x_ref[...], g_ref[...]).astype(BF16)
    o_ref[...] = _dot(h, w_ref[...]).astype(o_ref.dtype)


def norm_proj(x, gain, w):
    T, D = x.shape
    N = w.shape[1]
    return pl.pallas_call(
        _norm_proj_kernel,
        name="norm_proj",
        grid=(T // ROW_TILE,),
        in_specs=[
            pl.BlockSpec((ROW_TILE, D), lambda i: (i, 0)),
            _resident((1, D)),
            _resident((D, N)),
        ],
        out_specs=pl.BlockSpec((ROW_TILE, N), lambda i: (i, 0)),
        out_shape=jax.ShapeDtypeStruct((T, N), BF16),
        compiler_params=_cparams("parallel"),
    )(x, gain.reshape(1, D), w)


def _mla_proj_kernel(x_ref, g_ref, wd_ref, qn_ref, kvn_ref, wuq_ref, wukv_ref, q_ref, kv_ref, kr_ref):
    h = _rms(x_ref[...], g_ref[...]).astype(BF16)
    down = _dot(h, wd_ref[...])
    cq = _rms(down[:, :MLA_Q_RANK], qn_ref[...]).astype(BF16)
    ckv = _rms(down[:, MLA_Q_RANK:MLA_Q_RANK + MLA_KV_RANK], kvn_ref[...]).astype(BF16)
    q_ref[...] = _dot(cq, wuq_ref[...]).astype(BF16)
    kv_ref[...] = _dot(ckv, wukv_ref[...]).astype(BF16)
    kr_ref[...] = down[:, MLA_Q_RANK + MLA_KV_RANK:].astype(BF16)


def mla_proj(x, gain, wd, qn, kvn, wuq, wukv):
    T, D = x.shape
    nd = wd.shape[1]
    nq = wuq.shape[1]
    nkv = wukv.shape[1]
    row = lambda n: pl.BlockSpec((ROW_TILE, n), lambda i: (i, 0))
    return pl.pallas_call(
        _mla_proj_kernel,
        name="mla_proj",
        grid=(T // ROW_TILE,),
        in_specs=[
            row(D),
            _resident((1, D)),
            _resident((D, nd)),
            _resident((1, MLA_Q_RANK)),
            _resident((1, MLA_KV_RANK)),
            _resident((MLA_Q_RANK, nq)),
            _resident((MLA_KV_RANK, nkv)),
        ],
        out_specs=[row(nq), row(nkv), row(LANES)],
        out_shape=[
            jax.ShapeDtypeStruct((T, nq), BF16),
            jax.ShapeDtypeStruct((T, nkv), BF16),
            jax.ShapeDtypeStruct((T, LANES), BF16),
        ],
        compiler_params=_cparams("parallel"),
    )(x, gain.reshape(1, D), wd, qn.reshape(1, -1), kvn.reshape(1, -1), wuq, wukv)


def _two_unit_pipeline(n, scores, finish, emit):
    scores(0, 0)
    for i in range(n):
        scores(i, 1)
        o0 = finish(i, 0)
        o1 = finish(i, 1)
        if i + 1 < n:
            scores(i + 1, 0)
        emit(i, o0, o1)


def _store_scores(s, s_buf, m_buf=None):
    s_buf[...] = s
    if m_buf is not None:
        m_buf[...] = jnp.max(s, axis=-1, keepdims=True)


def _softmax_numerators(s_buf, m_buf=None, extra_logit=None):
    m = m_buf[...] if m_buf is not None else jnp.max(s_buf[...], axis=-1, keepdims=True)
    if extra_logit is not None:
        m = jnp.maximum(m, extra_logit)
    return jnp.exp2(s_buf[...] - m).astype(BF16), m


def _norm_rope(x, perm_ref, a, b, inv_dim, extra):
    xf = x.astype(F32)
    c = lax.rsqrt(jnp.sum(xf * xf, axis=-1, keepdims=True) * inv_dim + EPS) * extra
    return (xf * a + _dot(x, perm_ref[...]) * b) * c


def _axial_attn_kernel(q_ref, k_ref, v_ref, perm_ref, aq_ref, bq_ref, ak_ref, bk_ref, o_ref, q_scr, k_scr, v_scr,
                       s0_scr, s1_scr, *, tq, scale):
    S = k_ref.shape[0]
    s_bufs = (s0_scr, s1_scr)
    inv_dim = 1.0 / AX_HEAD_DIM

    k_scr[...] = _norm_rope(k_ref[...], perm_ref, ak_ref[...], bk_ref[...], inv_dim, 1.0).astype(BF16)
    for u in range(2):
        q_scr[u] = _norm_rope(q_ref[:, u * LANES:(u + 1) * LANES], perm_ref, aq_ref[...], bq_ref[...], inv_dim,
                              scale * LOG2E).astype(BF16)
    v_scr[:, :LANES] = v_ref[...]
    v_scr[:, LANES:] = jnp.ones((S, LANES), BF16)

    def rows(i):
        return pl.ds(i * tq, tq)

    def scores(i, u):
        _store_scores(_dot_nt(q_scr[u, rows(i), :], k_scr[...]), s_bufs[u])

    def finish(i, u):
        p, _ = _softmax_numerators(s_bufs[u])
        o = _dot(p, v_scr[...])
        return o[:, :LANES] / o[:, LANES:]

    def emit(i, o0, o1):
        o_ref[rows(i), :LANES] = o0.astype(o_ref.dtype)
        o_ref[rows(i), LANES:] = o1.astype(o_ref.dtype)

    _two_unit_pipeline(S // tq, scores, finish, emit)


def axial_attention(qkv, tabs, B, S):
    R = AX_HEADS // AX_KV_HEADS
    assert R == 2
    tq = min(ATTN_Q_TILE, S)
    kern = functools.partial(_axial_attn_kernel, tq=tq, scale=AX_HEAD_DIM ** -0.5)
    tab = pl.BlockSpec((S, LANES), lambda b, g: (0, 0), pipeline_mode=pl.Buffered(1))
    return pl.pallas_call(
        kern,
        name="axial_attn",
        grid=(B, AX_KV_HEADS),
        in_specs=[
            pl.BlockSpec((S, R * LANES), lambda b, g: (b, g)),
            pl.BlockSpec((S, LANES), lambda b, g: (b, AX_HEADS + g)),
            pl.BlockSpec((S, LANES), lambda b, g: (b, AX_HEADS + AX_KV_HEADS + g)),
            _resident((LANES, LANES)), tab, tab, tab, tab,
        ],
        out_specs=pl.BlockSpec((S, R * LANES), lambda b, g: (b, g)),
        out_shape=jax.ShapeDtypeStruct((B * S, AX_HEADS * AX_HEAD_DIM), BF16),
        scratch_shapes=[
            pltpu.VMEM((R, S, LANES), BF16),
            pltpu.VMEM((S, LANES), BF16),
            pltpu.VMEM((S, 2 * LANES), BF16),
            pltpu.VMEM((tq, S), F32),
            pltpu.VMEM((tq, S), F32),
        ],
        compiler_params=_cparams("parallel", "parallel"),
    )(qkv, qkv, qkv, *tabs)


def _mla_attn_kernel(q_ref, kv_ref, kr_ref, perm_ref, aq_ref, bq_ref, ak_ref, bk_ref, o_ref, q_scr, k_scr, v_scr,
                     s0_scr, s1_scr, m0_scr, m1_scr, *, tq, scale):
    S = kv_ref.shape[0]
    s_bufs = (s0_scr, s1_scr)
    m_bufs = (m0_scr, m1_scr)
    lane = lax.broadcasted_iota(jnp.int32, (1, LANES), 1)
    lo = lane < MLA_NOPE
    inv_dim = 1.0 / MLA_QK

    kr = kr_ref[...]
    krf = kr.astype(F32)
    rope_part = krf * ak_ref[...] + _dot(kr, perm_ref[...]) * bk_ref[...]
    ss_rope = jnp.sum(krf * krf, axis=-1, keepdims=True)
    for hh in range(2):
        kvh = kv_ref[:, hh * LANES:(hh + 1) * LANES].astype(F32)
        ss = jnp.sum(jnp.where(lo, kvh * kvh, 0.0), axis=-1, keepdims=True) + ss_rope
        k = jnp.where(lo, kvh * ak_ref[...], rope_part)
        k_scr[hh] = (k * lax.rsqrt(ss * inv_dim + EPS)).astype(BF16)
        vh = jnp.where(lo, pltpu.roll(kvh, MLA_V, 1), 1.0) if hh == 0 else jnp.where(lo, 1.0, kvh)
        v_scr[hh] = vh.astype(BF16)
        q_scr[hh] = _norm_rope(q_ref[:, hh * LANES:(hh + 1) * LANES], perm_ref, aq_ref[...], bq_ref[...], inv_dim,
                               scale * LOG2E).astype(BF16)

    def rows(i):
        return pl.ds(i * tq, tq)

    def scores(i, u):
        _store_scores(_dot_nt(q_scr[u, rows(i), :], k_scr[u]), s_bufs[u], m_bufs[u])

    def finish(i, u):
        p, _ = _softmax_numerators(s_bufs[u], m_bufs[u])
        o = _dot(p, v_scr[u])
        return o / pltpu.roll(o, MLA_V, 1)

    def emit(i, o0, o1):
        o_ref[rows(i), :] = jnp.where(lo, o0, o1).astype(o_ref.dtype)

    _two_unit_pipeline(S // tq, scores, finish, emit)


def mla_attention(q, kv, kr, tabs, B, S):
    tq = min(ATTN_Q_TILE, S)
    kern = functools.partial(_mla_attn_kernel, tq=tq, scale=MLA_QK ** -0.5)
    tab = pl.BlockSpec((S, LANES), lambda b, g: (0, 0), pipeline_mode=pl.Buffered(1))
    return pl.pallas_call(
        kern,
        name="mla_attn",
        grid=(B, MLA_HEADS // 2),
        in_specs=[
            pl.BlockSpec((S, 2 * LANES), lambda b, g: (b, g)),
            pl.BlockSpec((S, 2 * LANES), lambda b, g: (b, g)),
            pl.BlockSpec((S, LANES), lambda b, g: (b, 0)),
            _resident((LANES, LANES)), tab, tab, tab, tab,
        ],
        out_specs=pl.BlockSpec((S, LANES), lambda b, g: (b, g)),
        out_shape=jax.ShapeDtypeStruct((B * S, MLA_HEADS * MLA_V), BF16),
        scratch_shapes=[
            pltpu.VMEM((2, S, LANES), BF16),
            pltpu.VMEM((2, S, LANES), BF16),
            pltpu.VMEM((2, S, LANES), BF16),
            pltpu.VMEM((tq, S), F32),
            pltpu.VMEM((tq, S), F32),
            pltpu.VMEM((tq, 1), F32),
            pltpu.VMEM((tq, 1), F32),
        ],
        compiler_params=_cparams("parallel", "parallel"),
    )(q, kv, kr, *tabs)


def _swa_attn_kernel(sink_ref, q_ref, k_ref, v_ref, bias_ref, half_ref, swap_ref, gq_ref, gk_ref, o_ref,
                     q_scr, k_scr, v_scr, s0_scr, s1_scr, *, scale):
    S = k_ref.shape[0]
    span = BLOCK_Q + 2 * SWA_WINDOW
    R = SWA_HEADS // SWA_KV_HEADS
    rows_u = R * BLOCK_Q
    s_bufs = (s0_scr, s1_scr)
    pid = pl.program_id(0)
    lane = lax.broadcasted_iota(jnp.int32, (1, LANES), 1)
    lo = lane < SWA_HEAD_DIM
    hi = jnp.logical_not(lo)

    def seg_norm(x, gain):
        xf = x.astype(F32)
        sq = xf * xf
        sq_hi = sq.astype(BF16)
        sq_lo = (sq - sq_hi.astype(F32)).astype(BF16)
        ss = _dot(sq_hi, half_ref[...]) + _dot(sq_lo, half_ref[...])
        return xf * lax.rsqrt(ss * (1.0 / SWA_HEAD_DIM) + EPS) * gain

    k_scr[...] = seg_norm(k_ref[...], gk_ref[...]).astype(BF16)
    v = v_ref[...].astype(F32)
    v_scr[0] = jnp.where(lo, v, 1.0).astype(BF16)
    v_scr[1] = jnp.where(lo, 1.0, v).astype(BF16)
    for pb in range(R):
        e = pb // (R // 2)
        keep = lo if e == 0 else hi
        qp = seg_norm(q_ref[:, pb * LANES:(pb + 1) * LANES], gq_ref[...]) * (scale * LOG2E)
        qr = _dot(qp.astype(BF16), swap_ref[...])
        for i in range(2):
            qz = jnp.where(keep, qp if i == e else qr, 0.0).astype(BF16)
            r = (pb % (R // 2)) * 2 + i
            for j in range(S // BLOCK_Q):
                q_scr[e, j, r * BLOCK_Q:(r + 1) * BLOCK_Q, :] = qz[j * BLOCK_Q:(j + 1) * BLOCK_Q, :]

    head_of_row = lax.broadcasted_iota(jnp.int32, (rows_u, 1), 0) // BLOCK_Q

    def head_column(ref, e):
        col = jnp.zeros((rows_u, 1), F32)
        for r in range(R):
            col = jnp.where(head_of_row == r, ref[pid * 2 * R + e * R + r], col)
        return col

    sink_cols = [head_column(sink_ref, e) for e in range(2)]

    def rows(j):
        return pl.ds(j * BLOCK_Q, BLOCK_Q)

    def window(j):
        return min(max(j * BLOCK_Q - SWA_WINDOW, 0), S - span)

    def scores(j, e):
        start = window(j)
        bias = bias_ref[e, (j * BLOCK_Q - start) // SWA_WINDOW]
        _store_scores(_dot_nt(q_scr[e, j], k_scr[pl.ds(start, span), :]) + bias, s_bufs[e])

    def finish(j, e):
        p, m = _softmax_numerators(s_bufs[e], extra_logit=sink_cols[e])
        o = _dot(p, v_scr[e, pl.ds(window(j), span), :])
        den = pltpu.roll(o, SWA_HEAD_DIM, 1) + jnp.exp2(sink_cols[e] - m)
        return o / den

    def emit(j, o0, o1):
        for e, o in ((0, o0), (1, o1)):
            orot = pltpu.roll(o, SWA_HEAD_DIM, 1)
            for k in range(R // 2):
                pb = e * (R // 2) + k
                even = (o if e == 0 else orot)[2 * k * BLOCK_Q:(2 * k + 1) * BLOCK_Q]
                odd = (o if e == 1 else orot)[(2 * k + 1) * BLOCK_Q:(2 * k + 2) * BLOCK_Q]
                o_ref[rows(j), pb * LANES:(pb + 1) * LANES] = jnp.where(lo, even, odd).astype(o_ref.dtype)

    _two_unit_pipeline(S // BLOCK_Q, scores, finish, emit)


def swa_attention(qkv, slopes, sink, gq, gk, B, S):
    n_steps = SWA_KV_HEADS // 2
    R = SWA_HEADS // SWA_KV_HEADS
    span = BLOCK_Q + 2 * SWA_WINDOW
    qw = SWA_HEADS * SWA_HEAD_DIM // n_steps
    kbase = SWA_HEADS * SWA_HEAD_DIM // LANES
    smem = pl.BlockSpec(memory_space=pltpu.SMEM)
    gain = pl.BlockSpec((1, LANES), lambda g, b: (0, 0))
    kern = functools.partial(_swa_attn_kernel, scale=SWA_HEAD_DIM ** -0.5)
    lane = np.arange(LANES)
    half_ones = jnp.asarray(lane[:, None] // SWA_HEAD_DIM == lane[None, :] // SWA_HEAD_DIM, dtype=BF16)
    swap = _partner_matrix((lane + SWA_HEAD_DIM) % LANES)
    t_s = np.arange(BLOCK_Q)[:, None] - np.arange(span)[None, :]
    dist = np.abs(np.stack([t_s + c * SWA_WINDOW for c in range(3)]))
    bias = jnp.where(jnp.asarray(dist <= SWA_WINDOW)[None], -slopes[:, None, None, None] * jnp.asarray(dist, F32)[None],
                     -jnp.inf)
    bias = bias.reshape(n_steps, 2, R, 3, BLOCK_Q, span).transpose(0, 1, 3, 2, 4, 5)
    bias = bias.reshape(n_steps, 2, 3, R * BLOCK_Q, span)
    return pl.pallas_call(
        kern,
        name="swa_attn",
        grid=(n_steps, B),
        in_specs=[
            smem,
            pl.BlockSpec((S, qw), lambda g, b: (b, g)),
            pl.BlockSpec((S, LANES), lambda g, b: (b, kbase + g)),
            pl.BlockSpec((S, LANES), lambda g, b: (b, kbase + n_steps + g)),
            pl.BlockSpec((None, 2, 3, R * BLOCK_Q, span), lambda g, b: (g, 0, 0, 0, 0)),
            _resident((LANES, LANES)), _resident((LANES, LANES)),
            gain, gain,
        ],
        out_specs=pl.BlockSpec((S, qw), lambda g, b: (b, g)),
        out_shape=jax.ShapeDtypeStruct((B * S, SWA_HEADS * SWA_HEAD_DIM), BF16),
        scratch_shapes=[
            pltpu.VMEM((2, S // BLOCK_Q, R * BLOCK_Q, LANES), BF16),
            pltpu.VMEM((S, LANES), BF16),
            pltpu.VMEM((2, S, LANES), BF16),
            pltpu.VMEM((R * BLOCK_Q, span), F32),
            pltpu.VMEM((R * BLOCK_Q, span), F32),
        ],
        compiler_params=_cparams("parallel", "parallel"),
    )(sink, qkv, qkv, qkv, bias, half_ones, swap, gq, gk)


def _swiglu_accumulate(h, wg_ref, wu_ref, wd_ref, acc_ref):
    for c in range(wg_ref.shape[1] // FFN_SUB):
        sl = slice(c * FFN_SUB, (c + 1) * FFN_SUB)
        g = _dot(h, wg_ref[:, sl])
        u = _dot(h, wu_ref[:, sl])
        a = (g * jax.nn.sigmoid(g) * u).astype(BF16)
        acc_ref[...] += _dot(a, wd_ref[sl, :])


def _ple(x, p, gain, win_ref, wgate_ref):
    gate = jax.nn.sigmoid(_dot(_rms(x, gain).astype(BF16), wgate_ref[...]))
    return x + _dot(p.astype(BF16), win_ref[...]) * gate


def _dense_tail_kernel(x_ref, a_ref, wo_ref, g_ref, wg_ref, wu_ref, wd_ref, p_ref, gp_ref, win_ref, wgate_ref, o_ref,
                       acc_scr):
    x = x_ref[...] + _dot(a_ref[...], wo_ref[...])
    acc_scr[...] = x
    _swiglu_accumulate(_rms(x, g_ref[...]).astype(BF16), wg_ref, wu_ref, wd_ref, acc_scr)
    o_ref[...] = _ple(acc_scr[...], p_ref[...], gp_ref[...], win_ref, wgate_ref)


def dense_layer_tail(x, a, w_o, gain, wg, wu, wd, p, layer, ple_gain, w_in, w_gate):
    T, D = x.shape
    K = a.shape[1]
    Fd = wg.shape[1]
    P = p.shape[2]
    row = lambda n: pl.BlockSpec((ROW_TILE, n), lambda i: (i, 0))
    return pl.pallas_call(
        _dense_tail_kernel,
        name="dense_tail",
        grid=(T // ROW_TILE,),
        in_specs=[
            row(D), row(K), _resident((K, D)),
            _resident((1, D)), _resident((D, Fd)), _resident((D, Fd)), _resident((Fd, D)),
            pl.BlockSpec((None, ROW_TILE, P), lambda i: (layer, i, 0)),
            _resident((1, D)), _resident((P, D)), _resident((D, D)),
        ],
        out_specs=row(D),
        out_shape=jax.ShapeDtypeStruct((T, D), F32),
        scratch_shapes=[pltpu.VMEM((ROW_TILE, D), F32)],
        compiler_params=_cparams("parallel"),
    )(x, a, w_o, gain.reshape(1, D), wg, wu, wd, p, ple_gain.reshape(1, D), w_in, w_gate)


def _router_kernel(x_ref, a_ref, wo_ref, g_ref, whi_ref, wlo_ref, b_ref, x1_ref, h_ref, idx_ref, wt_ref, cnt_ref):
    x1 = x_ref[...] + _dot(a_ref[...], wo_ref[...])
    x1_ref[...] = x1
    hf = _rms(x1, g_ref[...])
    h_hi = hf.astype(BF16)
    h_lo = (hf - h_hi.astype(F32)).astype(BF16)
    h_ref[...] = h_hi
    logits = _dot(h_hi, whi_ref[...]) + _dot(h_hi, wlo_ref[...]) + _dot(h_lo, whi_ref[...]) + b_ref[...]
    lane = lax.broadcasted_iota(jnp.int32, logits.shape, 1)
    logits = jnp.where(lane < N_EXPERTS, logits, -jnp.inf)
    m1 = jnp.max(logits, axis=-1, keepdims=True)
    i1 = jnp.min(jnp.where(logits == m1, lane, LANES), axis=-1, keepdims=True)
    rest = jnp.where(lane == i1, -jnp.inf, logits)
    m2 = jnp.max(rest, axis=-1, keepdims=True)
    i2 = jnp.min(jnp.where(rest == m2, lane, LANES), axis=-1, keepdims=True)
    e2 = jnp.exp(m2 - m1)
    w1 = 1.0 / (1.0 + e2)
    w2 = e2 / (1.0 + e2)
    wt_ref[...] = jnp.where(lane == 0, w1, jnp.where(lane == 1, w2, 0.0))
    onehot = jnp.where(jnp.logical_or(lane == i1, lane == i2), 1.0, 0.0)
    tm = onehot.shape[0]
    earlier = (lax.broadcasted_iota(jnp.int32, (tm, tm), 0) > lax.broadcasted_iota(jnp.int32, (tm, tm), 1))
    prefix = _dot(jnp.where(earlier, 1.0, 0.0).astype(BF16), onehot.astype(BF16))
    r1 = jnp.sum(jnp.where(lane == i1, prefix, 0.0), axis=-1, keepdims=True).astype(jnp.int32)
    r2 = jnp.sum(jnp.where(lane == i2, prefix, 0.0), axis=-1, keepdims=True).astype(jnp.int32)
    idx = jnp.where(lane == 0, i1, jnp.where(lane == 1, i2, jnp.where(lane == 2, r1, jnp.where(lane == 3, r2, 0))))
    idx_ref[...] = idx.T[:SUBLANES, :]
    cnt_ref[...] = jnp.broadcast_to(jnp.sum(onehot, axis=0, keepdims=True), cnt_ref.shape)


def moe_router(x, a, w_o, gain, w_hi, w_lo, bias):
    T, D = x.shape
    K = a.shape[1]
    row = lambda n: pl.BlockSpec((ROW_TILE, n), lambda i: (i, 0))
    return pl.pallas_call(
        _router_kernel,
        name="moe_router",
        grid=(T // ROW_TILE,),
        in_specs=[row(D), row(K), _resident((K, D)), _resident((1, D)), _resident((D, LANES)), _resident((D, LANES)),
                  _resident((1, LANES))],
        out_specs=[row(D), row(D), pl.BlockSpec((SUBLANES, ROW_TILE), lambda i: (i, 0)), row(LANES),
                   pl.BlockSpec((SUBLANES, LANES), lambda i: (i, 0))],
        out_shape=[
            jax.ShapeDtypeStruct((T, D), F32),
            jax.ShapeDtypeStruct((T, D), BF16),
            jax.ShapeDtypeStruct((T // ROW_TILE * SUBLANES, ROW_TILE), jnp.int32),
            jax.ShapeDtypeStruct((T, LANES), F32),
            jax.ShapeDtypeStruct((T // ROW_TILE * SUBLANES, LANES), F32),
        ],
        compiler_params=_cparams("parallel"),
    )(x, a, w_o, gain.reshape(1, D), w_hi, w_lo, bias)


def _moe_dispatch_kernel(n_ref, base_ref, h_ref, idx_ref, zeros_ref, hs_ref, dest_ref, loc_scr, sem):
    del zeros_ref
    i = pl.program_id(0)
    e0, e1, r0, r1 = (idx_ref[k:k + 1, :] for k in range(4))
    slot0, slot1, dest0, dest1 = r0, r1, r0, r1
    offs = []
    off = jnp.int32(0)
    for e in range(N_EXPERTS):
        offs.append(off)
        base = base_ref[i * N_EXPERTS + e]
        slot0 = slot0 + jnp.where(e0 == e, off, 0)
        slot1 = slot1 + jnp.where(e1 == e, off, 0)
        dest0 = dest0 + jnp.where(e0 == e, base, 0)
        dest1 = dest1 + jnp.where(e1 == e, base, 0)
        off = off + n_ref[i * N_EXPERTS + e]
    row = lax.broadcasted_iota(jnp.int32, (SUBLANES, e0.shape[1]), 0)
    dest_ref[...] = jnp.where(row == 0, dest0, jnp.where(row == 1, dest1, 0))
    slot = lax.broadcasted_iota(jnp.int32, (loc_scr.shape[0], e0.shape[1]), 0)
    perm = jnp.where(jnp.logical_or(slot == slot0, slot == slot1), 1.0, 0.0).astype(BF16)
    loc_scr[...] = _dot(perm, h_ref[...]).astype(BF16)

    def piece(src_row, dst_row):
        return pltpu.make_async_copy(loc_scr.at[pl.ds(src_row, CHUNK_ALIGN), :],
                                     hs_ref.at[pl.ds(dst_row, CHUNK_ALIGN), :], sem)

    for e in range(N_EXPERTS):
        base = base_ref[i * N_EXPERTS + e]

        def start(g, carry, e=e, base=base):
            piece(pl.multiple_of(offs[e] + g * CHUNK_ALIGN, CHUNK_ALIGN),
                  pl.multiple_of(base + g * CHUNK_ALIGN, CHUNK_ALIGN)).start()
            return carry

        lax.fori_loop(0, n_ref[i * N_EXPERTS + e] // CHUNK_ALIGN, start, 0)

    def wait(g, carry):
        piece(0, 0).wait()
        return carry

    lax.fori_loop(0, off // CHUNK_ALIGN, wait, 0)


def moe_dispatch(h, idx, chunk_rows, chunk_base, n_rows):
    T, D = h.shape
    n_rt = T // ROW_TILE
    loc_rows = TOP_K * ROW_TILE + N_EXPERTS * CHUNK_ALIGN
    grid_spec = pltpu.PrefetchScalarGridSpec(
        num_scalar_prefetch=2,
        grid=(n_rt,),
        in_specs=[
            pl.BlockSpec((ROW_TILE, D), lambda i, n, b: (i, 0)),
            pl.BlockSpec((SUBLANES, ROW_TILE), lambda i, n, b: (i, 0)),
            pl.BlockSpec(memory_space=pl.ANY),
        ],
        out_specs=[
            pl.BlockSpec(memory_space=pl.ANY),
            pl.BlockSpec((SUBLANES, ROW_TILE), lambda i, n, b: (i, 0)),
        ],
        scratch_shapes=[pltpu.VMEM((loc_rows, D), BF16), pltpu.SemaphoreType.DMA],
    )
    return pl.pallas_call(
        _moe_dispatch_kernel,
        name="moe_dispatch",
        grid_spec=grid_spec,
        out_shape=[
            jax.ShapeDtypeStruct((n_rows, D), BF16),
            jax.ShapeDtypeStruct((n_rt * SUBLANES, ROW_TILE), jnp.int32),
        ],
        input_output_aliases={4: 0},
        compiler_params=_cparams("arbitrary"),
    )(chunk_rows, chunk_base, h, idx, jnp.zeros((n_rows, D), BF16))


def _moe_ffn_kernel(te_ref, tv_ref, h_ref, wg_ref, wu_ref, wd_ref, o_ref, acc_scr):
    i = pl.program_id(0)
    f = pl.program_id(1)
    last = pl.num_programs(1) - 1

    @pl.when(f == 0)
    def _():
        acc_scr[...] = jnp.zeros_like(acc_scr)

    @pl.when(tv_ref[i] > 0)
    def _():
        _swiglu_accumulate(h_ref[...], wg_ref, wu_ref, wd_ref, acc_scr)

    @pl.when(f == last)
    def _():
        o_ref[...] = acc_scr[...].astype(o_ref.dtype)


def moe_ffn(h_sorted, tile_expert, tile_valid, wg, wu, wd, layer, n_chunks):
    R, D = h_sorted.shape
    Fe = wg.shape[3]
    tf = Fe // n_chunks

    def chunk(i, f, tv):
        return jnp.where(tv[i] > 0, f, n_chunks - 1)

    grid_spec = pltpu.PrefetchScalarGridSpec(
        num_scalar_prefetch=2,
        grid=(R // MOE_ROW_TILE, n_chunks),
        in_specs=[
            pl.BlockSpec((MOE_ROW_TILE, D), lambda i, f, te, tv: (i, 0)),
            pl.BlockSpec((None, None, D, tf), lambda i, f, te, tv: (layer, te[i], 0, chunk(i, f, tv))),
            pl.BlockSpec((None, None, D, tf), lambda i, f, te, tv: (layer, te[i], 0, chunk(i, f, tv))),
            pl.BlockSpec((None, None, tf, D), lambda i, f, te, tv: (layer, te[i], chunk(i, f, tv), 0)),
        ],
        out_specs=pl.BlockSpec((MOE_ROW_TILE, D), lambda i, f, te, tv: (i, 0)),
        scratch_shapes=[pltpu.VMEM((MOE_ROW_TILE, D), F32)],
    )
    return pl.pallas_call(
        _moe_ffn_kernel,
        name="moe_ffn",
        grid_spec=grid_spec,
        out_shape=jax.ShapeDtypeStruct((R, D), BF16),
        compiler_params=_cparams("parallel", "arbitrary"),
    )(tile_expert, tile_valid, h_sorted, wg, wu, wd)


def _moe_tail_kernel(x_ref, y0_ref, y1_ref, wt_ref, p_ref, g_ref, win_ref, wgate_ref, *rest):
    wt = wt_ref[...]
    x = x_ref[...] + wt[:, 0:1] * y0_ref[...].astype(F32) + wt[:, 1:2] * y1_ref[...].astype(F32)
    x = _ple(x, p_ref[...], g_ref[...], win_ref, wgate_ref)
    if len(rest) == 1:
        (o_ref,) = rest
    else:
        gn_ref, wn_ref, o_ref, qkv_ref = rest
        qkv_ref[...] = _dot(_rms(x, gn_ref[...]).astype(BF16), wn_ref[...]).astype(qkv_ref.dtype)
    o_ref[...] = x


def moe_layer_tail(x, y0, y1, wt, p, layer, gain, w_in, w_gate, next_proj=None):
    T, D = x.shape
    P = p.shape[2]
    row = lambda n: pl.BlockSpec((ROW_TILE, n), lambda i: (i, 0))
    ins = [x, y0, y1, wt, p, gain.reshape(1, D), w_in, w_gate]
    in_specs = [row(D), row(D), row(D), row(LANES), pl.BlockSpec((None, ROW_TILE, P), lambda i: (layer, i, 0)),
                _resident((1, D)), _resident((P, D)), _resident((D, D))]
    out_specs = row(D)
    out_shape = jax.ShapeDtypeStruct((T, D), F32)
    if next_proj is not None:
        gn, wn = next_proj
        N = wn.shape[1]
        ins += [gn.reshape(1, D), wn]
        in_specs += [_resident((1, D)), _resident((D, N))]
        out_specs = [out_specs, row(N)]
        out_shape = [out_shape, jax.ShapeDtypeStruct((T, N), BF16)]
    return pl.pallas_call(
        _moe_tail_kernel,
        name="moe_tail",
        grid=(T // ROW_TILE,),
        in_specs=in_specs,
        out_specs=out_specs,
        out_shape=out_shape,
        compiler_params=_cparams("parallel"),
    )(*ins)


def _rope_cos_sin(pos, dim, theta):
    inv = theta ** (-jnp.arange(0, dim, 2, dtype=F32) / dim)
    ang = pos.astype(F32)[:, None] * inv[None, :]
    return jnp.cos(ang), jnp.sin(ang)


def _fold_tables(gain_lanes, cos_lanes, sin_lanes, partner):
    return gain_lanes[None, :] * cos_lanes, gain_lanes[partner][None, :] * sin_lanes


def _partner_matrix(partner):
    m = np.zeros((LANES, LANES), np.float32)
    m[partner, np.arange(LANES)] = 1.0
    return jnp.asarray(m, dtype=BF16)


def _axial_tables(S, q_gain, k_gain):
    pos = jnp.arange(S)
    cr, sr = _rope_cos_sin(pos // GRID_W, AX_HEAD_DIM // 2, AX_THETA)
    cc, sc = _rope_cos_sin(pos % GRID_W, AX_HEAD_DIM // 2, AX_THETA)
    cos = jnp.concatenate([cr, cr, cc, cc], axis=1)
    sin = jnp.concatenate([-sr, sr, -sc, sc], axis=1)
    lane = np.arange(LANES)
    partner = np.where(lane % 64 < 32, lane + 32, lane - 32)
    return (_partner_matrix(partner),) + _fold_tables(q_gain, cos, sin, partner) + _fold_tables(k_gain, cos, sin, partner)


def _mla_tables(S, q_gain, k_gain):
    c, s = _rope_cos_sin(jnp.arange(S), MLA_ROPE, MLA_THETA)
    pad = LANES - MLA_QK
    cos = jnp.concatenate([jnp.ones((S, MLA_NOPE), F32), c, c, jnp.ones((S, pad), F32)], axis=1)
    sin = jnp.concatenate([jnp.zeros((S, MLA_NOPE), F32), -s, s, jnp.zeros((S, pad), F32)], axis=1)
    lane = np.arange(LANES)
    half = MLA_ROPE // 2
    partner = np.where((lane >= MLA_NOPE) & (lane < MLA_NOPE + half), lane + half,
                       np.where((lane >= MLA_NOPE + half) & (lane < MLA_QK), lane - half, lane))
    zpad = jnp.zeros((pad,), F32)
    gq = jnp.concatenate([q_gain, zpad])
    gk = jnp.concatenate([k_gain, zpad])
    return (_partner_matrix(partner),) + _fold_tables(gq, cos, sin, partner) + _fold_tables(gk, cos, sin, partner)


def _moe_layout(cnt, n_rt, n_tiles):
    tm = MOE_ROW_TILE
    cnt = cnt.reshape(n_rt, SUBLANES, LANES)[:, 0, :N_EXPERTS].astype(jnp.int32)
    chunk_rows = ((cnt + CHUNK_ALIGN - 1) // CHUNK_ALIGN) * CHUNK_ALIGN
    rt = jnp.arange(n_rt)
    before = jnp.sum(jnp.where((rt[None, :] < rt[:, None])[:, :, None], chunk_rows[None, :, :], 0), axis=1)
    region = ((jnp.sum(chunk_rows, axis=0) + tm - 1) // tm) * tm
    ex = jnp.arange(N_EXPERTS)
    ends = jnp.sum(jnp.where(ex[None, :] <= ex[:, None], region[None, :], 0), axis=1)
    chunk_base = (ends - region)[None, :] + before
    tile_start = jnp.arange(n_tiles, dtype=jnp.int32) * tm
    tile_expert = jnp.minimum(jnp.sum((tile_start[:, None] >= ends[None, :]).astype(jnp.int32), axis=1), N_EXPERTS - 1)
    tile_valid = (tile_start < ends[-1]).astype(jnp.int32)
    return chunk_rows.reshape(-1), chunk_base.reshape(-1), tile_expert, tile_valid


def kernel(x, p, attn_norm, ffn_norm, ple_norm, ple_w_in, ple_w_gate, mla_w_down, mla_q_norm, mla_w_uq, mla_kv_norm, mla_w_ukv, mla_q_gain, mla_k_gain, mla_w_o, swa_w_qkv, swa_q_gain, swa_k_gain, swa_sink, swa_w_o, ax_w_qkv, ax_q_gain, ax_k_gain, ax_w_o, ffn_w_gate, ffn_w_up, ffn_w_down, moe_w_router, moe_b_router, moe_w_gate, moe_w_up, moe_w_down):
    B, S, D = x.shape
    depth = p.shape[0]
    T = B * S
    xt = x.reshape(T, D)
    bf = lambda a: a.astype(BF16)
    n_rt = T // ROW_TILE
    n_moe_tiles = (TOP_K * T + n_rt * N_EXPERTS * (CHUNK_ALIGN - 1)) // MOE_ROW_TILE + N_EXPERTS
    slopes = jnp.asarray(2.0 ** (-8.0 * np.arange(1, SWA_HEADS + 1) / SWA_HEADS) * LOG2E, dtype=F32)
    p3 = p.reshape(depth, T, -1)
    moe_wg, moe_wu, moe_wd = bf(moe_w_gate), bf(moe_w_up), bf(moe_w_down)

    def plain_proj(i):
        if i >= depth or i % N_MIXERS == 0:
            return None
        w = swa_w_qkv if i % N_MIXERS == 1 else ax_w_qkv
        return attn_norm[i], bf(w[i // N_MIXERS])

    qkv_next = None
    for i in range(depth):
        kind = i % N_MIXERS
        j = i // N_MIXERS
        if kind == 0:
            wd = mla_w_down[j]
            zc = lambda n: jnp.zeros((D, n), F32)
            wd = jnp.concatenate([wd[:, :MLA_Q_RANK + MLA_KV_RANK], zc(MLA_NOPE), wd[:, MLA_Q_RANK + MLA_KV_RANK:],
                                  zc(LANES - MLA_QK)], axis=1)
            wuq = mla_w_uq[j].reshape(MLA_Q_RANK, MLA_HEADS, MLA_QK)
            wuq = jnp.pad(wuq, ((0, 0), (0, 0), (0, LANES - MLA_QK))).reshape(MLA_Q_RANK, MLA_HEADS * LANES)
            q, kv, kr = mla_proj(xt, attn_norm[i], bf(wd), mla_q_norm[j], mla_kv_norm[j], bf(wuq), bf(mla_w_ukv[j]))
            o = mla_attention(q, kv, kr, _mla_tables(S, mla_q_gain[j], mla_k_gain[j]), B, S)
            w_o = mla_w_o[j]
        elif kind == 1:
            qkv = qkv_next if qkv_next is not None else norm_proj(xt, *plain_proj(i))
            gq = jnp.tile(swa_q_gain[j], 2).reshape(1, LANES)
            gk = jnp.tile(swa_k_gain[j], 2).reshape(1, LANES)
            o = swa_attention(qkv, slopes, swa_sink[j].astype(F32) * LOG2E, gq, gk, B, S)
            w_o = swa_w_o[j]
        else:
            qkv = qkv_next if qkv_next is not None else norm_proj(xt, *plain_proj(i))
            o = axial_attention(qkv, _axial_tables(S, ax_q_gain[j], ax_k_gain[j]), B, S)
            w_o = ax_w_o[j]
        qkv_next = None
        f = i // 2
        if i % 2 == 0:
            xt = dense_layer_tail(xt, o, bf(w_o), ffn_norm[i], bf(ffn_w_gate[f]), bf(ffn_w_up[f]), bf(ffn_w_down[f]),
                                  p3, i, ple_norm[i], bf(ple_w_in[i]), bf(ple_w_gate[i]))
        else:
            wr = jnp.pad(moe_w_router[f], ((0, 0), (0, LANES - N_EXPERTS)))
            wr_hi = bf(wr)
            wr_lo = bf(wr - wr_hi.astype(F32))
            br = jnp.pad(moe_b_router[f].astype(F32), (0, LANES - N_EXPERTS)).reshape(1, LANES)
            xt, h, idx, wt, cnt = moe_router(xt, o, bf(w_o), ffn_norm[i], wr_hi, wr_lo, br)
            chunk_rows, chunk_base, tile_expert, tile_valid = _moe_layout(cnt, n_rt, n_moe_tiles)
            h_sorted, dest = moe_dispatch(h, idx, chunk_rows, chunk_base, n_moe_tiles * MOE_ROW_TILE)
            y = moe_ffn(h_sorted, tile_expert, tile_valid, moe_wg, moe_wu, moe_wd, f, 2)
            dest = dest.reshape(n_rt, SUBLANES, ROW_TILE)
            y0 = jnp.take(y, dest[:, 0, :].reshape(T), axis=0, mode="clip")
            y1 = jnp.take(y, dest[:, 1, :].reshape(T), axis=0, mode="clip")
            nxt = plain_proj(i + 1)
            out = moe_layer_tail(xt, y0, y1, wt, p3, i, ple_norm[i], bf(ple_w_in[i]), bf(ple_w_gate[i]), nxt)
            xt, qkv_next = out if nxt is not None else (out, None)
    return xt.reshape(B, S, D)
```

```python
import functools

import numpy as np
import jax
import jax.numpy as jnp
from jax import lax
from jax.experimental import pallas as pl
from jax.experimental.pallas import tpu as pltpu

F32 = jnp.float32
BF16 = jnp.bfloat16

EPS = 1e-6
GRID_W = 64
BLOCK_Q = 128

MLA_HEADS = 16
MLA_NOPE = 64
MLA_ROPE = 32
MLA_V = 64
MLA_Q_RANK = 256
MLA_KV_RANK = 128
MLA_THETA = 10000.0
MLA_QK = MLA_NOPE + MLA_ROPE

SWA_HEADS = 16
SWA_KV_HEADS = 4
SWA_HEAD_DIM = 64
SWA_WINDOW = 128

AX_HEADS = 8
AX_KV_HEADS = 4
AX_HEAD_DIM = 128
AX_THETA = 10000.0

N_EXPERTS = 8
TOP_K = 2
N_MIXERS = 3

LANES = 128
SUBLANES = 8
ROW_TILE = 512
MOE_ROW_TILE = 512
CHUNK_ALIGN = 2 * SUBLANES
ATTN_Q_TILE = 512
FFN_SUB = 256
VMEM_LIMIT = 56 * 1024 * 1024
LOG2E = 1.4426950408889634


def _cparams(*sem):
    return pltpu.CompilerParams(dimension_semantics=sem, vmem_limit_bytes=VMEM_LIMIT)


def _rms(xf, gain):
    ms = jnp.mean(xf * xf, axis=-1, keepdims=True)
    return xf * lax.rsqrt(ms + EPS) * gain


def _dot(a, b):
    return jnp.dot(a, b, preferred_element_type=F32)


def _dot_nt(a, b):
    return lax.dot_general(a, b, (((1,), (1,)), ((), ())), preferred_element_type=F32)


def _resident(shape):
    return pl.BlockSpec(shape, lambda *_: (0,) * len(shape), pipeline_mode=pl.Buffered(1))


def _norm_proj_kernel(x_ref, g_ref, w_ref, o_ref):
    h = _rms(x_ref[...], g_ref[...]).astype(BF16)
    o_ref[...] = _dot(h, w_ref[...]).astype(o_ref.dtype)


def norm_proj(x, gain, w):
    T, D = x.shape
    N = w.shape[1]
    return pl.pallas_call(
        _norm_proj_kernel,
        name="norm_proj",
        grid=(T // ROW_TILE,),
        in_specs=[
            pl.BlockSpec((ROW_TILE, D), lambda i: (i, 0)),
            _resident((1, D)),
            _resident((D, N)),
        ],
        out_specs=pl.BlockSpec((ROW_TILE, N), lambda i: (i, 0)),
        out_shape=jax.ShapeDtypeStruct((T, N), BF16),
        compiler_params=_cparams("parallel"),
    )(x, gain.reshape(1, D), w)


def _mla_proj_kernel(x_ref, g_ref, wd_ref, qn_ref, kvn_ref, wuq_ref, wukv_ref, q_ref, kv_ref, kr_ref):
    h = _rms(x_ref[...], g_ref[...]).astype(BF16)
    down = _dot(h, wd_ref[...])
    cq = _rms(down[:, :MLA_Q_RANK], qn_ref[...]).astype(BF16)
    ckv = _rms(down[:, MLA_Q_RANK:MLA_Q_RANK + MLA_KV_RANK], kvn_ref[...]).astype(BF16)
    q_ref[...] = _dot(cq, wuq_ref[...]).astype(BF16)
    kv_ref[...] = _dot(ckv, wukv_ref[...]).astype(BF16)
    kr_ref[...] = down[:, MLA_Q_RANK + MLA_KV_RANK:].astype(BF16)


def mla_proj(x, gain, wd, qn, kvn, wuq, wukv):
    T, D = x.shape
    nd = wd.shape[1]
    nq = wuq.shape[1]
    nkv = wukv.shape[1]
    row = lambda n: pl.BlockSpec((ROW_TILE, n), lambda i: (i, 0))
    return pl.pallas_call(
        _mla_proj_kernel,
        name="mla_proj",
        grid=(T // ROW_TILE,),
        in_specs=[
            row(D),
            _resident((1, D)),
            _resident((D, nd)),
            _resident((1, MLA_Q_RANK)),
            _resident((1, MLA_KV_RANK)),
            _resident((MLA_Q_RANK, nq)),
            _resident((MLA_KV_RANK, nkv)),
        ],
        out_specs=[row(nq), row(nkv), row(LANES)],
        out_shape=[
            jax.ShapeDtypeStruct((T, nq), BF16),
            jax.ShapeDtypeStruct((T, nkv), BF16),
            jax.ShapeDtypeStruct((T, LANES), BF16),
        ],
        compiler_params=_cparams("parallel"),
    )(x, gain.reshape(1, D), wd, qn.reshape(1, -1), kvn.reshape(1, -1), wuq, wukv)


def _two_unit_pipeline(n, scores, finish, emit):
    scores(0, 0)
    for i in range(n):
        scores(i, 1)
        o0 = finish(i, 0)
        o1 = finish(i, 1)
        if i + 1 < n:
            scores(i + 1, 0)
        emit(i, o0, o1)


def _softmax_numerators(s_buf, extra_logit=None):
    m = jnp.max(s_buf[...], axis=-1, keepdims=True)
    if extra_logit is not None:
        m = jnp.maximum(m, extra_logit)
    return jnp.exp2(s_buf[...] - m).astype(BF16), m


def _norm_rope(x, perm_ref, a, b, inv_dim, extra):
    xf = x.astype(F32)
    c = lax.rsqrt(jnp.sum(xf * xf, axis=-1, keepdims=True) * inv_dim + EPS) * extra
    return (xf * a + _dot(x, perm_ref[...]) * b) * c


def _axial_attn_kernel(q_ref, k_ref, v_ref, perm_ref, aq_ref, bq_ref, ak_ref, bk_ref, o_ref, q_scr, k_scr, v_scr,
                       s0_scr, s1_scr, *, tq, scale):
    S = k_ref.shape[0]
    s_bufs = (s0_scr, s1_scr)
    inv_dim = 1.0 / AX_HEAD_DIM

    k_scr[...] = _norm_rope(k_ref[...], perm_ref, ak_ref[...], bk_ref[...], inv_dim, 1.0).astype(BF16)
    for u in range(2):
        q_scr[u] = _norm_rope(q_ref[:, u * LANES:(u + 1) * LANES], perm_ref, aq_ref[...], bq_ref[...], inv_dim,
                              scale * LOG2E).astype(BF16)
    v_scr[:, :LANES] = v_ref[...]
    v_scr[:, LANES:] = jnp.ones((S, LANES), BF16)

    def rows(i):
        return pl.ds(i * tq, tq)

    def scores(i, u):
        s_bufs[u][...] = _dot_nt(q_scr[u, rows(i), :], k_scr[...])

    def finish(i, u):
        p, _ = _softmax_numerators(s_bufs[u])
        o = _dot(p, v_scr[...])
        return o[:, :LANES] / o[:, LANES:]

    def emit(i, o0, o1):
        o_ref[rows(i), :LANES] = o0.astype(o_ref.dtype)
        o_ref[rows(i), LANES:] = o1.astype(o_ref.dtype)

    _two_unit_pipeline(S // tq, scores, finish, emit)


def axial_attention(qkv, tabs, B, S):
    R = AX_HEADS // AX_KV_HEADS
    assert R == 2
    tq = min(ATTN_Q_TILE, S)
    kern = functools.partial(_axial_attn_kernel, tq=tq, scale=AX_HEAD_DIM ** -0.5)
    tab = pl.BlockSpec((S, LANES), lambda b, g: (0, 0), pipeline_mode=pl.Buffered(1))
    return pl.pallas_call(
        kern,
        name="axial_attn",
        grid=(B, AX_KV_HEADS),
        in_specs=[
            pl.BlockSpec((S, R * LANES), lambda b, g: (b, g)),
            pl.BlockSpec((S, LANES), lambda b, g: (b, AX_HEADS + g)),
            pl.BlockSpec((S, LANES), lambda b, g: (b, AX_HEADS + AX_KV_HEADS + g)),
            _resident((LANES, LANES)), tab, tab, tab, tab,
        ],
        out_specs=pl.BlockSpec((S, R * LANES), lambda b, g: (b, g)),
        out_shape=jax.ShapeDtypeStruct((B * S, AX_HEADS * AX_HEAD_DIM), BF16),
        scratch_shapes=[
            pltpu.VMEM((R, S, LANES), BF16),
            pltpu.VMEM((S, LANES), BF16),
            pltpu.VMEM((S, 2 * LANES), BF16),
            pltpu.VMEM((tq, S), F32),
            pltpu.VMEM((tq, S), F32),
        ],
        compiler_params=_cparams("parallel", "parallel"),
    )(qkv, qkv, qkv, *tabs)


def _mla_attn_kernel(q_ref, kv_ref, kr_ref, perm_ref, aq_ref, bq_ref, ak_ref, bk_ref, o_ref, q_scr, k_scr, v_scr,
                     s0_scr, s1_scr, *, tq, scale):
    S = kv_ref.shape[0]
    s_bufs = (s0_scr, s1_scr)
    lane = lax.broadcasted_iota(jnp.int32, (1, LANES), 1)
    lo = lane < MLA_NOPE
    inv_dim = 1.0 / MLA_QK

    kr = kr_ref[...]
    krf = kr.astype(F32)
    rope_part = krf * ak_ref[...] + _dot(kr, perm_ref[...]) * bk_ref[...]
    ss_rope = jnp.sum(krf * krf, axis=-1, keepdims=True)
    for hh in range(2):
        kvh = kv_ref[:, hh * LANES:(hh + 1) * LANES].astype(F32)
        ss = jnp.sum(jnp.where(lo, kvh * kvh, 0.0), axis=-1, keepdims=True) + ss_rope
        k = jnp.where(lo, kvh * ak_ref[...], rope_part)
        k_scr[hh] = (k * lax.rsqrt(ss * inv_dim + EPS)).astype(BF16)
        vh = jnp.where(lo, pltpu.roll(kvh, MLA_V, 1), 1.0) if hh == 0 else jnp.where(lo, 1.0, kvh)
        v_scr[hh] = vh.astype(BF16)
        q_scr[hh] = _norm_rope(q_ref[:, hh * LANES:(hh + 1) * LANES], perm_ref, aq_ref[...], bq_ref[...], inv_dim,
                               scale * LOG2E).astype(BF16)

    def rows(i):
        return pl.ds(i * tq, tq)

    def scores(i, u):
        s_bufs[u][...] = _dot_nt(q_scr[u, rows(i), :], k_scr[u])

    def finish(i, u):
        p, _ = _softmax_numerators(s_bufs[u])
        o = _dot(p, v_scr[u])
        return o / pltpu.roll(o, MLA_V, 1)

    def emit(i, o0, o1):
        o_ref[rows(i), :] = jnp.where(lo, o0, o1).astype(o_ref.dtype)

    _two_unit_pipeline(S // tq, scores, finish, emit)


def mla_attention(q, kv, kr, tabs, B, S):
    tq = min(ATTN_Q_TILE, S)
    kern = functools.partial(_mla_attn_kernel, tq=tq, scale=MLA_QK ** -0.5)
    tab = pl.BlockSpec((S, LANES), lambda b, g: (0, 0), pipeline_mode=pl.Buffered(1))
    return pl.pallas_call(
        kern,
        name="mla_attn",
        grid=(B, MLA_HEADS // 2),
        in_specs=[
            pl.BlockSpec((S, 2 * LANES), lambda b, g: (b, g)),
            pl.BlockSpec((S, 2 * LANES), lambda b, g: (b, g)),
            pl.BlockSpec((S, LANES), lambda b, g: (b, 0)),
            _resident((LANES, LANES)), tab, tab, tab, tab,
        ],
        out_specs=pl.BlockSpec((S, LANES), lambda b, g: (b, g)),
        out_shape=jax.ShapeDtypeStruct((B * S, MLA_HEADS * MLA_V), BF16),
        scratch_shapes=[
            pltpu.VMEM((2, S, LANES), BF16),
            pltpu.VMEM((2, S, LANES), BF16),
            pltpu.VMEM((2, S, LANES), BF16),
            pltpu.VMEM((tq, S), F32),
            pltpu.VMEM((tq, S), F32),
        ],
        compiler_params=_cparams("parallel", "parallel"),
    )(q, kv, kr, *tabs)


def _swa_attn_kernel(sink_ref, q_ref, k_ref, v_ref, bias_ref, half_ref, swap_ref, gq_ref, gk_ref, o_ref,
                     q_scr, k_scr, v_scr, s0_scr, s1_scr, *, scale):
    S = k_ref.shape[0]
    span = BLOCK_Q + 2 * SWA_WINDOW
    R = SWA_HEADS // SWA_KV_HEADS
    rows_u = R * BLOCK_Q
    s_bufs = (s0_scr, s1_scr)
    pid = pl.program_id(0)
    lane = lax.broadcasted_iota(jnp.int32, (1, LANES), 1)
    lo = lane < SWA_HEAD_DIM
    hi = jnp.logical_not(lo)

    def seg_norm(x, gain):
        xf = x.astype(F32)
        sq = xf * xf
        sq_hi = sq.astype(BF16)
        sq_lo = (sq - sq_hi.astype(F32)).astype(BF16)
        ss = _dot(sq_hi, half_ref[...]) + _dot(sq_lo, half_ref[...])
        return xf * lax.rsqrt(ss * (1.0 / SWA_HEAD_DIM) + EPS) * gain

    k_scr[...] = seg_norm(k_ref[...], gk_ref[...]).astype(BF16)
    v = v_ref[...].astype(F32)
    v_scr[0] = jnp.where(lo, v, 1.0).astype(BF16)
    v_scr[1] = jnp.where(lo, 1.0, v).astype(BF16)
    for pb in range(R):
        e = pb // (R // 2)
        keep = lo if e == 0 else hi
        qp = seg_norm(q_ref[:, pb * LANES:(pb + 1) * LANES], gq_ref[...]) * (scale * LOG2E)
        qr = _dot(qp.astype(BF16), swap_ref[...])
        for i in range(2):
            qz = jnp.where(keep, qp if i == e else qr, 0.0).astype(BF16)
            r = (pb % (R // 2)) * 2 + i
            for j in range(S // BLOCK_Q):
                q_scr[e, j, r * BLOCK_Q:(r + 1) * BLOCK_Q, :] = qz[j * BLOCK_Q:(j + 1) * BLOCK_Q, :]

    head_of_row = lax.broadcasted_iota(jnp.int32, (rows_u, 1), 0) // BLOCK_Q

    def head_column(ref, e):
        col = jnp.zeros((rows_u, 1), F32)
        for r in range(R):
            col = jnp.where(head_of_row == r, ref[pid * 2 * R + e * R + r], col)
        return col

    sink_cols = [head_column(sink_ref, e) for e in range(2)]

    def rows(j):
        return pl.ds(j * BLOCK_Q, BLOCK_Q)

    def window(j):
        return min(max(j * BLOCK_Q - SWA_WINDOW, 0), S - span)

    def scores(j, e):
        start = window(j)
        bias = bias_ref[e, (j * BLOCK_Q - start) // SWA_WINDOW]
        s_bufs[e][...] = _dot_nt(q_scr[e, j], k_scr[pl.ds(start, span), :]) + bias

    def finish(j, e):
        p, m = _softmax_numerators(s_bufs[e], extra_logit=sink_cols[e])
        o = _dot(p, v_scr[e, pl.ds(window(j), span), :])
        den = pltpu.roll(o, SWA_HEAD_DIM, 1) + jnp.exp2(sink_cols[e] - m)
        return o / den

    def emit(j, o0, o1):
        for e, o in ((0, o0), (1, o1)):
            orot = pltpu.roll(o, SWA_HEAD_DIM, 1)
            for k in range(R // 2):
                pb = e * (R // 2) + k
                even = (o if e == 0 else orot)[2 * k * BLOCK_Q:(2 * k + 1) * BLOCK_Q]
                odd = (o if e == 1 else orot)[(2 * k + 1) * BLOCK_Q:(2 * k + 2) * BLOCK_Q]
                o_ref[rows(j), pb * LANES:(pb + 1) * LANES] = jnp.where(lo, even, odd).astype(o_ref.dtype)

    _two_unit_pipeline(S // BLOCK_Q, scores, finish, emit)


def swa_attention(qkv, slopes, sink, gq, gk, B, S):
    n_steps = SWA_KV_HEADS // 2
    R = SWA_HEADS // SWA_KV_HEADS
    span = BLOCK_Q + 2 * SWA_WINDOW
    qw = SWA_HEADS * SWA_HEAD_DIM // n_steps
    kbase = SWA_HEADS * SWA_HEAD_DIM // LANES
    smem = pl.BlockSpec(memory_space=pltpu.SMEM)
    gain = pl.BlockSpec((1, LANES), lambda g, b: (0, 0))
    kern = functools.partial(_swa_attn_kernel, scale=SWA_HEAD_DIM ** -0.5)
    lane = np.arange(LANES)
    half_ones = jnp.asarray(lane[:, None] // SWA_HEAD_DIM == lane[None, :] // SWA_HEAD_DIM, dtype=BF16)
    swap = _partner_matrix((lane + SWA_HEAD_DIM) % LANES)
    t_s = np.arange(BLOCK_Q)[:, None] - np.arange(span)[None, :]
    dist = np.abs(np.stack([t_s + c * SWA_WINDOW for c in range(3)]))
    bias = jnp.where(jnp.asarray(dist <= SWA_WINDOW)[None], -slopes[:, None, None, None] * jnp.asarray(dist, F32)[None],
                     -jnp.inf)
    bias = bias.reshape(n_steps, 2, R, 3, BLOCK_Q, span).transpose(0, 1, 3, 2, 4, 5)
    bias = bias.reshape(n_steps, 2, 3, R * BLOCK_Q, span)
    return pl.pallas_call(
        kern,
        name="swa_attn",
        grid=(n_steps, B),
        in_specs=[
            smem,
            pl.BlockSpec((S, qw), lambda g, b: (b, g)),
            pl.BlockSpec((S, LANES), lambda g, b: (b, kbase + g)),
            pl.BlockSpec((S, LANES), lambda g, b: (b, kbase + n_steps + g)),
            pl.BlockSpec((None, 2, 3, R * BLOCK_Q, span), lambda g, b: (g, 0, 0, 0, 0)),
            _resident((LANES, LANES)), _resident((LANES, LANES)),
            gain, gain,
        ],
        out_specs=pl.BlockSpec((S, qw), lambda g, b: (b, g)),
        out_shape=jax.ShapeDtypeStruct((B * S, SWA_HEADS * SWA_HEAD_DIM), BF16),
        scratch_shapes=[
            pltpu.VMEM((2, S // BLOCK_Q, R * BLOCK_Q, LANES), BF16),
            pltpu.VMEM((S, LANES), BF16),
            pltpu.VMEM((2, S, LANES), BF16),
            pltpu.VMEM((R * BLOCK_Q, span), F32),
            pltpu.VMEM((R * BLOCK_Q, span), F32),
        ],
        compiler_params=_cparams("parallel", "parallel"),
    )(sink, qkv, qkv, qkv, bias, half_ones, swap, gq, gk)


def _swiglu_accumulate(h, wg_ref, wu_ref, wd_ref, acc_ref):
    for c in range(wg_ref.shape[1] // FFN_SUB):
        sl = slice(c * FFN_SUB, (c + 1) * FFN_SUB)
        g = _dot(h, wg_ref[:, sl])
        u = _dot(h, wu_ref[:, sl])
        a = (g * jax.nn.sigmoid(g) * u).astype(BF16)
        acc_ref[...] += _dot(a, wd_ref[sl, :])


def _ple(x, p, gain, win_ref, wgate_ref):
    gate = jax.nn.sigmoid(_dot(_rms(x, gain).astype(BF16), wgate_ref[...]))
    return x + _dot(p.astype(BF16), win_ref[...]) * gate


def _dense_tail_kernel(x_ref, a_ref, wo_ref, g_ref, wg_ref, wu_ref, wd_ref, p_ref, gp_ref, win_ref, wgate_ref, o_ref,
                       acc_scr):
    x = x_ref[...] + _dot(a_ref[...], wo_ref[...])
    acc_scr[...] = x
    _swiglu_accumulate(_rms(x, g_ref[...]).astype(BF16), wg_ref, wu_ref, wd_ref, acc_scr)
    o_ref[...] = _ple(acc_scr[...], p_ref[...], gp_ref[...], win_ref, wgate_ref)


def dense_layer_tail(x, a, w_o, gain, wg, wu, wd, p, layer, ple_gain, w_in, w_gate):
    T, D = x.shape
    K = a.shape[1]
    Fd = wg.shape[1]
    P = p.shape[2]
    row = lambda n: pl.BlockSpec((ROW_TILE, n), lambda i: (i, 0))
    return pl.pallas_call(
        _dense_tail_kernel,
        name="dense_tail",
        grid=(T // ROW_TILE,),
        in_specs=[
            row(D), row(K), _resident((K, D)),
            _resident((1, D)), _resident((D, Fd)), _resident((D, Fd)), _resident((Fd, D)),
            pl.BlockSpec((None, ROW_TILE, P), lambda i: (layer, i, 0)),
            _resident((1, D)), _resident((P, D)), _resident((D, D)),
        ],
        out_specs=row(D),
        out_shape=jax.ShapeDtypeStruct((T, D), F32),
        scratch_shapes=[pltpu.VMEM((ROW_TILE, D), F32)],
        compiler_params=_cparams("parallel"),
    )(x, a, w_o, gain.reshape(1, D), wg, wu, wd, p, ple_gain.reshape(1, D), w_in, w_gate)


def _router_kernel(x_ref, a_ref, wo_ref, g_ref, whi_ref, wlo_ref, b_ref, x1_ref, h_ref, idx_ref, wt_ref, cnt_ref):
    x1 = x_ref[...] + _dot(a_ref[...], wo_ref[...])
    x1_ref[...] = x1
    hf = _rms(x1, g_ref[...])
    h_hi = hf.astype(BF16)
    h_lo = (hf - h_hi.astype(F32)).astype(BF16)
    h_ref[...] = h_hi
    logits = _dot(h_hi, whi_ref[...]) + _dot(h_hi, wlo_ref[...]) + _dot(h_lo, whi_ref[...]) + b_ref[...]
    lane = lax.broadcasted_iota(jnp.int32, logits.shape, 1)
    logits = jnp.where(lane < N_EXPERTS, logits, -jnp.inf)
    m1 = jnp.max(logits, axis=-1, keepdims=True)
    i1 = jnp.min(jnp.where(logits == m1, lane, LANES), axis=-1, keepdims=True)
    rest = jnp.where(lane == i1, -jnp.inf, logits)
    m2 = jnp.max(rest, axis=-1, keepdims=True)
    i2 = jnp.min(jnp.where(rest == m2, lane, LANES), axis=-1, keepdims=True)
    e2 = jnp.exp(m2 - m1)
    w1 = 1.0 / (1.0 + e2)
    w2 = e2 / (1.0 + e2)
    wt_ref[...] = jnp.where(lane == 0, w1, jnp.where(lane == 1, w2, 0.0))
    onehot = jnp.where(jnp.logical_or(lane == i1, lane == i2), 1.0, 0.0)
    tm = onehot.shape[0]
    earlier = (lax.broadcasted_iota(jnp.int32, (tm, tm), 0) > lax.broadcasted_iota(jnp.int32, (tm, tm), 1))
    prefix = _dot(jnp.where(earlier, 1.0, 0.0).astype(BF16), onehot.astype(BF16))
    r1 = jnp.sum(jnp.where(lane == i1, prefix, 0.0), axis=-1, keepdims=True).astype(jnp.int32)
    r2 = jnp.sum(jnp.where(lane == i2, prefix, 0.0), axis=-1, keepdims=True).astype(jnp.int32)
    idx = jnp.where(lane == 0, i1, jnp.where(lane == 1, i2, jnp.where(lane == 2, r1, jnp.where(lane == 3, r2, 0))))
    idx_ref[...] = idx.T[:SUBLANES, :]
    cnt_ref[...] = jnp.broadcast_to(jnp.sum(onehot, axis=0, keepdims=True), cnt_ref.shape)


def moe_router(x, a, w_o, gain, w_hi, w_lo, bias):
    T, D = x.shape
    K = a.shape[1]
    row = lambda n: pl.BlockSpec((ROW_TILE, n), lambda i: (i, 0))
    return pl.pallas_call(
        _router_kernel,
        name="moe_router",
        grid=(T // ROW_TILE,),
        in_specs=[row(D), row(K), _resident((K, D)), _resident((1, D)), _resident((D, LANES)), _resident((D, LANES)),
                  _resident((1, LANES))],
        out_specs=[row(D), row(D), pl.BlockSpec((SUBLANES, ROW_TILE), lambda i: (i, 0)), row(LANES),
                   pl.BlockSpec((SUBLANES, LANES), lambda i: (i, 0))],
        out_shape=[
            jax.ShapeDtypeStruct((T, D), F32),
            jax.ShapeDtypeStruct((T, D), BF16),
            jax.ShapeDtypeStruct((T // ROW_TILE * SUBLANES, ROW_TILE), jnp.int32),
            jax.ShapeDtypeStruct((T, LANES), F32),
            jax.ShapeDtypeStruct((T // ROW_TILE * SUBLANES, LANES), F32),
        ],
        compiler_params=_cparams("parallel"),
    )(x, a, w_o, gain.reshape(1, D), w_hi, w_lo, bias)


def _moe_dispatch_kernel(n_ref, base_ref, h_ref, idx_ref, zeros_ref, hs_ref, dest_ref, loc_scr, sem):
    del zeros_ref
    i = pl.program_id(0)
    e0, e1, r0, r1 = (idx_ref[k:k + 1, :] for k in range(4))
    slot0, slot1, dest0, dest1 = r0, r1, r0, r1
    offs = []
    off = jnp.int32(0)
    for e in range(N_EXPERTS):
        offs.append(off)
        base = base_ref[i * N_EXPERTS + e]
        slot0 = slot0 + jnp.where(e0 == e, off, 0)
        slot1 = slot1 + jnp.where(e1 == e, off, 0)
        dest0 = dest0 + jnp.where(e0 == e, base, 0)
        dest1 = dest1 + jnp.where(e1 == e, base, 0)
        off = off + n_ref[i * N_EXPERTS + e]
    row = lax.broadcasted_iota(jnp.int32, (SUBLANES, e0.shape[1]), 0)
    dest_ref[...] = jnp.where(row == 0, dest0, jnp.where(row == 1, dest1, 0))
    slot = lax.broadcasted_iota(jnp.int32, (loc_scr.shape[0], e0.shape[1]), 0)
    perm = jnp.where(jnp.logical_or(slot == slot0, slot == slot1), 1.0, 0.0).astype(BF16)
    loc_scr[...] = _dot(perm, h_ref[...]).astype(BF16)

    def piece(src_row, dst_row):
        return pltpu.make_async_copy(loc_scr.at[pl.ds(src_row, CHUNK_ALIGN), :],
                                     hs_ref.at[pl.ds(dst_row, CHUNK_ALIGN), :], sem)

    for e in range(N_EXPERTS):
        base = base_ref[i * N_EXPERTS + e]

        def start(g, carry, e=e, base=base):
            piece(pl.multiple_of(offs[e] + g * CHUNK_ALIGN, CHUNK_ALIGN),
                  pl.multiple_of(base + g * CHUNK_ALIGN, CHUNK_ALIGN)).start()
            return carry

        lax.fori_loop(0, n_ref[i * N_EXPERTS + e] // CHUNK_ALIGN, start, 0)

    def wait(g, carry):
        piece(0, 0).wait()
        return carry

    lax.fori_loop(0, off // CHUNK_ALIGN, wait, 0)


def moe_dispatch(h, idx, chunk_rows, chunk_base, n_rows):
    T, D = h.shape
    n_rt = T // ROW_TILE
    loc_rows = TOP_K * ROW_TILE + N_EXPERTS * CHUNK_ALIGN
    grid_spec = pltpu.PrefetchScalarGridSpec(
        num_scalar_prefetch=2,
        grid=(n_rt,),
        in_specs=[
            pl.BlockSpec((ROW_TILE, D), lambda i, n, b: (i, 0)),
            pl.BlockSpec((SUBLANES, ROW_TILE), lambda i, n, b: (i, 0)),
            pl.BlockSpec(memory_space=pl.ANY),
        ],
        out_specs=[
            pl.BlockSpec(memory_space=pl.ANY),
            pl.BlockSpec((SUBLANES, ROW_TILE), lambda i, n, b: (i, 0)),
        ],
        scratch_shapes=[pltpu.VMEM((loc_rows, D), BF16), pltpu.SemaphoreType.DMA],
    )
    return pl.pallas_call(
        _moe_dispatch_kernel,
        name="moe_dispatch",
        grid_spec=grid_spec,
        out_shape=[
            jax.ShapeDtypeStruct((n_rows, D), BF16),
            jax.ShapeDtypeStruct((n_rt * SUBLANES, ROW_TILE), jnp.int32),
        ],
        input_output_aliases={4: 0},
        compiler_params=_cparams("arbitrary"),
    )(chunk_rows, chunk_base, h, idx, jnp.zeros((n_rows, D), BF16))


def _moe_ffn_kernel(te_ref, tv_ref, h_ref, wg_ref, wu_ref, wd_ref, o_ref, acc_scr):
    i = pl.program_id(0)
    f = pl.program_id(1)
    last = pl.num_programs(1) - 1

    @pl.when(f == 0)
    def _():
        acc_scr[...] = jnp.zeros_like(acc_scr)

    @pl.when(tv_ref[i] > 0)
    def _():
        _swiglu_accumulate(h_ref[...], wg_ref, wu_ref, wd_ref, acc_scr)

    @pl.when(f == last)
    def _():
        o_ref[...] = acc_scr[...].astype(o_ref.dtype)


def moe_ffn(h_sorted, tile_expert, tile_valid, wg, wu, wd, layer, n_chunks):
    R, D = h_sorted.shape
    Fe = wg.shape[3]
    tf = Fe // n_chunks

    def chunk(i, f, tv):
        return jnp.where(tv[i] > 0, f, n_chunks - 1)

    grid_spec = pltpu.PrefetchScalarGridSpec(
        num_scalar_prefetch=2,
        grid=(R // MOE_ROW_TILE, n_chunks),
        in_specs=[
            pl.BlockSpec((MOE_ROW_TILE, D), lambda i, f, te, tv: (i, 0)),
            pl.BlockSpec((None, None, D, tf), lambda i, f, te, tv: (layer, te[i], 0, chunk(i, f, tv))),
            pl.BlockSpec((None, None, D, tf), lambda i, f, te, tv: (layer, te[i], 0, chunk(i, f, tv))),
            pl.BlockSpec((None, None, tf, D), lambda i, f, te, tv: (layer, te[i], chunk(i, f, tv), 0)),
        ],
        out_specs=pl.BlockSpec((MOE_ROW_TILE, D), lambda i, f, te, tv: (i, 0)),
        scratch_shapes=[pltpu.VMEM((MOE_ROW_TILE, D), F32)],
    )
    return pl.pallas_call(
        _moe_ffn_kernel,
        name="moe_ffn",
        grid_spec=grid_spec,
        out_shape=jax.ShapeDtypeStruct((R, D), BF16),
        compiler_params=_cparams("parallel", "arbitrary"),
    )(tile_expert, tile_valid, h_sorted, wg, wu, wd)


def _moe_tail_kernel(x_ref, y0_ref, y1_ref, wt_ref, p_ref, g_ref, win_ref, wgate_ref, *rest):
    wt = wt_ref[...]
    x = x_ref[...] + wt[:, 0:1] * y0_ref[...].astype(F32) + wt[:, 1:2] * y1_ref[...].astype(F32)
    x = _ple(x, p_ref[...], g_ref[...], win_ref, wgate_ref)
    if len(rest) == 1:
        (o_ref,) = rest
    else:
        gn_ref, wn_ref, o_ref, qkv_ref = rest
        qkv_ref[...] = _dot(_rms(x, gn_ref[...]).astype(BF16), wn_ref[...]).astype(qkv_ref.dtype)
    o_ref[...] = x


def moe_layer_tail(x, y0, y1, wt, p, layer, gain, w_in, w_gate, next_proj=None):
    T, D = x.shape
    P = p.shape[2]
    row = lambda n: pl.BlockSpec((ROW_TILE, n), lambda i: (i, 0))
    ins = [x, y0, y1, wt, p, gain.reshape(1, D), w_in, w_gate]
    in_specs = [row(D), row(D), row(D), row(LANES), pl.BlockSpec((None, ROW_TILE, P), lambda i: (layer, i, 0)),
                _resident((1, D)), _resident((P, D)), _resident((D, D))]
    out_specs = row(D)
    out_shape = jax.ShapeDtypeStruct((T, D), F32)
    if next_proj is not None:
        gn, wn = next_proj
        N = wn.shape[1]
        ins += [gn.reshape(1, D), wn]
        in_specs += [_resident((1, D)), _resident((D, N))]
        out_specs = [out_specs, row(N)]
        out_shape = [out_shape, jax.ShapeDtypeStruct((T, N), BF16)]
    return pl.pallas_call(
        _moe_tail_kernel,
        name="moe_tail",
        grid=(T // ROW_TILE,),
        in_specs=in_specs,
        out_specs=out_specs,
        out_shape=out_shape,
        compiler_params=_cparams("parallel"),
    )(*ins)


def _rope_cos_sin(pos, dim, theta):
    inv = theta ** (-jnp.arange(0, dim, 2, dtype=F32) / dim)
    ang = pos.astype(F32)[:, None] * inv[None, :]
    return jnp.cos(ang), jnp.sin(ang)


def _fold_tables(gain_lanes, cos_lanes, sin_lanes, partner):
    return gain_lanes[None, :] * cos_lanes, gain_lanes[partner][None, :] * sin_lanes


def _partner_matrix(partner):
    m = np.zeros((LANES, LANES), np.float32)
    m[partner, np.arange(LANES)] = 1.0
    return jnp.asarray(m, dtype=BF16)


def _axial_tables(S, q_gain, k_gain):
    pos = jnp.arange(S)
    cr, sr = _rope_cos_sin(pos // GRID_W, AX_HEAD_DIM // 2, AX_THETA)
    cc, sc = _rope_cos_sin(pos % GRID_W, AX_HEAD_DIM // 2, AX_THETA)
    cos = jnp.concatenate([cr, cr, cc, cc], axis=1)
    sin = jnp.concatenate([-sr, sr, -sc, sc], axis=1)
    lane = np.arange(LANES)
    partner = np.where(lane % 64 < 32, lane + 32, lane - 32)
    return (_partner_matrix(partner),) + _fold_tables(q_gain, cos, sin, partner) + _fold_tables(k_gain, cos, sin, partner)


def _mla_tables(S, q_gain, k_gain):
    c, s = _rope_cos_sin(jnp.arange(S), MLA_ROPE, MLA_THETA)
    pad = LANES - MLA_QK
    cos = jnp.concatenate([jnp.ones((S, MLA_NOPE), F32), c, c, jnp.ones((S, pad), F32)], axis=1)
    sin = jnp.concatenate([jnp.zeros((S, MLA_NOPE), F32), -s, s, jnp.zeros((S, pad), F32)], axis=1)
    lane = np.arange(LANES)
    half = MLA_ROPE // 2
    partner = np.where((lane >= MLA_NOPE) & (lane < MLA_NOPE + half), lane + half,
                       np.where((lane >= MLA_NOPE + half) & (lane < MLA_QK), lane - half, lane))
    zpad = jnp.zeros((pad,), F32)
    gq = jnp.concatenate([q_gain, zpad])
    gk = jnp.concatenate([k_gain, zpad])
    return (_partner_matrix(partner),) + _fold_tables(gq, cos, sin, partner) + _fold_tables(gk, cos, sin, partner)


def _moe_layout(cnt, n_rt, n_tiles):
    tm = MOE_ROW_TILE
    cnt = cnt.reshape(n_rt, SUBLANES, LANES)[:, 0, :N_EXPERTS].astype(jnp.int32)
    chunk_rows = ((cnt + CHUNK_ALIGN - 1) // CHUNK_ALIGN) * CHUNK_ALIGN
    rt = jnp.arange(n_rt)
    before = jnp.sum(jnp.where((rt[None, :] < rt[:, None])[:, :, None], chunk_rows[None, :, :], 0), axis=1)
    region = ((jnp.sum(chunk_rows, axis=0) + tm - 1) // tm) * tm
    ex = jnp.arange(N_EXPERTS)
    ends = jnp.sum(jnp.where(ex[None, :] <= ex[:, None], region[None, :], 0), axis=1)
    chunk_base = (ends - region)[None, :] + before
    tile_start = jnp.arange(n_tiles, dtype=jnp.int32) * tm
    tile_expert = jnp.minimum(jnp.sum((tile_start[:, None] >= ends[None, :]).astype(jnp.int32), axis=1), N_EXPERTS - 1)
    tile_valid = (tile_start < ends[-1]).astype(jnp.int32)
    return chunk_rows.reshape(-1), chunk_base.reshape(-1), tile_expert, tile_valid


def kernel(x, p, attn_norm, ffn_norm, ple_norm, ple_w_in, ple_w_gate, mla_w_down, mla_q_norm, mla_w_uq, mla_kv_norm, mla_w_ukv, mla_q_gain, mla_k_gain, mla_w_o, swa_w_qkv, swa_q_gain, swa_k_gain, swa_sink, swa_w_o, ax_w_qkv, ax_q_gain, ax_k_gain, ax_w_o, ffn_w_gate, ffn_w_up, ffn_w_down, moe_w_router, moe_b_router, moe_w_gate, moe_w_up, moe_w_down):
    B, S, D = x.shape
    depth = p.shape[0]
    T = B * S
    xt = x.reshape(T, D)
    bf = lambda a: a.astype(BF16)
    n_rt = T // ROW_TILE
    n_moe_tiles = (TOP_K * T + n_rt * N_EXPERTS * (CHUNK_ALIGN - 1)) // MOE_ROW_TILE + N_EXPERTS
    slopes = jnp.asarray(2.0 ** (-8.0 * np.arange(1, SWA_HEADS + 1) / SWA_HEADS) * LOG2E, dtype=F32)
    p3 = p.reshape(depth, T, -1)
    moe_wg, moe_wu, moe_wd = bf(moe_w_gate), bf(moe_w_up), bf(moe_w_down)

    def plain_proj(i):
        if i >= depth or i % N_MIXERS == 0:
            return None
        w = swa_w_qkv if i % N_MIXERS == 1 else ax_w_qkv
        return attn_norm[i], bf(w[i // N_MIXERS])

    qkv_next = None
    for i in range(depth):
        kind = i % N_MIXERS
        j = i // N_MIXERS
        if kind == 0:
            wd = mla_w_down[j]
            zc = lambda n: jnp.zeros((D, n), F32)
            wd = jnp.concatenate([wd[:, :MLA_Q_RANK + MLA_KV_RANK], zc(MLA_NOPE), wd[:, MLA_Q_RANK + MLA_KV_RANK:],
                                  zc(LANES - MLA_QK)], axis=1)
            wuq = mla_w_uq[j].reshape(MLA_Q_RANK, MLA_HEADS, MLA_QK)
            wuq = jnp.pad(wuq, ((0, 0), (0, 0), (0, LANES - MLA_QK))).reshape(MLA_Q_RANK, MLA_HEADS * LANES)
            q, kv, kr = mla_proj(xt, attn_norm[i], bf(wd), mla_q_norm[j], mla_kv_norm[j], bf(wuq), bf(mla_w_ukv[j]))
            o = mla_attention(q, kv, kr, _mla_tables(S, mla_q_gain[j], mla_k_gain[j]), B, S)
            w_o = mla_w_o[j]
        elif kind == 1:
            qkv = qkv_next if qkv_next is not None else norm_proj(xt, *plain_proj(i))
            gq = jnp.tile(swa_q_gain[j], 2).reshape(1, LANES)
            gk = jnp.tile(swa_k_gain[j], 2).reshape(1, LANES)
            o = swa_attention(qkv, slopes, swa_sink[j].astype(F32) * LOG2E, gq, gk, B, S)
            w_o = swa_w_o[j]
        else:
            qkv = qkv_next if qkv_next is not None else norm_proj(xt, *plain_proj(i))
            o = axial_attention(qkv, _axial_tables(S, ax_q_gain[j], ax_k_gain[j]), B, S)
            w_o = ax_w_o[j]
        qkv_next = None
        f = i // 2
        if i % 2 == 0:
            xt = dense_layer_tail(xt, o, bf(w_o), ffn_norm[i], bf(ffn_w_gate[f]), bf(ffn_w_up[f]), bf(ffn_w_down[f]),
                                  p3, i, ple_norm[i], bf(ple_w_in[i]), bf(ple_w_gate[i]))
        else:
            wr = jnp.pad(moe_w_router[f], ((0, 0), (0, LANES - N_EXPERTS)))
            wr_hi = bf(wr)
            wr_lo = bf(wr - wr_hi.astype(F32))
            br = jnp.pad(moe_b_router[f].astype(F32), (0, LANES - N_EXPERTS)).reshape(1, LANES)
            xt, h, idx, wt, cnt = moe_router(xt, o, bf(w_o), ffn_norm[i], wr_hi, wr_lo, br)
            chunk_rows, chunk_base, tile_expert, tile_valid = _moe_layout(cnt, n_rt, n_moe_tiles)
            h_sorted, dest = moe_dispatch(h, idx, chunk_rows, chunk_base, n_moe_tiles * MOE_ROW_TILE)
            y = moe_ffn(h_sorted, tile_expert, tile_valid, moe_wg, moe_wu, moe_wd, f, 2)
            dest = dest.reshape(n_rt, SUBLANES, ROW_TILE)
            y0 = jnp.take(y, dest[:, 0, :].reshape(T), axis=0, mode="clip")
            y1 = jnp.take(y, dest[:, 1, :].reshape(T), axis=0, mode="clip")
            nxt = plain_proj(i + 1)
            out = moe_layer_tail(xt, y0, y1, wt, p3, i, ple_norm[i], bf(ple_w_in[i]), bf(ple_w_gate[i]), nxt)
            xt, qkv_next = out if nxt is not None else (out, None)
    return xt.reshape(B, S, D)
```

```python
import functools

import numpy as np
import jax
import jax.numpy as jnp
from jax import lax
from jax.experimental import pallas as pl
from jax.experimental.pallas import tpu as pltpu

F32 = jnp.float32
BF16 = jnp.bfloat16

EPS = 1e-6
GRID_W = 64
BLOCK_Q = 128

MLA_HEADS = 16
MLA_NOPE = 64
MLA_ROPE = 32
MLA_V = 64
MLA_Q_RANK = 256
MLA_KV_RANK = 128
MLA_THETA = 10000.0
MLA_QK = MLA_NOPE + MLA_ROPE

SWA_HEADS = 16
SWA_KV_HEADS = 4
SWA_HEAD_DIM = 64
SWA_WINDOW = 128

AX_HEADS = 8
AX_KV_HEADS = 4
AX_HEAD_DIM = 128
AX_THETA = 10000.0

N_EXPERTS = 8
TOP_K = 2
N_MIXERS = 3

LANES = 128
SUBLANES = 8
ROW_TILE = 512
MOE_ROW_TILE = 512
CHUNK_ALIGN = 2 * SUBLANES
ATTN_Q_TILE = 1024
FFN_SUB = 256
VMEM_LIMIT = 56 * 1024 * 1024
LOG2E = 1.4426950408889634


def _cparams(*sem):
    return pltpu.CompilerParams(dimension_semantics=sem, vmem_limit_bytes=VMEM_LIMIT)


def _rms(xf, gain):
    ms = jnp.mean(xf * xf, axis=-1, keepdims=True)
    return xf * lax.rsqrt(ms + EPS) * gain


def _dot(a, b):
    return jnp.dot(a, b, preferred_element_type=F32)


def _dot_nt(a, b):
    return lax.dot_general(a, b, (((1,), (1,)), ((), ())), preferred_element_type=F32)


def _resident(shape):
    return pl.BlockSpec(shape, lambda *_: (0,) * len(shape), pipeline_mode=pl.Buffered(1))


def _norm_proj_kernel(x_ref, g_ref, w_ref, o_ref):
    h = _rms(x_ref[...], g_ref[...]).astype(BF16)
    o_ref[...] = _dot(h, w_ref[...]).astype(o_ref.dtype)


def norm_proj(x, gain, w):
    T, D = x.shape
    N = w.shape[1]
    return pl.pallas_call(
        _norm_proj_kernel,
        name="norm_proj",
        grid=(T // ROW_TILE,),
        in_specs=[
            pl.BlockSpec((ROW_TILE, D), lambda i: (i, 0)),
            _resident((1, D)),
            _resident((D, N)),
        ],
        out_specs=pl.BlockSpec((ROW_TILE, N), lambda i: (i, 0)),
        out_shape=jax.ShapeDtypeStruct((T, N), BF16),
        compiler_params=_cparams("parallel"),
    )(x, gain.reshape(1, D), w)


def _mla_proj_kernel(x_ref, g_ref, wd_ref, qn_ref, kvn_ref, wuq_ref, wukv_ref, q_ref, kv_ref, kr_ref):
    h = _rms(x_ref[...], g_ref[...]).astype(BF16)
    down = _dot(h, wd_ref[...])
    cq = _rms(down[:, :MLA_Q_RANK], qn_ref[...]).astype(BF16)
    ckv = _rms(down[:, MLA_Q_RANK:MLA_Q_RANK + MLA_KV_RANK], kvn_ref[...]).astype(BF16)
    q_ref[...] = _dot(cq, wuq_ref[...]).astype(BF16)
    kv_ref[...] = _dot(ckv, wukv_ref[...]).astype(BF16)
    kr_ref[...] = down[:, MLA_Q_RANK + MLA_KV_RANK:].astype(BF16)


def mla_proj(x, gain, wd, qn, kvn, wuq, wukv):
    T, D = x.shape
    nd = wd.shape[1]
    nq = wuq.shape[1]
    nkv = wukv.shape[1]
    row = lambda n: pl.BlockSpec((ROW_TILE, n), lambda i: (i, 0))
    return pl.pallas_call(
        _mla_proj_kernel,
        name="mla_proj",
        grid=(T // ROW_TILE,),
        in_specs=[
            row(D),
            _resident((1, D)),
            _resident((D, nd)),
            _resident((1, MLA_Q_RANK)),
            _resident((1, MLA_KV_RANK)),
            _resident((MLA_Q_RANK, nq)),
            _resident((MLA_KV_RANK, nkv)),
        ],
        out_specs=[row(nq), row(nkv), row(LANES)],
        out_shape=[
            jax.ShapeDtypeStruct((T, nq), BF16),
            jax.ShapeDtypeStruct((T, nkv), BF16),
            jax.ShapeDtypeStruct((T, LANES), BF16),
        ],
        compiler_params=_cparams("parallel"),
    )(x, gain.reshape(1, D), wd, qn.reshape(1, -1), kvn.reshape(1, -1), wuq, wukv)


def _two_unit_pipeline(n, scores, finish, emit):
    scores(0, 0)
    for i in range(n):
        scores(i, 1)
        o0 = finish(i, 0)
        o1 = finish(i, 1)
        if i + 1 < n:
            scores(i + 1, 0)
        emit(i, o0, o1)


def _softmax_numerators(s_buf, extra_logit=None):
    m = jnp.max(s_buf[...], axis=-1, keepdims=True)
    if extra_logit is not None:
        m = jnp.maximum(m, extra_logit)
    return jnp.exp2(s_buf[...] - m).astype(BF16), m


def _norm_rope(x, perm_ref, a, b, inv_dim, extra):
    xf = x.astype(F32)
    c = lax.rsqrt(jnp.sum(xf * xf, axis=-1, keepdims=True) * inv_dim + EPS) * extra
    return (xf * a + _dot(x, perm_ref[...]) * b) * c


def _axial_attn_kernel(q_ref, k_ref, v_ref, perm_ref, aq_ref, bq_ref, ak_ref, bk_ref, o_ref, q_scr, k_scr, v_scr,
                       s0_scr, s1_scr, *, tq, scale):
    S = k_ref.shape[0]
    s_bufs = (s0_scr, s1_scr)
    inv_dim = 1.0 / AX_HEAD_DIM

    k_scr[...] = _norm_rope(k_ref[...], perm_ref, ak_ref[...], bk_ref[...], inv_dim, 1.0).astype(BF16)
    for u in range(2):
        q_scr[u] = _norm_rope(q_ref[:, u * LANES:(u + 1) * LANES], perm_ref, aq_ref[...], bq_ref[...], inv_dim,
                              scale * LOG2E).astype(BF16)
    v_scr[:, :LANES] = v_ref[...]
    v_scr[:, LANES:] = jnp.ones((S, LANES), BF16)

    def rows(i):
        return pl.ds(i * tq, tq)

    def scores(i, u):
        s_bufs[u][...] = _dot_nt(q_scr[u, rows(i), :], k_scr[...])

    def finish(i, u):
        p, _ = _softmax_numerators(s_bufs[u])
        o = _dot(p, v_scr[...])
        return o[:, :LANES] / o[:, LANES:]

    def emit(i, o0, o1):
        o_ref[rows(i), :LANES] = o0.astype(o_ref.dtype)
        o_ref[rows(i), LANES:] = o1.astype(o_ref.dtype)

    _two_unit_pipeline(S // tq, scores, finish, emit)


def axial_attention(qkv, tabs, B, S):
    R = AX_HEADS // AX_KV_HEADS
    assert R == 2
    tq = min(ATTN_Q_TILE, S)
    kern = functools.partial(_axial_attn_kernel, tq=tq, scale=AX_HEAD_DIM ** -0.5)
    tab = pl.BlockSpec((S, LANES), lambda b, g: (0, 0), pipeline_mode=pl.Buffered(1))
    return pl.pallas_call(
        kern,
        name="axial_attn",
        grid=(B, AX_KV_HEADS),
        in_specs=[
            pl.BlockSpec((S, R * LANES), lambda b, g: (b, g)),
            pl.BlockSpec((S, LANES), lambda b, g: (b, AX_HEADS + g)),
            pl.BlockSpec((S, LANES), lambda b, g: (b, AX_HEADS + AX_KV_HEADS + g)),
            _resident((LANES, LANES)), tab, tab, tab, tab,
        ],
        out_specs=pl.BlockSpec((S, R * LANES), lambda b, g: (b, g)),
        out_shape=jax.ShapeDtypeStruct((B * S, AX_HEADS * AX_HEAD_DIM), BF16),
        scratch_shapes=[
            pltpu.VMEM((R, S, LANES), BF16),
            pltpu.VMEM((S, LANES), BF16),
            pltpu.VMEM((S, 2 * LANES), BF16),
            pltpu.VMEM((tq, S), F32),
            pltpu.VMEM((tq, S), F32),
        ],
        compiler_params=_cparams("parallel", "parallel"),
    )(qkv, qkv, qkv, *tabs)


def _mla_attn_kernel(q_ref, kv_ref, kr_ref, perm_ref, aq_ref, bq_ref, ak_ref, bk_ref, o_ref, q_scr, k_scr, v_scr,
                     s0_scr, s1_scr, *, tq, scale):
    S = kv_ref.shape[0]
    s_bufs = (s0_scr, s1_scr)
    lane = lax.broadcasted_iota(jnp.int32, (1, LANES), 1)
    lo = lane < MLA_NOPE
    inv_dim = 1.0 / MLA_QK

    kr = kr_ref[...]
    krf = kr.astype(F32)
    rope_part = krf * ak_ref[...] + _dot(kr, perm_ref[...]) * bk_ref[...]
    ss_rope = jnp.sum(krf * krf, axis=-1, keepdims=True)
    for hh in range(2):
        kvh = kv_ref[:, hh * LANES:(hh + 1) * LANES].astype(F32)
        ss = jnp.sum(jnp.where(lo, kvh * kvh, 0.0), axis=-1, keepdims=True) + ss_rope
        k = jnp.where(lo, kvh * ak_ref[...], rope_part)
        k_scr[hh] = (k * lax.rsqrt(ss * inv_dim + EPS)).astype(BF16)
        vh = jnp.where(lo, pltpu.roll(kvh, MLA_V, 1), 1.0) if hh == 0 else jnp.where(lo, 1.0, kvh)
        v_scr[hh] = vh.astype(BF16)
        q_scr[hh] = _norm_rope(q_ref[:, hh * LANES:(hh + 1) * LANES], perm_ref, aq_ref[...], bq_ref[...], inv_dim,
                               scale * LOG2E).astype(BF16)

    def rows(i):
        return pl.ds(i * tq, tq)

    def scores(i, u):
        s_bufs[u][...] = _dot_nt(q_scr[u, rows(i), :], k_scr[u])

    def finish(i, u):
        p, _ = _softmax_numerators(s_bufs[u])
        o = _dot(p, v_scr[u])
        return o / pltpu.roll(o, MLA_V, 1)

    def emit(i, o0, o1):
        o_ref[rows(i), :] = jnp.where(lo, o0, o1).astype(o_ref.dtype)

    _two_unit_pipeline(S // tq, scores, finish, emit)


def mla_attention(q, kv, kr, tabs, B, S):
    tq = min(ATTN_Q_TILE, S)
    kern = functools.partial(_mla_attn_kernel, tq=tq, scale=MLA_QK ** -0.5)
    tab = pl.BlockSpec((S, LANES), lambda b, g: (0, 0), pipeline_mode=pl.Buffered(1))
    return pl.pallas_call(
        kern,
        name="mla_attn",
        grid=(B, MLA_HEADS // 2),
        in_specs=[
            pl.BlockSpec((S, 2 * LANES), lambda b, g: (b, g)),
            pl.BlockSpec((S, 2 * LANES), lambda b, g: (b, g)),
            pl.BlockSpec((S, LANES), lambda b, g: (b, 0)),
            _resident((LANES, LANES)), tab, tab, tab, tab,
        ],
        out_specs=pl.BlockSpec((S, LANES), lambda b, g: (b, g)),
        out_shape=jax.ShapeDtypeStruct((B * S, MLA_HEADS * MLA_V), BF16),
        scratch_shapes=[
            pltpu.VMEM((2, S, LANES), BF16),
            pltpu.VMEM((2, S, LANES), BF16),
            pltpu.VMEM((2, S, LANES), BF16),
            pltpu.VMEM((tq, S), F32),
            pltpu.VMEM((tq, S), F32),
        ],
        compiler_params=_cparams("parallel", "parallel"),
    )(q, kv, kr, *tabs)


def _swa_attn_kernel(sink_ref, q_ref, k_ref, v_ref, bias_ref, half_ref, swap_ref, gq_ref, gk_ref, o_ref,
                     q_scr, k_scr, v_scr, s0_scr, s1_scr, *, scale):
    S = k_ref.shape[0]
    span = BLOCK_Q + 2 * SWA_WINDOW
    R = SWA_HEADS // SWA_KV_HEADS
    rows_u = R * BLOCK_Q
    s_bufs = (s0_scr, s1_scr)
    pid = pl.program_id(0)
    lane = lax.broadcasted_iota(jnp.int32, (1, LANES), 1)
    lo = lane < SWA_HEAD_DIM
    hi = jnp.logical_not(lo)

    def seg_norm(x, gain):
        xf = x.astype(F32)
        sq = xf * xf
        sq_hi = sq.astype(BF16)
        sq_lo = (sq - sq_hi.astype(F32)).astype(BF16)
        ss = _dot(sq_hi, half_ref[...]) + _dot(sq_lo, half_ref[...])
        return xf * lax.rsqrt(ss * (1.0 / SWA_HEAD_DIM) + EPS) * gain

    k_scr[...] = seg_norm(k_ref[...], gk_ref[...]).astype(BF16)
    v = v_ref[...].astype(F32)
    v_scr[0] = jnp.where(lo, v, 1.0).astype(BF16)
    v_scr[1] = jnp.where(lo, 1.0, v).astype(BF16)
    for pb in range(R):
        e = pb // (R // 2)
        keep = lo if e == 0 else hi
        qp = seg_norm(q_ref[:, pb * LANES:(pb + 1) * LANES], gq_ref[...]) * (scale * LOG2E)
        qr = _dot(qp.astype(BF16), swap_ref[...])
        for i in range(2):
            qz = jnp.where(keep, qp if i == e else qr, 0.0).astype(BF16)
            r = (pb % (R // 2)) * 2 + i
            for j in range(S // BLOCK_Q):
                q_scr[e, j, r * BLOCK_Q:(r + 1) * BLOCK_Q, :] = qz[j * BLOCK_Q:(j + 1) * BLOCK_Q, :]

    head_of_row = lax.broadcasted_iota(jnp.int32, (rows_u, 1), 0) // BLOCK_Q

    def head_column(ref, e):
        col = jnp.zeros((rows_u, 1), F32)
        for r in range(R):
            col = jnp.where(head_of_row == r, ref[pid * 2 * R + e * R + r], col)
        return col

    sink_cols = [head_column(sink_ref, e) for e in range(2)]

    def rows(j):
        return pl.ds(j * BLOCK_Q, BLOCK_Q)

    def window(j):
        return min(max(j * BLOCK_Q - SWA_WINDOW, 0), S - span)

    def scores(j, e):
        start = window(j)
        bias = bias_ref[e, (j * BLOCK_Q - start) // SWA_WINDOW]
        s_bufs[e][...] = _dot_nt(q_scr[e, j], k_scr[pl.ds(start, span), :]) + bias

    def finish(j, e):
        p, m = _softmax_numerators(s_bufs[e], extra_logit=sink_cols[e])
        o = _dot(p, v_scr[e, pl.ds(window(j), span), :])
        den = pltpu.roll(o, SWA_HEAD_DIM, 1) + jnp.exp2(sink_cols[e] - m)
        return o / den

    def emit(j, o0, o1):
        for e, o in ((0, o0), (1, o1)):
            orot = pltpu.roll(o, SWA_HEAD_DIM, 1)
            for k in range(R // 2):
                pb = e * (R // 2) + k
                even = (o if e == 0 else orot)[2 * k * BLOCK_Q:(2 * k + 1) * BLOCK_Q]
                odd = (o if e == 1 else orot)[(2 * k + 1) * BLOCK_Q:(2 * k + 2) * BLOCK_Q]
                o_ref[rows(j), pb * LANES:(pb + 1) * LANES] = jnp.where(lo, even, odd).astype(o_ref.dtype)

    _two_unit_pipeline(S // BLOCK_Q, scores, finish, emit)


def swa_attention(qkv, slopes, sink, gq, gk, B, S):
    n_steps = SWA_KV_HEADS // 2
    R = SWA_HEADS // SWA_KV_HEADS
    span = BLOCK_Q + 2 * SWA_WINDOW
    qw = SWA_HEADS * SWA_HEAD_DIM // n_steps
    kbase = SWA_HEADS * SWA_HEAD_DIM // LANES
    smem = pl.BlockSpec(memory_space=pltpu.SMEM)
    gain = pl.BlockSpec((1, LANES), lambda g, b: (0, 0))
    kern = functools.partial(_swa_attn_kernel, scale=SWA_HEAD_DIM ** -0.5)
    lane = np.arange(LANES)
    half_ones = jnp.asarray(lane[:, None] // SWA_HEAD_DIM == lane[None, :] // SWA_HEAD_DIM, dtype=BF16)
    swap = _partner_matrix((lane + SWA_HEAD_DIM) % LANES)
    t_s = np.arange(BLOCK_Q)[:, None] - np.arange(span)[None, :]
    dist = np.abs(np.stack([t_s + c * SWA_WINDOW for c in range(3)]))
    bias = jnp.where(jnp.asarray(dist <= SWA_WINDOW)[None], -slopes[:, None, None, None] * jnp.asarray(dist, F32)[None],
                     -jnp.inf)
    bias = bias.reshape(n_steps, 2, R, 3, BLOCK_Q, span).transpose(0, 1, 3, 2, 4, 5)
    bias = bias.reshape(n_steps, 2, 3, R * BLOCK_Q, span)
    return pl.pallas_call(
        kern,
        name="swa_attn",
        grid=(n_steps, B),
        in_specs=[
            smem,
            pl.BlockSpec((S, qw), lambda g, b: (b, g)),
            pl.BlockSpec((S, LANES), lambda g, b: (b, kbase + g)),
            pl.BlockSpec((S, LANES), lambda g, b: (b, kbase + n_steps + g)),
            pl.BlockSpec((None, 2, 3, R * BLOCK_Q, span), lambda g, b: (g, 0, 0, 0, 0)),
            _resident((LANES, LANES)), _resident((LANES, LANES)),
            gain, gain,
        ],
        out_specs=pl.BlockSpec((S, qw), lambda g, b: (b, g)),
        out_shape=jax.ShapeDtypeStruct((B * S, SWA_HEADS * SWA_HEAD_DIM), BF16),
        scratch_shapes=[
            pltpu.VMEM((2, S // BLOCK_Q, R * BLOCK_Q, LANES), BF16),
            pltpu.VMEM((S, LANES), BF16),
            pltpu.VMEM((2, S, LANES), BF16),
            pltpu.VMEM((R * BLOCK_Q, span), F32),
            pltpu.VMEM((R * BLOCK_Q, span), F32),
        ],
        compiler_params=_cparams("parallel", "parallel"),
    )(sink, qkv, qkv, qkv, bias, half_ones, swap, gq, gk)


def _swiglu_accumulate(h, wg_ref, wu_ref, wd_ref, acc_ref):
    for c in range(wg_ref.shape[1] // FFN_SUB):
        sl = slice(c * FFN_SUB, (c + 1) * FFN_SUB)
        g = _dot(h, wg_ref[:, sl])
        u = _dot(h, wu_ref[:, sl])
        a = (g * jax.nn.sigmoid(g) * u).astype(BF16)
        acc_ref[...] += _dot(a, wd_ref[sl, :])


def _ple(x, p, gain, win_ref, wgate_ref):
    gate = jax.nn.sigmoid(_dot(_rms(x, gain).astype(BF16), wgate_ref[...]))
    return x + _dot(p.astype(BF16), win_ref[...]) * gate


def _dense_tail_kernel(x_ref, a_ref, wo_ref, g_ref, wg_ref, wu_ref, wd_ref, p_ref, gp_ref, win_ref, wgate_ref, o_ref,
                       acc_scr):
    x = x_ref[...] + _dot(a_ref[...], wo_ref[...])
    acc_scr[...] = x
    _swiglu_accumulate(_rms(x, g_ref[...]).astype(BF16), wg_ref, wu_ref, wd_ref, acc_scr)
    o_ref[...] = _ple(acc_scr[...], p_ref[...], gp_ref[...], win_ref, wgate_ref)


def dense_layer_tail(x, a, w_o, gain, wg, wu, wd, p, layer, ple_gain, w_in, w_gate):
    T, D = x.shape
    K = a.shape[1]
    Fd = wg.shape[1]
    P = p.shape[2]
    row = lambda n: pl.BlockSpec((ROW_TILE, n), lambda i: (i, 0))
    return pl.pallas_call(
        _dense_tail_kernel,
        name="dense_tail",
        grid=(T // ROW_TILE,),
        in_specs=[
            row(D), row(K), _resident((K, D)),
            _resident((1, D)), _resident((D, Fd)), _resident((D, Fd)), _resident((Fd, D)),
            pl.BlockSpec((None, ROW_TILE, P), lambda i: (layer, i, 0)),
            _resident((1, D)), _resident((P, D)), _resident((D, D)),
        ],
        out_specs=row(D),
        out_shape=jax.ShapeDtypeStruct((T, D), F32),
        scratch_shapes=[pltpu.VMEM((ROW_TILE, D), F32)],
        compiler_params=_cparams("parallel"),
    )(x, a, w_o, gain.reshape(1, D), wg, wu, wd, p, ple_gain.reshape(1, D), w_in, w_gate)


def _router_kernel(x_ref, a_ref, wo_ref, g_ref, whi_ref, wlo_ref, b_ref, x1_ref, h_ref, idx_ref, wt_ref, cnt_ref):
    x1 = x_ref[...] + _dot(a_ref[...], wo_ref[...])
    x1_ref[...] = x1
    hf = _rms(x1, g_ref[...])
    h_hi = hf.astype(BF16)
    h_lo = (hf - h_hi.astype(F32)).astype(BF16)
    h_ref[...] = h_hi
    logits = _dot(h_hi, whi_ref[...]) + _dot(h_hi, wlo_ref[...]) + _dot(h_lo, whi_ref[...]) + b_ref[...]
    lane = lax.broadcasted_iota(jnp.int32, logits.shape, 1)
    logits = jnp.where(lane < N_EXPERTS, logits, -jnp.inf)
    m1 = jnp.max(logits, axis=-1, keepdims=True)
    i1 = jnp.min(jnp.where(logits == m1, lane, LANES), axis=-1, keepdims=True)
    rest = jnp.where(lane == i1, -jnp.inf, logits)
    m2 = jnp.max(rest, axis=-1, keepdims=True)
    i2 = jnp.min(jnp.where(rest == m2, lane, LANES), axis=-1, keepdims=True)
    e2 = jnp.exp(m2 - m1)
    w1 = 1.0 / (1.0 + e2)
    w2 = e2 / (1.0 + e2)
    wt_ref[...] = jnp.where(lane == 0, w1, jnp.where(lane == 1, w2, 0.0))
    onehot = jnp.where(jnp.logical_or(lane == i1, lane == i2), 1.0, 0.0)
    tm = onehot.shape[0]
    earlier = (lax.broadcasted_iota(jnp.int32, (tm, tm), 0) > lax.broadcasted_iota(jnp.int32, (tm, tm), 1))
    prefix = _dot(jnp.where(earlier, 1.0, 0.0).astype(BF16), onehot.astype(BF16))
    r1 = jnp.sum(jnp.where(lane == i1, prefix, 0.0), axis=-1, keepdims=True).astype(jnp.int32)
    r2 = jnp.sum(jnp.where(lane == i2, prefix, 0.0), axis=-1, keepdims=True).astype(jnp.int32)
    idx = jnp.where(lane == 0, i1, jnp.where(lane == 1, i2, jnp.where(lane == 2, r1, jnp.where(lane == 3, r2, 0))))
    idx_ref[...] = idx.T[:SUBLANES, :]
    cnt_ref[...] = jnp.broadcast_to(jnp.sum(onehot, axis=0, keepdims=True), cnt_ref.shape)


def moe_router(x, a, w_o, gain, w_hi, w_lo, bias):
    T, D = x.shape
    K = a.shape[1]
    row = lambda n: pl.BlockSpec((ROW_TILE, n), lambda i: (i, 0))
    return pl.pallas_call(
        _router_kernel,
        name="moe_router",
        grid=(T // ROW_TILE,),
        in_specs=[row(D), row(K), _resident((K, D)), _resident((1, D)), _resident((D, LANES)), _resident((D, LANES)),
                  _resident((1, LANES))],
        out_specs=[row(D), row(D), pl.BlockSpec((SUBLANES, ROW_TILE), lambda i: (i, 0)), row(LANES),
                   pl.BlockSpec((SUBLANES, LANES), lambda i: (i, 0))],
        out_shape=[
            jax.ShapeDtypeStruct((T, D), F32),
            jax.ShapeDtypeStruct((T, D), BF16),
            jax.ShapeDtypeStruct((T // ROW_TILE * SUBLANES, ROW_TILE), jnp.int32),
            jax.ShapeDtypeStruct((T, LANES), F32),
            jax.ShapeDtypeStruct((T // ROW_TILE * SUBLANES, LANES), F32),
        ],
        compiler_params=_cparams("parallel"),
    )(x, a, w_o, gain.reshape(1, D), w_hi, w_lo, bias)


def _moe_dispatch_kernel(n_ref, base_ref, h_ref, idx_ref, zeros_ref, hs_ref, dest_ref, loc_scr, sem):
    del zeros_ref
    i = pl.program_id(0)
    e0, e1, r0, r1 = (idx_ref[k:k + 1, :] for k in range(4))
    slot0, slot1, dest0, dest1 = r0, r1, r0, r1
    offs = []
    off = jnp.int32(0)
    for e in range(N_EXPERTS):
        offs.append(off)
        base = base_ref[i * N_EXPERTS + e]
        slot0 = slot0 + jnp.where(e0 == e, off, 0)
        slot1 = slot1 + jnp.where(e1 == e, off, 0)
        dest0 = dest0 + jnp.where(e0 == e, base, 0)
        dest1 = dest1 + jnp.where(e1 == e, base, 0)
        off = off + n_ref[i * N_EXPERTS + e]
    row = lax.broadcasted_iota(jnp.int32, (SUBLANES, e0.shape[1]), 0)
    dest_ref[...] = jnp.where(row == 0, dest0, jnp.where(row == 1, dest1, 0))
    slot = lax.broadcasted_iota(jnp.int32, (loc_scr.shape[0], e0.shape[1]), 0)
    perm = jnp.where(jnp.logical_or(slot == slot0, slot == slot1), 1.0, 0.0).astype(BF16)
    loc_scr[...] = _dot(perm, h_ref[...]).astype(BF16)

    def piece(src_row, dst_row):
        return pltpu.make_async_copy(loc_scr.at[pl.ds(src_row, CHUNK_ALIGN), :],
                                     hs_ref.at[pl.ds(dst_row, CHUNK_ALIGN), :], sem)

    for e in range(N_EXPERTS):
        base = base_ref[i * N_EXPERTS + e]

        def start(g, carry, e=e, base=base):
            piece(pl.multiple_of(offs[e] + g * CHUNK_ALIGN, CHUNK_ALIGN),
                  pl.multiple_of(base + g * CHUNK_ALIGN, CHUNK_ALIGN)).start()
            return carry

        lax.fori_loop(0, n_ref[i * N_EXPERTS + e] // CHUNK_ALIGN, start, 0)

    def wait(g, carry):
        piece(0, 0).wait()
        return carry

    lax.fori_loop(0, off // CHUNK_ALIGN, wait, 0)


def moe_dispatch(h, idx, chunk_rows, chunk_base, n_rows):
    T, D = h.shape
    n_rt = T // ROW_TILE
    loc_rows = TOP_K * ROW_TILE + N_EXPERTS * CHUNK_ALIGN
    grid_spec = pltpu.PrefetchScalarGridSpec(
        num_scalar_prefetch=2,
        grid=(n_rt,),
        in_specs=[
            pl.BlockSpec((ROW_TILE, D), lambda i, n, b: (i, 0)),
            pl.BlockSpec((SUBLANES, ROW_TILE), lambda i, n, b: (i, 0)),
            pl.BlockSpec(memory_space=pl.ANY),
        ],
        out_specs=[
            pl.BlockSpec(memory_space=pl.ANY),
            pl.BlockSpec((SUBLANES, ROW_TILE), lambda i, n, b: (i, 0)),
        ],
        scratch_shapes=[pltpu.VMEM((loc_rows, D), BF16), pltpu.SemaphoreType.DMA],
    )
    return pl.pallas_call(
        _moe_dispatch_kernel,
        name="moe_dispatch",
        grid_spec=grid_spec,
        out_shape=[
            jax.ShapeDtypeStruct((n_rows, D), BF16),
            jax.ShapeDtypeStruct((n_rt * SUBLANES, ROW_TILE), jnp.int32),
        ],
        input_output_aliases={4: 0},
        compiler_params=_cparams("arbitrary"),
    )(chunk_rows, chunk_base, h, idx, jnp.zeros((n_rows, D), BF16))


def _moe_ffn_kernel(te_ref, tv_ref, h_ref, wg_ref, wu_ref, wd_ref, o_ref, acc_scr):
    i = pl.program_id(0)
    f = pl.program_id(1)
    last = pl.num_programs(1) - 1

    @pl.when(f == 0)
    def _():
        acc_scr[...] = jnp.zeros_like(acc_scr)

    @pl.when(tv_ref[i] > 0)
    def _():
        _swiglu_accumulate(h_ref[...], wg_ref, wu_ref, wd_ref, acc_scr)

    @pl.when(f == last)
    def _():
        o_ref[...] = acc_scr[...].astype(o_ref.dtype)


def moe_ffn(h_sorted, tile_expert, tile_valid, wg, wu, wd, layer, n_chunks):
    R, D = h_sorted.shape
    Fe = wg.shape[3]
    tf = Fe // n_chunks

    def chunk(i, f, tv):
        return jnp.where(tv[i] > 0, f, n_chunks - 1)

    grid_spec = pltpu.PrefetchScalarGridSpec(
        num_scalar_prefetch=2,
        grid=(R // MOE_ROW_TILE, n_chunks),
        in_specs=[
            pl.BlockSpec((MOE_ROW_TILE, D), lambda i, f, te, tv: (i, 0)),
            pl.BlockSpec((None, None, D, tf), lambda i, f, te, tv: (layer, te[i], 0, chunk(i, f, tv))),
            pl.BlockSpec((None, None, D, tf), lambda i, f, te, tv: (layer, te[i], 0, chunk(i, f, tv))),
            pl.BlockSpec((None, None, tf, D), lambda i, f, te, tv: (layer, te[i], chunk(i, f, tv), 0)),
        ],
        out_specs=pl.BlockSpec((MOE_ROW_TILE, D), lambda i, f, te, tv: (i, 0)),
        scratch_shapes=[pltpu.VMEM((MOE_ROW_TILE, D), F32)],
    )
    return pl.pallas_call(
        _moe_ffn_kernel,
        name="moe_ffn",
        grid_spec=grid_spec,
        out_shape=jax.ShapeDtypeStruct((R, D), BF16),
        compiler_params=_cparams("parallel", "arbitrary"),
    )(tile_expert, tile_valid, h_sorted, wg, wu, wd)


def _moe_tail_kernel(x_ref, y0_ref, y1_ref, wt_ref, p_ref, g_ref, win_ref, wgate_ref, *rest):
    wt = wt_ref[...]
    x = x_ref[...] + wt[:, 0:1] * y0_ref[...].astype(F32) + wt[:, 1:2] * y1_ref[...].astype(F32)
    x = _ple(x, p_ref[...], g_ref[...], win_ref, wgate_ref)
    if len(rest) == 1:
        (o_ref,) = rest
    else:
        gn_ref, wn_ref, o_ref, qkv_ref = rest
        qkv_ref[...] = _dot(_rms(x, gn_ref[...]).astype(BF16), wn_ref[...]).astype(qkv_ref.dtype)
    o_ref[...] = x


def moe_layer_tail(x, y0, y1, wt, p, layer, gain, w_in, w_gate, next_proj=None):
    T, D = x.shape
    P = p.shape[2]
    row = lambda n: pl.BlockSpec((ROW_TILE, n), lambda i: (i, 0))
    ins = [x, y0, y1, wt, p, gain.reshape(1, D), w_in, w_gate]
    in_specs = [row(D), row(D), row(D), row(LANES), pl.BlockSpec((None, ROW_TILE, P), lambda i: (layer, i, 0)),
                _resident((1, D)), _resident((P, D)), _resident((D, D))]
    out_specs = row(D)
    out_shape = jax.ShapeDtypeStruct((T, D), F32)
    if next_proj is not None:
        gn, wn = next_proj
        N = wn.shape[1]
        ins += [gn.reshape(1, D), wn]
        in_specs += [_resident((1, D)), _resident((D, N))]
        out_specs = [out_specs, row(N)]
        out_shape = [out_shape, jax.ShapeDtypeStruct((T, N), BF16)]
    return pl.pallas_call(
        _moe_tail_kernel,
        name="moe_tail",
        grid=(T // ROW_TILE,),
        in_specs=in_specs,
        out_specs=out_specs,
        out_shape=out_shape,
        compiler_params=_cparams("parallel"),
    )(*ins)


def _rope_cos_sin(pos, dim, theta):
    inv = theta ** (-jnp.arange(0, dim, 2, dtype=F32) / dim)
    ang = pos.astype(F32)[:, None] * inv[None, :]
    return jnp.cos(ang), jnp.sin(ang)


def _fold_tables(gain_lanes, cos_lanes, sin_lanes, partner):
    return gain_lanes[None, :] * cos_lanes, gain_lanes[partner][None, :] * sin_lanes


def _partner_matrix(partner):
    m = np.zeros((LANES, LANES), np.float32)
    m[partner, np.arange(LANES)] = 1.0
    return jnp.asarray(m, dtype=BF16)


def _axial_tables(S, q_gain, k_gain):
    pos = jnp.arange(S)
    cr, sr = _rope_cos_sin(pos // GRID_W, AX_HEAD_DIM // 2, AX_THETA)
    cc, sc = _rope_cos_sin(pos % GRID_W, AX_HEAD_DIM // 2, AX_THETA)
    cos = jnp.concatenate([cr, cr, cc, cc], axis=1)
    sin = jnp.concatenate([-sr, sr, -sc, sc], axis=1)
    lane = np.arange(LANES)
    partner = np.where(lane % 64 < 32, lane + 32, lane - 32)
    return (_partner_matrix(partner),) + _fold_tables(q_gain, cos, sin, partner) + _fold_tables(k_gain, cos, sin, partner)


def _mla_tables(S, q_gain, k_gain):
    c, s = _rope_cos_sin(jnp.arange(S), MLA_ROPE, MLA_THETA)
    pad = LANES - MLA_QK
    cos = jnp.concatenate([jnp.ones((S, MLA_NOPE), F32), c, c, jnp.ones((S, pad), F32)], axis=1)
    sin = jnp.concatenate([jnp.zeros((S, MLA_NOPE), F32), -s, s, jnp.zeros((S, pad), F32)], axis=1)
    lane = np.arange(LANES)
    half = MLA_ROPE // 2
    partner = np.where((lane >= MLA_NOPE) & (lane < MLA_NOPE + half), lane + half,
                       np.where((lane >= MLA_NOPE + half) & (lane < MLA_QK), lane - half, lane))
    zpad = jnp.zeros((pad,), F32)
    gq = jnp.concatenate([q_gain, zpad])
    gk = jnp.concatenate([k_gain, zpad])
    return (_partner_matrix(partner),) + _fold_tables(gq, cos, sin, partner) + _fold_tables(gk, cos, sin, partner)


def _moe_layout(cnt, n_rt, n_tiles):
    tm = MOE_ROW_TILE
    cnt = cnt.reshape(n_rt, SUBLANES, LANES)[:, 0, :N_EXPERTS].astype(jnp.int32)
    chunk_rows = ((cnt + CHUNK_ALIGN - 1) // CHUNK_ALIGN) * CHUNK_ALIGN
    rt = jnp.arange(n_rt)
    before = jnp.sum(jnp.where((rt[None, :] < rt[:, None])[:, :, None], chunk_rows[None, :, :], 0), axis=1)
    region = ((jnp.sum(chunk_rows, axis=0) + tm - 1) // tm) * tm
    ex = jnp.arange(N_EXPERTS)
    ends = jnp.sum(jnp.where(ex[None, :] <= ex[:, None], region[None, :], 0), axis=1)
    chunk_base = (ends - region)[None, :] + before
    tile_start = jnp.arange(n_tiles, dtype=jnp.int32) * tm
    tile_expert = jnp.minimum(jnp.sum((tile_start[:, None] >= ends[None, :]).astype(jnp.int32), axis=1), N_EXPERTS - 1)
    tile_valid = (tile_start < ends[-1]).astype(jnp.int32)
    return chunk_rows.reshape(-1), chunk_base.reshape(-1), tile_expert, tile_valid


def kernel(x, p, attn_norm, ffn_norm, ple_norm, ple_w_in, ple_w_gate, mla_w_down, mla_q_norm, mla_w_uq, mla_kv_norm, mla_w_ukv, mla_q_gain, mla_k_gain, mla_w_o, swa_w_qkv, swa_q_gain, swa_k_gain, swa_sink, swa_w_o, ax_w_qkv, ax_q_gain, ax_k_gain, ax_w_o, ffn_w_gate, ffn_w_up, ffn_w_down, moe_w_router, moe_b_router, moe_w_gate, moe_w_up, moe_w_down):
    B, S, D = x.shape
    depth = p.shape[0]
    T = B * S
    xt = x.reshape(T, D)
    bf = lambda a: a.astype(BF16)
    n_rt = T // ROW_TILE
    n_moe_tiles = (TOP_K * T + n_rt * N_EXPERTS * (CHUNK_ALIGN - 1)) // MOE_ROW_TILE + N_EXPERTS
    slopes = jnp.asarray(2.0 ** (-8.0 * np.arange(1, SWA_HEADS + 1) / SWA_HEADS) * LOG2E, dtype=F32)
    p3 = p.reshape(depth, T, -1)
    moe_wg, moe_wu, moe_wd = bf(moe_w_gate), bf(moe_w_up), bf(moe_w_down)

    def plain_proj(i):
        if i >= depth or i % N_MIXERS == 0:
            return None
        w = swa_w_qkv if i % N_MIXERS == 1 else ax_w_qkv
        return attn_norm[i], bf(w[i // N_MIXERS])

    qkv_next = None
    for i in range(depth):
        kind = i % N_MIXERS
        j = i // N_MIXERS
        if kind == 0:
            wd = mla_w_down[j]
            zc = lambda n: jnp.zeros((D, n), F32)
            wd = jnp.concatenate([wd[:, :MLA_Q_RANK + MLA_KV_RANK], zc(MLA_NOPE), wd[:, MLA_Q_RANK + MLA_KV_RANK:],
                                  zc(LANES - MLA_QK)], axis=1)
            wuq = mla_w_uq[j].reshape(MLA_Q_RANK, MLA_HEADS, MLA_QK)
            wuq = jnp.pad(wuq, ((0, 0), (0, 0), (0, LANES - MLA_QK))).reshape(MLA_Q_RANK, MLA_HEADS * LANES)
            q, kv, kr = mla_proj(xt, attn_norm[i], bf(wd), mla_q_norm[j], mla_kv_norm[j], bf(wuq), bf(mla_w_ukv[j]))
            o = mla_attention(q, kv, kr, _mla_tables(S, mla_q_gain[j], mla_k_gain[j]), B, S)
            w_o = mla_w_o[j]
        elif kind == 1:
            qkv = qkv_next if qkv_next is not None else norm_proj(xt, *plain_proj(i))
            gq = jnp.tile(swa_q_gain[j], 2).reshape(1, LANES)
            gk = jnp.tile(swa_k_gain[j], 2).reshape(1, LANES)
            o = swa_attention(qkv, slopes, swa_sink[j].astype(F32) * LOG2E, gq, gk, B, S)
            w_o = swa_w_o[j]
        else:
            qkv = qkv_next if qkv_next is not None else norm_proj(xt, *plain_proj(i))
            o = axial_attention(qkv, _axial_tables(S, ax_q_gain[j], ax_k_gain[j]), B, S)
            w_o = ax_w_o[j]
        qkv_next = None
        f = i // 2
        if i % 2 == 0:
            xt = dense_layer_tail(xt, o, bf(w_o), ffn_norm[i], bf(ffn_w_gate[f]), bf(ffn_w_up[f]), bf(ffn_w_down[f]),
                                  p3, i, ple_norm[i], bf(ple_w_in[i]), bf(ple_w_gate[i]))
        else:
            wr = jnp.pad(moe_w_router[f], ((0, 0), (0, LANES - N_EXPERTS)))
            wr_hi = bf(wr)
            wr_lo = bf(wr - wr_hi.astype(F32))
            br = jnp.pad(moe_b_router[f].astype(F32), (0, LANES - N_EXPERTS)).reshape(1, LANES)
            xt, h, idx, wt, cnt = moe_router(xt, o, bf(w_o), ffn_norm[i], wr_hi, wr_lo, br)
            chunk_rows, chunk_base, tile_expert, tile_valid = _moe_layout(cnt, n_rt, n_moe_tiles)
            h_sorted, dest = moe_dispatch(h, idx, chunk_rows, chunk_base, n_moe_tiles * MOE_ROW_TILE)
            y = moe_ffn(h_sorted, tile_expert, tile_valid, moe_wg, moe_wu, moe_wd, f, 2)
            dest = dest.reshape(n_rt, SUBLANES, ROW_TILE)
            y0 = jnp.take(y, dest[:, 0, :].reshape(T), axis=0, mode="clip")
            y1 = jnp.take(y, dest[:, 1, :].reshape(T), axis=0, mode="clip")
            nxt = plain_proj(i + 1)
            out = moe_layer_tail(xt, y0, y1, wt, p3, i, ple_norm[i], bf(ple_w_in[i]), bf(ple_w_gate[i]), nxt)
            xt, qkv_next = out if nxt is not None else (out, None)
    return xt.reshape(B, S, D)
```

```python
import functools

import numpy as np
import jax
import jax.numpy as jnp
from jax import lax
from jax.experimental import pallas as pl
from jax.experimental.pallas import tpu as pltpu

F32 = jnp.float32
BF16 = jnp.bfloat16

EPS = 1e-6
GRID_W = 64
BLOCK_Q = 128

MLA_HEADS = 16
MLA_NOPE = 64
MLA_ROPE = 32
MLA_V = 64
MLA_Q_RANK = 256
MLA_KV_RANK = 128
MLA_THETA = 10000.0
MLA_QK = MLA_NOPE + MLA_ROPE

SWA_HEADS = 16
SWA_KV_HEADS = 4
SWA_HEAD_DIM = 64
SWA_WINDOW = 128

AX_HEADS = 8
AX_KV_HEADS = 4
AX_HEAD_DIM = 128
AX_THETA = 10000.0

N_EXPERTS = 8
TOP_K = 2
N_MIXERS = 3

LANES = 128
SUBLANES = 8
ROW_TILE = 512
MOE_ROW_TILE = 512
CHUNK_ALIGN = 2 * SUBLANES
ATTN_Q_TILE = 1024
FFN_SUB = 256
VMEM_LIMIT = 56 * 1024 * 1024
LOG2E = 1.4426950408889634


def _cparams(*sem):
    return pltpu.CompilerParams(dimension_semantics=sem, vmem_limit_bytes=VMEM_LIMIT)


def _rms(xf, gain):
    ms = jnp.mean(xf * xf, axis=-1, keepdims=True)
    return xf * lax.rsqrt(ms + EPS) * gain


def _dot(a, b):
    return jnp.dot(a, b, preferred_element_type=F32)


def _dot_nt(a, b):
    return lax.dot_general(a, b, (((1,), (1,)), ((), ())), preferred_element_type=F32)


def _resident(shape):
    return pl.BlockSpec(shape, lambda *_: (0,) * len(shape), pipeline_mode=pl.Buffered(1))


def _norm_proj_kernel(x_ref, g_ref, w_ref, o_ref):
    h = _rms(x_ref[...], g_ref[...]).astype(BF16)
    o_ref[...] = _dot(h, w_ref[...]).astype(o_ref.dtype)


def norm_proj(x, gain, w):
    T, D = x.shape
    N = w.shape[1]
    return pl.pallas_call(
        _norm_proj_kernel,
        name="norm_proj",
        grid=(T // ROW_TILE,),
        in_specs=[
            pl.BlockSpec((ROW_TILE, D), lambda i: (i, 0)),
            _resident((1, D)),
            _resident((D, N)),
        ],
        out_specs=pl.BlockSpec((ROW_TILE, N), lambda i: (i, 0)),
        out_shape=jax.ShapeDtypeStruct((T, N), BF16),
        compiler_params=_cparams("parallel"),
    )(x, gain.reshape(1, D), w)


def _mla_proj_kernel(x_ref, g_ref, wd_ref, qn_ref, kvn_ref, wuq_ref, wukv_ref, q_ref, kv_ref, kr_ref):
    h = _rms(x_ref[...], g_ref[...]).astype(BF16)
    down = _dot(h, wd_ref[...])
    cq = _rms(down[:, :MLA_Q_RANK], qn_ref[...]).astype(BF16)
    ckv = _rms(down[:, MLA_Q_RANK:MLA_Q_RANK + MLA_KV_RANK], kvn_ref[...]).astype(BF16)
    q_ref[...] = _dot(cq, wuq_ref[...]).astype(BF16)
    kv_ref[...] = _dot(ckv, wukv_ref[...]).astype(BF16)
    kr_ref[...] = down[:, MLA_Q_RANK + MLA_KV_RANK:].astype(BF16)


def mla_proj(x, gain, wd, qn, kvn, wuq, wukv):
    T, D = x.shape
    nd = wd.shape[1]
    nq = wuq.shape[1]
    nkv = wukv.shape[1]
    row = lambda n: pl.BlockSpec((ROW_TILE, n), lambda i: (i, 0))
    return pl.pallas_call(
        _mla_proj_kernel,
        name="mla_proj",
        grid=(T // ROW_TILE,),
        in_specs=[
            row(D),
            _resident((1, D)),
            _resident((D, nd)),
            _resident((1, MLA_Q_RANK)),
            _resident((1, MLA_KV_RANK)),
            _resident((MLA_Q_RANK, nq)),
            _resident((MLA_KV_RANK, nkv)),
        ],
        out_specs=[row(nq), row(nkv), row(LANES)],
        out_shape=[
            jax.ShapeDtypeStruct((T, nq), BF16),
            jax.ShapeDtypeStruct((T, nkv), BF16),
            jax.ShapeDtypeStruct((T, LANES), BF16),
        ],
        compiler_params=_cparams("parallel"),
    )(x, gain.reshape(1, D), wd, qn.reshape(1, -1), kvn.reshape(1, -1), wuq, wukv)


def _two_unit_pipeline(n, scores, finish, emit):
    scores(0, 0)
    for i in range(n):
        scores(i, 1)
        o0 = finish(i, 0)
        o1 = finish(i, 1)
        if i + 1 < n:
            scores(i + 1, 0)
        emit(i, o0, o1)


def _softmax_numerators(s_buf, extra_logit=None):
    m = jnp.max(s_buf[...], axis=-1, keepdims=True)
    if extra_logit is not None:
        m = jnp.maximum(m, extra_logit)
    return jnp.exp2(s_buf[...] - m).astype(BF16), m


def _norm_rope(x, perm_ref, a, b, inv_dim, extra):
    xf = x.astype(F32)
    c = lax.rsqrt(jnp.sum(xf * xf, axis=-1, keepdims=True) * inv_dim + EPS) * extra
    return (xf * a + _dot(x, perm_ref[...]) * b) * c


def _axial_attn_kernel(q_ref, k_ref, v_ref, perm_ref, aq_ref, bq_ref, ak_ref, bk_ref, o_ref, q_scr, k_scr, v_scr,
                       s0_scr, s1_scr, *, tq, scale):
    S = k_ref.shape[0]
    s_bufs = (s0_scr, s1_scr)
    inv_dim = 1.0 / AX_HEAD_DIM

    k_scr[...] = _norm_rope(k_ref[...], perm_ref, ak_ref[...], bk_ref[...], inv_dim, 1.0).astype(BF16)
    for u in range(2):
        q_scr[u] = _norm_rope(q_ref[:, u * LANES:(u + 1) * LANES], perm_ref, aq_ref[...], bq_ref[...], inv_dim,
                              scale * LOG2E).astype(BF16)
    v_scr[:, :LANES] = v_ref[...]
    v_scr[:, LANES:] = jnp.ones((S, LANES), BF16)

    def rows(i):
        return pl.ds(i * tq, tq)

    def scores(i, u):
        s_bufs[u][...] = _dot_nt(q_scr[u, rows(i), :], k_scr[...])

    def finish(i, u):
        p, _ = _softmax_numerators(s_bufs[u])
        o = _dot(p, v_scr[...])
        return o[:, :LANES] / o[:, LANES:]

    def emit(i, o0, o1):
        o_ref[rows(i), :LANES] = o0.astype(o_ref.dtype)
        o_ref[rows(i), LANES:] = o1.astype(o_ref.dtype)

    _two_unit_pipeline(S // tq, scores, finish, emit)


def axial_attention(qkv, tabs, B, S):
    R = AX_HEADS // AX_KV_HEADS
    assert R == 2
    tq = min(ATTN_Q_TILE, S)
    kern = functools.partial(_axial_attn_kernel, tq=tq, scale=AX_HEAD_DIM ** -0.5)
    tab = pl.BlockSpec((S, LANES), lambda b, g: (0, 0), pipeline_mode=pl.Buffered(1))
    return pl.pallas_call(
        kern,
        name="axial_attn",
        grid=(B, AX_KV_HEADS),
        in_specs=[
            pl.BlockSpec((S, R * LANES), lambda b, g: (b, g)),
            pl.BlockSpec((S, LANES), lambda b, g: (b, AX_HEADS + g)),
            pl.BlockSpec((S, LANES), lambda b, g: (b, AX_HEADS + AX_KV_HEADS + g)),
            _resident((LANES, LANES)), tab, tab, tab, tab,
        ],
        out_specs=pl.BlockSpec((S, R * LANES), lambda b, g: (b, g)),
        out_shape=jax.ShapeDtypeStruct((B * S, AX_HEADS * AX_HEAD_DIM), BF16),
        scratch_shapes=[
            pltpu.VMEM((R, S, LANES), BF16),
            pltpu.VMEM((S, LANES), BF16),
            pltpu.VMEM((S, 2 * LANES), BF16),
            pltpu.VMEM((tq, S), F32),
            pltpu.VMEM((tq, S), F32),
        ],
        compiler_params=_cparams("parallel", "parallel"),
    )(qkv, qkv, qkv, *tabs)


def _mla_attn_kernel(q_ref, kv_ref, kr_ref, perm_ref, aq_ref, bq_ref, ak_ref, bk_ref, o_ref, q_scr, k_scr, v_scr,
                     s0_scr, s1_scr, *, tq, scale):
    S = kv_ref.shape[0]
    s_bufs = (s0_scr, s1_scr)
    lane = lax.broadcasted_iota(jnp.int32, (1, LANES), 1)
    lo = lane < MLA_NOPE
    inv_dim = 1.0 / MLA_QK

    kr = kr_ref[...]
    krf = kr.astype(F32)
    rope_part = krf * ak_ref[...] + _dot(kr, perm_ref[...]) * bk_ref[...]
    ss_rope = jnp.sum(krf * krf, axis=-1, keepdims=True)
    for hh in range(2):
        kvh = kv_ref[:, hh * LANES:(hh + 1) * LANES].astype(F32)
        ss = jnp.sum(jnp.where(lo, kvh * kvh, 0.0), axis=-1, keepdims=True) + ss_rope
        k = jnp.where(lo, kvh * ak_ref[...], rope_part)
        k_scr[hh] = (k * lax.rsqrt(ss * inv_dim + EPS)).astype(BF16)
        vh = jnp.where(lo, pltpu.roll(kvh, MLA_V, 1), 1.0) if hh == 0 else jnp.where(lo, 1.0, kvh)
        v_scr[hh] = vh.astype(BF16)
        q_scr[hh] = _norm_rope(q_ref[:, hh * LANES:(hh + 1) * LANES], perm_ref, aq_ref[...], bq_ref[...], inv_dim,
                               scale * LOG2E).astype(BF16)

    def rows(i):
        return pl.ds(i * tq, tq)

    def scores(i, u):
        s_bufs[u][...] = _dot_nt(q_scr[u, rows(i), :], k_scr[u])

    def finish(i, u):
        p, _ = _softmax_numerators(s_bufs[u])
        o = _dot(p, v_scr[u])
        return o / pltpu.roll(o, MLA_V, 1)

    def emit(i, o0, o1):
        o_ref[rows(i), :] = jnp.where(lo, o0, o1).astype(o_ref.dtype)

    _two_unit_pipeline(S // tq, scores, finish, emit)


def mla_attention(q, kv, kr, tabs, B, S):
    tq = min(ATTN_Q_TILE, S)
    kern = functools.partial(_mla_attn_kernel, tq=tq, scale=MLA_QK ** -0.5)
    tab = pl.BlockSpec((S, LANES), lambda b, g: (0, 0), pipeline_mode=pl.Buffered(1))
    return pl.pallas_call(
        kern,
        name="mla_attn",
        grid=(B, MLA_HEADS // 2),
        in_specs=[
            pl.BlockSpec((S, 2 * LANES), lambda b, g: (b, g)),
            pl.BlockSpec((S, 2 * LANES), lambda b, g: (b, g)),
            pl.BlockSpec((S, LANES), lambda b, g: (b, 0)),
            _resident((LANES, LANES)), tab, tab, tab, tab,
        ],
        out_specs=pl.BlockSpec((S, LANES), lambda b, g: (b, g)),
        out_shape=jax.ShapeDtypeStruct((B * S, MLA_HEADS * MLA_V), BF16),
        scratch_shapes=[
            pltpu.VMEM((2, S, LANES), BF16),
            pltpu.VMEM((2, S, LANES), BF16),
            pltpu.VMEM((2, S, LANES), BF16),
            pltpu.VMEM((tq, S), F32),
            pltpu.VMEM((tq, S), F32),
        ],
        compiler_params=_cparams("parallel", "parallel"),
    )(q, kv, kr, *tabs)


def _swa_attn_kernel(sink_ref, q_ref, k_ref, v_ref, bias_ref, half_ref, swap_ref, gq_ref, gk_ref, o_ref,
                     q_scr, k_scr, v_scr, s0_scr, s1_scr, *, scale):
    S = k_ref.shape[0]
    span = BLOCK_Q + 2 * SWA_WINDOW
    R = SWA_HEADS // SWA_KV_HEADS
    rows_u = R * BLOCK_Q
    s_bufs = (s0_scr, s1_scr)
    pid = pl.program_id(0)
    lane = lax.broadcasted_iota(jnp.int32, (1, LANES), 1)
    lo = lane < SWA_HEAD_DIM
    hi = jnp.logical_not(lo)

    def seg_norm(x, gain):
        xf = x.astype(F32)
        sq = xf * xf
        sq_hi = sq.astype(BF16)
        sq_lo = (sq - sq_hi.astype(F32)).astype(BF16)
        ss = _dot(sq_hi, half_ref[...]) + _dot(sq_lo, half_ref[...])
        return xf * lax.rsqrt(ss * (1.0 / SWA_HEAD_DIM) + EPS) * gain

    k_scr[...] = seg_norm(k_ref[...], gk_ref[...]).astype(BF16)
    v = v_ref[...].astype(F32)
    v_scr[0] = jnp.where(lo, v, 1.0).astype(BF16)
    v_scr[1] = jnp.where(lo, 1.0, v).astype(BF16)
    for pb in range(R):
        e = pb // (R // 2)
        keep = lo if e == 0 else hi
        qp = seg_norm(q_ref[:, pb * LANES:(pb + 1) * LANES], gq_ref[...]) * (scale * LOG2E)
        qr = _dot(qp.astype(BF16), swap_ref[...])
        for i in range(2):
            qz = jnp.where(keep, qp if i == e else qr, 0.0).astype(BF16)
            r = (pb % (R // 2)) * 2 + i
            for j in range(S // BLOCK_Q):
                q_scr[e, j, r * BLOCK_Q:(r + 1) * BLOCK_Q, :] = qz[j * BLOCK_Q:(j + 1) * BLOCK_Q, :]

    head_of_row = lax.broadcasted_iota(jnp.int32, (rows_u, 1), 0) // BLOCK_Q

    def head_column(ref, e):
        col = jnp.zeros((rows_u, 1), F32)
        for r in range(R):
            col = jnp.where(head_of_row == r, ref[pid * 2 * R + e * R + r], col)
        return col

    sink_cols = [head_column(sink_ref, e) for e in range(2)]

    def rows(j):
        return pl.ds(j * BLOCK_Q, BLOCK_Q)

    def window(j):
        return min(max(j * BLOCK_Q - SWA_WINDOW, 0), S - span)

    def scores(j, e):
        start = window(j)
        bias = bias_ref[e, (j * BLOCK_Q - start) // SWA_WINDOW]
        s_bufs[e][...] = _dot_nt(q_scr[e, j], k_scr[pl.ds(start, span), :]) + bias

    def finish(j, e):
        p, m = _softmax_numerators(s_bufs[e], extra_logit=sink_cols[e])
        o = _dot(p, v_scr[e, pl.ds(window(j), span), :])
        den = pltpu.roll(o, SWA_HEAD_DIM, 1) + jnp.exp2(sink_cols[e] - m)
        return o / den

    def emit(j, o0, o1):
        for e, o in ((0, o0), (1, o1)):
            orot = pltpu.roll(o, SWA_HEAD_DIM, 1)
            for k in range(R // 2):
                pb = e * (R // 2) + k
                even = (o if e == 0 else orot)[2 * k * BLOCK_Q:(2 * k + 1) * BLOCK_Q]
                odd = (o if e == 1 else orot)[(2 * k + 1) * BLOCK_Q:(2 * k + 2) * BLOCK_Q]
                o_ref[rows(j), pb * LANES:(pb + 1) * LANES] = jnp.where(lo, even, odd).astype(o_ref.dtype)

    _two_unit_pipeline(S // BLOCK_Q, scores, finish, emit)


def swa_attention(qkv, slopes, sink, gq, gk, B, S):
    n_steps = SWA_KV_HEADS // 2
    R = SWA_HEADS // SWA_KV_HEADS
    span = BLOCK_Q + 2 * SWA_WINDOW
    qw = SWA_HEADS * SWA_HEAD_DIM // n_steps
    kbase = SWA_HEADS * SWA_HEAD_DIM // LANES
    smem = pl.BlockSpec(memory_space=pltpu.SMEM)
    gain = pl.BlockSpec((1, LANES), lambda g, b: (0, 0))
    kern = functools.partial(_swa_attn_kernel, scale=SWA_HEAD_DIM ** -0.5)
    lane = np.arange(LANES)
    half_ones = jnp.asarray(lane[:, None] // SWA_HEAD_DIM == lane[None, :] // SWA_HEAD_DIM, dtype=BF16)
    swap = _partner_matrix((lane + SWA_HEAD_DIM) % LANES)
    t_s = np.arange(BLOCK_Q)[:, None] - np.arange(span)[None, :]
    dist = np.abs(np.stack([t_s + c * SWA_WINDOW for c in range(3)]))
    bias = jnp.where(jnp.asarray(dist <= SWA_WINDOW)[None], -slopes[:, None, None, None] * jnp.asarray(dist, F32)[None],
                     -jnp.inf)
    bias = bias.reshape(n_steps, 2, R, 3, BLOCK_Q, span).transpose(0, 1, 3, 2, 4, 5)
    bias = bias.reshape(n_steps, 2, 3, R * BLOCK_Q, span)
    return pl.pallas_call(
        kern,
        name="swa_attn",
        grid=(n_steps, B),
        in_specs=[
            smem,
            pl.BlockSpec((S, qw), lambda g, b: (b, g)),
            pl.BlockSpec((S, LANES), lambda g, b: (b, kbase + g)),
            pl.BlockSpec((S, LANES), lambda g, b: (b, kbase + n_steps + g)),
            pl.BlockSpec((None, 2, 3, R * BLOCK_Q, span), lambda g, b: (g, 0, 0, 0, 0)),
            _resident((LANES, LANES)), _resident((LANES, LANES)),
            gain, gain,
        ],
        out_specs=pl.BlockSpec((S, qw), lambda g, b: (b, g)),
        out_shape=jax.ShapeDtypeStruct((B * S, SWA_HEADS * SWA_HEAD_DIM), BF16),
        scratch_shapes=[
            pltpu.VMEM((2, S // BLOCK_Q, R * BLOCK_Q, LANES), BF16),
            pltpu.VMEM((S, LANES), BF16),
            pltpu.VMEM((2, S, LANES), BF16),
            pltpu.VMEM((R * BLOCK_Q, span), F32),
            pltpu.VMEM((R * BLOCK_Q, span), F32),
        ],
        compiler_params=_cparams("parallel", "parallel"),
    )(sink, qkv, qkv, qkv, bias, half_ones, swap, gq, gk)


def _swiglu_accumulate(h, wg_ref, wu_ref, wd_ref, acc_ref, assign_first=False):
    for c in range(wg_ref.shape[1] // FFN_SUB):
        sl = slice(c * FFN_SUB, (c + 1) * FFN_SUB)
        g = _dot(h, wg_ref[:, sl])
        u = _dot(h, wu_ref[:, sl])
        a = (g * jax.nn.sigmoid(g) * u).astype(BF16)
        d = _dot(a, wd_ref[sl, :])
        if assign_first and c == 0:
            acc_ref[...] = d
        else:
            acc_ref[...] += d


def _ple(x, p, gain, win_ref, wgate_ref):
    gate = jax.nn.sigmoid(_dot(_rms(x, gain).astype(BF16), wgate_ref[...]))
    return x + _dot(p.astype(BF16), win_ref[...]) * gate


def _dense_tail_kernel(x_ref, a_ref, wo_ref, g_ref, wg_ref, wu_ref, wd_ref, p_ref, gp_ref, win_ref, wgate_ref, o_ref,
                       acc_scr):
    x = x_ref[...] + _dot(a_ref[...], wo_ref[...])
    acc_scr[...] = x
    _swiglu_accumulate(_rms(x, g_ref[...]).astype(BF16), wg_ref, wu_ref, wd_ref, acc_scr)
    o_ref[...] = _ple(acc_scr[...], p_ref[...], gp_ref[...], win_ref, wgate_ref)


def dense_layer_tail(x, a, w_o, gain, wg, wu, wd, p, layer, ple_gain, w_in, w_gate):
    T, D = x.shape
    K = a.shape[1]
    Fd = wg.shape[1]
    P = p.shape[2]
    row = lambda n: pl.BlockSpec((ROW_TILE, n), lambda i: (i, 0))
    return pl.pallas_call(
        _dense_tail_kernel,
        name="dense_tail",
        grid=(T // ROW_TILE,),
        in_specs=[
            row(D), row(K), _resident((K, D)),
            _resident((1, D)), _resident((D, Fd)), _resident((D, Fd)), _resident((Fd, D)),
            pl.BlockSpec((None, ROW_TILE, P), lambda i: (layer, i, 0)),
            _resident((1, D)), _resident((P, D)), _resident((D, D)),
        ],
        out_specs=row(D),
        out_shape=jax.ShapeDtypeStruct((T, D), F32),
        scratch_shapes=[pltpu.VMEM((ROW_TILE, D), F32)],
        compiler_params=_cparams("parallel"),
    )(x, a, w_o, gain.reshape(1, D), wg, wu, wd, p, ple_gain.reshape(1, D), w_in, w_gate)


def _router_kernel(x_ref, a_ref, wo_ref, g_ref, whi_ref, wlo_ref, b_ref, x1_ref, h_ref, idx_ref, wt_ref, cnt_ref):
    x1 = x_ref[...] + _dot(a_ref[...], wo_ref[...])
    x1_ref[...] = x1
    hf = _rms(x1, g_ref[...])
    h_hi = hf.astype(BF16)
    h_lo = (hf - h_hi.astype(F32)).astype(BF16)
    h_ref[...] = h_hi
    logits = _dot(h_hi, whi_ref[...]) + _dot(h_hi, wlo_ref[...]) + _dot(h_lo, whi_ref[...]) + b_ref[...]
    lane = lax.broadcasted_iota(jnp.int32, logits.shape, 1)
    logits = jnp.where(lane < N_EXPERTS, logits, -jnp.inf)
    m1 = jnp.max(logits, axis=-1, keepdims=True)
    i1 = jnp.min(jnp.where(logits == m1, lane, LANES), axis=-1, keepdims=True)
    rest = jnp.where(lane == i1, -jnp.inf, logits)
    m2 = jnp.max(rest, axis=-1, keepdims=True)
    i2 = jnp.min(jnp.where(rest == m2, lane, LANES), axis=-1, keepdims=True)
    e2 = jnp.exp(m2 - m1)
    w1 = 1.0 / (1.0 + e2)
    w2 = e2 / (1.0 + e2)
    wt_ref[...] = jnp.where(lane == 0, w1, jnp.where(lane == 1, w2, 0.0))
    onehot = jnp.where(jnp.logical_or(lane == i1, lane == i2), 1.0, 0.0)
    tm = onehot.shape[0]
    earlier = (lax.broadcasted_iota(jnp.int32, (tm, tm), 0) > lax.broadcasted_iota(jnp.int32, (tm, tm), 1))
    prefix = _dot(jnp.where(earlier, 1.0, 0.0).astype(BF16), onehot.astype(BF16))
    r1 = jnp.sum(jnp.where(lane == i1, prefix, 0.0), axis=-1, keepdims=True).astype(jnp.int32)
    r2 = jnp.sum(jnp.where(lane == i2, prefix, 0.0), axis=-1, keepdims=True).astype(jnp.int32)
    idx = jnp.where(lane == 0, i1, jnp.where(lane == 1, i2, jnp.where(lane == 2, r1, jnp.where(lane == 3, r2, 0))))
    idx_ref[...] = idx.T[:SUBLANES, :]
    cnt_ref[...] = jnp.broadcast_to(jnp.sum(onehot, axis=0, keepdims=True), cnt_ref.shape)


def moe_router(x, a, w_o, gain, w_hi, w_lo, bias):
    T, D = x.shape
    K = a.shape[1]
    row = lambda n: pl.BlockSpec((ROW_TILE, n), lambda i: (i, 0))
    return pl.pallas_call(
        _router_kernel,
        name="moe_router",
        grid=(T // ROW_TILE,),
        in_specs=[row(D), row(K), _resident((K, D)), _resident((1, D)), _resident((D, LANES)), _resident((D, LANES)),
                  _resident((1, LANES))],
        out_specs=[row(D), row(D), pl.BlockSpec((SUBLANES, ROW_TILE), lambda i: (i, 0)), row(LANES),
                   pl.BlockSpec((SUBLANES, LANES), lambda i: (i, 0))],
        out_shape=[
            jax.ShapeDtypeStruct((T, D), F32),
            jax.ShapeDtypeStruct((T, D), BF16),
            jax.ShapeDtypeStruct((T // ROW_TILE * SUBLANES, ROW_TILE), jnp.int32),
            jax.ShapeDtypeStruct((T, LANES), F32),
            jax.ShapeDtypeStruct((T // ROW_TILE * SUBLANES, LANES), F32),
        ],
        compiler_params=_cparams("parallel"),
    )(x, a, w_o, gain.reshape(1, D), w_hi, w_lo, bias)


def _moe_dispatch_kernel(n_ref, base_ref, h_ref, idx_ref, zeros_ref, hs_ref, dest_ref, loc_scr, sem):
    del zeros_ref
    i = pl.program_id(0)
    e0, e1, r0, r1 = (idx_ref[k:k + 1, :] for k in range(4))
    slot0, slot1, dest0, dest1 = r0, r1, r0, r1
    offs = []
    off = jnp.int32(0)
    for e in range(N_EXPERTS):
        offs.append(off)
        base = base_ref[i * N_EXPERTS + e]
        slot0 = slot0 + jnp.where(e0 == e, off, 0)
        slot1 = slot1 + jnp.where(e1 == e, off, 0)
        dest0 = dest0 + jnp.where(e0 == e, base, 0)
        dest1 = dest1 + jnp.where(e1 == e, base, 0)
        off = off + n_ref[i * N_EXPERTS + e]
    row = lax.broadcasted_iota(jnp.int32, (SUBLANES, e0.shape[1]), 0)
    dest_ref[...] = jnp.where(row == 0, dest0, jnp.where(row == 1, dest1, 0))
    slot = lax.broadcasted_iota(jnp.int32, (loc_scr.shape[0], e0.shape[1]), 0)
    perm = jnp.where(jnp.logical_or(slot == slot0, slot == slot1), 1.0, 0.0).astype(BF16)
    loc_scr[...] = _dot(perm, h_ref[...]).astype(BF16)

    def piece(src_row, dst_row):
        return pltpu.make_async_copy(loc_scr.at[pl.ds(src_row, CHUNK_ALIGN), :],
                                     hs_ref.at[pl.ds(dst_row, CHUNK_ALIGN), :], sem)

    for e in range(N_EXPERTS):
        base = base_ref[i * N_EXPERTS + e]

        def start(g, carry, e=e, base=base):
            piece(pl.multiple_of(offs[e] + g * CHUNK_ALIGN, CHUNK_ALIGN),
                  pl.multiple_of(base + g * CHUNK_ALIGN, CHUNK_ALIGN)).start()
            return carry

        lax.fori_loop(0, n_ref[i * N_EXPERTS + e] // CHUNK_ALIGN, start, 0)

    def wait(g, carry):
        piece(0, 0).wait()
        return carry

    lax.fori_loop(0, off // CHUNK_ALIGN, wait, 0)


def moe_dispatch(h, idx, chunk_rows, chunk_base, n_rows):
    T, D = h.shape
    n_rt = T // ROW_TILE
    loc_rows = TOP_K * ROW_TILE + N_EXPERTS * CHUNK_ALIGN
    grid_spec = pltpu.PrefetchScalarGridSpec(
        num_scalar_prefetch=2,
        grid=(n_rt,),
        in_specs=[
            pl.BlockSpec((ROW_TILE, D), lambda i, n, b: (i, 0)),
            pl.BlockSpec((SUBLANES, ROW_TILE), lambda i, n, b: (i, 0)),
            pl.BlockSpec(memory_space=pl.ANY),
        ],
        out_specs=[
            pl.BlockSpec(memory_space=pl.ANY),
            pl.BlockSpec((SUBLANES, ROW_TILE), lambda i, n, b: (i, 0)),
        ],
        scratch_shapes=[pltpu.VMEM((loc_rows, D), BF16), pltpu.SemaphoreType.DMA],
    )
    return pl.pallas_call(
        _moe_dispatch_kernel,
        name="moe_dispatch",
        grid_spec=grid_spec,
        out_shape=[
            jax.ShapeDtypeStruct((n_rows, D), BF16),
            jax.ShapeDtypeStruct((n_rt * SUBLANES, ROW_TILE), jnp.int32),
        ],
        input_output_aliases={4: 0},
        compiler_params=_cparams("arbitrary"),
    )(chunk_rows, chunk_base, h, idx, jnp.zeros((n_rows, D), BF16))


def _moe_ffn_kernel(te_ref, tv_ref, h_ref, wg_ref, wu_ref, wd_ref, o_ref, acc_scr):
    i = pl.program_id(0)
    f = pl.program_id(1)
    last = pl.num_programs(1) - 1
    valid = tv_ref[i] > 0

    @pl.when(valid)
    def _():
        _swiglu_accumulate(h_ref[...], wg_ref, wu_ref, wd_ref, acc_scr.at[f], assign_first=True)

    @pl.when(jnp.logical_and(valid, f == last))
    def _():
        total = acc_scr[0]
        for c in range(1, acc_scr.shape[0]):
            total = total + acc_scr[c]
        o_ref[...] = total.astype(o_ref.dtype)

    @pl.when(jnp.logical_and(jnp.logical_not(valid), f == last))
    def _():
        o_ref[...] = jnp.zeros_like(o_ref)


def moe_ffn(h_sorted, tile_expert, tile_valid, wg, wu, wd, layer, n_chunks):
    R, D = h_sorted.shape
    Fe = wg.shape[3]
    tf = Fe // n_chunks

    def chunk(i, f, tv):
        return jnp.where(tv[i] > 0, f, n_chunks - 1)

    grid_spec = pltpu.PrefetchScalarGridSpec(
        num_scalar_prefetch=2,
        grid=(R // MOE_ROW_TILE, n_chunks),
        in_specs=[
            pl.BlockSpec((MOE_ROW_TILE, D), lambda i, f, te, tv: (i, 0)),
            pl.BlockSpec((None, None, D, tf), lambda i, f, te, tv: (layer, te[i], 0, chunk(i, f, tv))),
            pl.BlockSpec((None, None, D, tf), lambda i, f, te, tv: (layer, te[i], 0, chunk(i, f, tv))),
            pl.BlockSpec((None, None, tf, D), lambda i, f, te, tv: (layer, te[i], chunk(i, f, tv), 0)),
        ],
        out_specs=pl.BlockSpec((MOE_ROW_TILE, D), lambda i, f, te, tv: (i, 0)),
        scratch_shapes=[pltpu.VMEM((n_chunks, MOE_ROW_TILE, D), F32)],
    )
    return pl.pallas_call(
        _moe_ffn_kernel,
        name="moe_ffn",
        grid_spec=grid_spec,
        out_shape=jax.ShapeDtypeStruct((R, D), BF16),
        compiler_params=_cparams("parallel", "arbitrary"),
    )(tile_expert, tile_valid, h_sorted, wg, wu, wd)


def _moe_tail_kernel(x_ref, y0_ref, y1_ref, wt_ref, p_ref, g_ref, win_ref, wgate_ref, *rest):
    wt = wt_ref[...]
    x = x_ref[...] + wt[:, 0:1] * y0_ref[...].astype(F32) + wt[:, 1:2] * y1_ref[...].astype(F32)
    x = _ple(x, p_ref[...], g_ref[...], win_ref, wgate_ref)
    if len(rest) == 1:
        (o_ref,) = rest
    else:
        gn_ref, wn_ref, o_ref, qkv_ref = rest
        qkv_ref[...] = _dot(_rms(x, gn_ref[...]).astype(BF16), wn_ref[...]).astype(qkv_ref.dtype)
    o_ref[...] = x


def moe_layer_tail(x, y0, y1, wt, p, layer, gain, w_in, w_gate, next_proj=None):
    T, D = x.shape
    P = p.shape[2]
    row = lambda n: pl.BlockSpec((ROW_TILE, n), lambda i: (i, 0))
    ins = [x, y0, y1, wt, p, gain.reshape(1, D), w_in, w_gate]
    in_specs = [row(D), row(D), row(D), row(LANES), pl.BlockSpec((None, ROW_TILE, P), lambda i: (layer, i, 0)),
                _resident((1, D)), _resident((P, D)), _resident((D, D))]
    out_specs = row(D)
    out_shape = jax.ShapeDtypeStruct((T, D), F32)
    if next_proj is not None:
        gn, wn = next_proj
        N = wn.shape[1]
        ins += [gn.reshape(1, D), wn]
        in_specs += [_resident((1, D)), _resident((D, N))]
        out_specs = [out_specs, row(N)]
        out_shape = [out_shape, jax.ShapeDtypeStruct((T, N), BF16)]
    return pl.pallas_call(
        _moe_tail_kernel,
        name="moe_tail",
        grid=(T // ROW_TILE,),
        in_specs=in_specs,
        out_specs=out_specs,
        out_shape=out_shape,
        compiler_params=_cparams("parallel"),
    )(*ins)


def _rope_cos_sin(pos, dim, theta):
    inv = theta ** (-jnp.arange(0, dim, 2, dtype=F32) / dim)
    ang = pos.astype(F32)[:, None] * inv[None, :]
    return jnp.cos(ang), jnp.sin(ang)


def _fold_tables(gain_lanes, cos_lanes, sin_lanes, partner):
    return gain_lanes[None, :] * cos_lanes, gain_lanes[partner][None, :] * sin_lanes


def _partner_matrix(partner):
    m = np.zeros((LANES, LANES), np.float32)
    m[partner, np.arange(LANES)] = 1.0
    return jnp.asarray(m, dtype=BF16)


def _axial_tables(S, q_gain, k_gain):
    pos = jnp.arange(S)
    cr, sr = _rope_cos_sin(pos // GRID_W, AX_HEAD_DIM // 2, AX_THETA)
    cc, sc = _rope_cos_sin(pos % GRID_W, AX_HEAD_DIM // 2, AX_THETA)
    cos = jnp.concatenate([cr, cr, cc, cc], axis=1)
    sin = jnp.concatenate([-sr, sr, -sc, sc], axis=1)
    lane = np.arange(LANES)
    partner = np.where(lane % 64 < 32, lane + 32, lane - 32)
    return (_partner_matrix(partner),) + _fold_tables(q_gain, cos, sin, partner) + _fold_tables(k_gain, cos, sin, partner)


def _mla_tables(S, q_gain, k_gain):
    c, s = _rope_cos_sin(jnp.arange(S), MLA_ROPE, MLA_THETA)
    pad = LANES - MLA_QK
    cos = jnp.concatenate([jnp.ones((S, MLA_NOPE), F32), c, c, jnp.ones((S, pad), F32)], axis=1)
    sin = jnp.concatenate([jnp.zeros((S, MLA_NOPE), F32), -s, s, jnp.zeros((S, pad), F32)], axis=1)
    lane = np.arange(LANES)
    half = MLA_ROPE // 2
    partner = np.where((lane >= MLA_NOPE) & (lane < MLA_NOPE + half), lane + half,
                       np.where((lane >= MLA_NOPE + half) & (lane < MLA_QK), lane - half, lane))
    zpad = jnp.zeros((pad,), F32)
    gq = jnp.concatenate([q_gain, zpad])
    gk = jnp.concatenate([k_gain, zpad])
    return (_partner_matrix(partner),) + _fold_tables(gq, cos, sin, partner) + _fold_tables(gk, cos, sin, partner)


def _moe_layout(cnt, n_rt, n_tiles):
    tm = MOE_ROW_TILE
    cnt = cnt.reshape(n_rt, SUBLANES, LANES)[:, 0, :N_EXPERTS].astype(jnp.int32)
    chunk_rows = ((cnt + CHUNK_ALIGN - 1) // CHUNK_ALIGN) * CHUNK_ALIGN
    rt = jnp.arange(n_rt)
    before = jnp.sum(jnp.where((rt[None, :] < rt[:, None])[:, :, None], chunk_rows[None, :, :], 0), axis=1)
    region = ((jnp.sum(chunk_rows, axis=0) + tm - 1) // tm) * tm
    ex = jnp.arange(N_EXPERTS)
    ends = jnp.sum(jnp.where(ex[None, :] <= ex[:, None], region[None, :], 0), axis=1)
    chunk_base = (ends - region)[None, :] + before
    tile_start = jnp.arange(n_tiles, dtype=jnp.int32) * tm
    tile_expert = jnp.minimum(jnp.sum((tile_start[:, None] >= ends[None, :]).astype(jnp.int32), axis=1), N_EXPERTS - 1)
    tile_valid = (tile_start < ends[-1]).astype(jnp.int32)
    return chunk_rows.reshape(-1), chunk_base.reshape(-1), tile_expert, tile_valid


def kernel(x, p, attn_norm, ffn_norm, ple_norm, ple_w_in, ple_w_gate, mla_w_down, mla_q_norm, mla_w_uq, mla_kv_norm, mla_w_ukv, mla_q_gain, mla_k_gain, mla_w_o, swa_w_qkv, swa_q_gain, swa_k_gain, swa_sink, swa_w_o, ax_w_qkv, ax_q_gain, ax_k_gain, ax_w_o, ffn_w_gate, ffn_w_up, ffn_w_down, moe_w_router, moe_b_router, moe_w_gate, moe_w_up, moe_w_down):
    B, S, D = x.shape
    depth = p.shape[0]
    T = B * S
    xt = x.reshape(T, D)
    bf = lambda a: a.astype(BF16)
    n_rt = T // ROW_TILE
    n_moe_tiles = (TOP_K * T + n_rt * N_EXPERTS * (CHUNK_ALIGN - 1)) // MOE_ROW_TILE + N_EXPERTS
    slopes = jnp.asarray(2.0 ** (-8.0 * np.arange(1, SWA_HEADS + 1) / SWA_HEADS) * LOG2E, dtype=F32)
    p3 = p.reshape(depth, T, -1)
    moe_wg, moe_wu, moe_wd = bf(moe_w_gate), bf(moe_w_up), bf(moe_w_down)

    def plain_proj(i):
        if i >= depth or i % N_MIXERS == 0:
            return None
        w = swa_w_qkv if i % N_MIXERS == 1 else ax_w_qkv
        return attn_norm[i], bf(w[i // N_MIXERS])

    qkv_next = None
    for i in range(depth):
        kind = i % N_MIXERS
        j = i // N_MIXERS
        if kind == 0:
            wd = mla_w_down[j]
            zc = lambda n: jnp.zeros((D, n), F32)
            wd = jnp.concatenate([wd[:, :MLA_Q_RANK + MLA_KV_RANK], zc(MLA_NOPE), wd[:, MLA_Q_RANK + MLA_KV_RANK:],
                                  zc(LANES - MLA_QK)], axis=1)
            wuq = mla_w_uq[j].reshape(MLA_Q_RANK, MLA_HEADS, MLA_QK)
            wuq = jnp.pad(wuq, ((0, 0), (0, 0), (0, LANES - MLA_QK))).reshape(MLA_Q_RANK, MLA_HEADS * LANES)
            q, kv, kr = mla_proj(xt, attn_norm[i], bf(wd), mla_q_norm[j], mla_kv_norm[j], bf(wuq), bf(mla_w_ukv[j]))
            o = mla_attention(q, kv, kr, _mla_tables(S, mla_q_gain[j], mla_k_gain[j]), B, S)
            w_o = mla_w_o[j]
        elif kind == 1:
            qkv = qkv_next if qkv_next is not None else norm_proj(xt, *plain_proj(i))
            gq = jnp.tile(swa_q_gain[j], 2).reshape(1, LANES)
            gk = jnp.tile(swa_k_gain[j], 2).reshape(1, LANES)
            o = swa_attention(qkv, slopes, swa_sink[j].astype(F32) * LOG2E, gq, gk, B, S)
            w_o = swa_w_o[j]
        else:
            qkv = qkv_next if qkv_next is not None else norm_proj(xt, *plain_proj(i))
            o = axial_attention(qkv, _axial_tables(S, ax_q_gain[j], ax_k_gain[j]), B, S)
            w_o = ax_w_o[j]
        qkv_next = None
        f = i // 2
        if i % 2 == 0:
            xt = dense_layer_tail(xt, o, bf(w_o), ffn_norm[i], bf(ffn_w_gate[f]), bf(ffn_w_up[f]), bf(ffn_w_down[f]),
                                  p3, i, ple_norm[i], bf(ple_w_in[i]), bf(ple_w_gate[i]))
        else:
            wr = jnp.pad(moe_w_router[f], ((0, 0), (0, LANES - N_EXPERTS)))
            wr_hi = bf(wr)
            wr_lo = bf(wr - wr_hi.astype(F32))
            br = jnp.pad(moe_b_router[f].astype(F32), (0, LANES - N_EXPERTS)).reshape(1, LANES)
            xt, h, idx, wt, cnt = moe_router(xt, o, bf(w_o), ffn_norm[i], wr_hi, wr_lo, br)
            chunk_rows, chunk_base, tile_expert, tile_valid = _moe_layout(cnt, n_rt, n_moe_tiles)
            h_sorted, dest = moe_dispatch(h, idx, chunk_rows, chunk_base, n_moe_tiles * MOE_ROW_TILE)
            y = moe_ffn(h_sorted, tile_expert, tile_valid, moe_wg, moe_wu, moe_wd, f, 2)
            dest = dest.reshape(n_rt, SUBLANES, ROW_TILE)
            y0 = jnp.take(y, dest[:, 0, :].reshape(T), axis=0, mode="clip")
            y1 = jnp.take(y, dest[:, 1, :].reshape(T), axis=0, mode="clip")
            nxt = plain_proj(i + 1)
            out = moe_layer_tail(xt, y0, y1, wt, p3, i, ple_norm[i], bf(ple_w_in[i]), bf(ple_w_gate[i]), nxt)
            xt, qkv_next = out if nxt is not None else (out, None)
    return xt.reshape(B, S, D)
```
